```python
import math
import jax, jax.numpy as jnp
from jax import lax
import numpy as np

D_MODEL = 1024
BATCH = 8
SEQ = 2048
DEPTH = 2
DEC_BATCH = 128
DEC_SEQ = 1
PAST_LEN = 16384
PAGE_SIZE = 128

N_BRANCH = 4
BRANCH_WIDTH = D_MODEL // 4
W_POOL = BRANCH_WIDTH
POOL_WINDOWS = (2, 4, 8, 16)
POOL_GROUP = W_POOL // len(POOL_WINDOWS)
POOL_BUF = max(POOL_WINDOWS) - 1
W_SSD = BRANCH_WIDTH
SSD_HEAD_DIM = 64
SSD_HEADS = W_SSD // SSD_HEAD_DIM
SSD_GROUPS = 2
SSD_HEADS_PER_GROUP = SSD_HEADS // SSD_GROUPS
SSD_STATE = 64
SSD_CONV = 4
SSD_CONV_DIM = W_SSD + 2 * SSD_GROUPS * SSD_STATE
W_MLSTM = BRANCH_WIDTH
M_HEAD_DIM = 64
M_HEADS = W_MLSTM // M_HEAD_DIM
W_S5 = BRANCH_WIDTH
S5_GROUP_CH = 16
S5_GROUPS = W_S5 // S5_GROUP_CH
S5_STATE = 64
CHUNK = 128
ALPHA = (2.0 * DEPTH) ** 0.25
BETA = (8.0 * DEPTH) ** -0.25
LN_EPS = 1e-5
IN_SPLITS = (W_POOL, W_POOL,
             SSD_CONV_DIM, SSD_HEADS, W_SSD,
             W_MLSTM, W_MLSTM, W_MLSTM, M_HEADS, M_HEADS,
             W_MLSTM, W_MLSTM,
             W_S5, W_S5,
             N_BRANCH * D_MODEL)
D_IN = sum(IN_SPLITS)
N_STATES = 8

kernel_name = 'hybrid_pool_ssd_mlstm_s5_step'


def _split_columns(t, sizes):
    parts = []
    start = 0
    for s in sizes:
        parts.append(t[..., start:start + s])
        start += s
    return parts


def _layer_norm(x, g, b):
    xf = x.astype(jnp.float32)
    mu = jnp.mean(xf, axis=-1, keepdims=True)
    var = jnp.mean(jnp.square(xf - mu), axis=-1, keepdims=True)
    return (xf - mu) * lax.rsqrt(var + LN_EPS) * g + b


def _chunk_len(L):
    return CHUNK if L % CHUNK == 0 else L


def _pool_mixer(u, buf, pos0, w_pool, scale):
    bsz, L, _ = u.shape
    ext = jnp.concatenate([buf.astype(u.dtype), u], axis=1)
    cs = jnp.cumsum(ext.astype(jnp.float32), axis=1)
    cs = jnp.concatenate([jnp.zeros((bsz, 1, W_POOL), jnp.float32), cs], axis=1)
    uf = u.astype(jnp.float32)
    pos = pos0 + jnp.arange(L)
    outs = []
    for gi, w in enumerate(POOL_WINDOWS):
        sl = slice(gi * POOL_GROUP, (gi + 1) * POOL_GROUP)
        hi = cs[:, POOL_BUF + 1:POOL_BUF + 1 + L, sl]
        lo = cs[:, POOL_BUF + 1 - w:POOL_BUF + 1 - w + L, sl]
        cnt = jnp.minimum(pos + 1, w).astype(jnp.float32)[None, :, None]
        d = (hi - lo) / cnt - uf[..., sl]
        outs.append(jnp.einsum('blc,ce->ble', d, w_pool[gi]))
    y = jnp.concatenate(outs, axis=-1) * scale
    return y, ext[:, -POOL_BUF:]


def _causal_dwconv(u, buf, w, b):
    L = u.shape[1]
    ext = jnp.concatenate([buf.astype(u.dtype), u], axis=1)
    acc = b + ext[:, 0:L] * w[0]
    for k in range(1, SSD_CONV):
        acc = acc + ext[:, k:k + L] * w[k]
    return jax.nn.silu(acc), ext[:, -(SSD_CONV - 1):]


def _ssd_chunked(xh, Bg, Cg, dt, A, d_skip, h0):
    f32 = jnp.float32
    bsz, L, H, P = xh.shape
    Q = _chunk_len(L)
    nc = L // Q
    xh = xh.astype(f32)
    Bh = jnp.repeat(Bg.astype(f32), SSD_HEADS_PER_GROUP, axis=2)
    Ch = jnp.repeat(Cg.astype(f32), SSD_HEADS_PER_GROUP, axis=2)

    def chunks(t):
        return t.reshape((bsz, nc, Q) + t.shape[2:])

    xc, Bc, Cc, dtc = chunks(xh), chunks(Bh), chunks(Ch), chunks(dt)
    acs = jnp.cumsum(dtc * A, axis=2)
    causal = jnp.tril(jnp.ones((Q, Q), bool))
    seg = acs[:, :, :, None, :] - acs[:, :, None, :, :]
    lmat = jnp.exp(jnp.where(causal[None, None, :, :, None], seg, -jnp.inf))
    gmat = jnp.einsum('bcthn,bcshn->bctsh', Cc, Bc) * lmat * dtc[:, :, None, :, :]
    y_diag = jnp.einsum('bctsh,bcshp->bcthp', gmat, xc)
    decay_end = jnp.exp(acs[:, :, -1:, :] - acs)
    chunk_states = jnp.einsum('bcshn,bcsh,bcshp->bchpn', Bc, decay_end * dtc, xc)
    chunk_decay = jnp.exp(acs[:, :, -1, :])

    def step(h, inp):
        dec, st = inp
        return dec[:, :, None, None] * h + st, h

    h_final, h_prev = lax.scan(step, h0.astype(f32),
                               (jnp.moveaxis(chunk_decay, 1, 0), jnp.moveaxis(chunk_states, 1, 0)))
    h_prev = jnp.moveaxis(h_prev, 0, 1)
    y_off = jnp.einsum('bcthn,bchpn,bcth->bcthp', Cc, h_prev, jnp.exp(acs))
    y = (y_diag + y_off).reshape(bsz, L, H, P) + d_skip[:, None] * xh
    return y, h_final


def _mlstm_chunk(carry, inp):
    C, n, m = carry
    q, k, v, ig, lf = inp
    Q = q.shape[1]
    b = jnp.cumsum(lf, axis=1)
    causal = jnp.tril(jnp.ones((Q, Q), bool))
    dlog = b[:, :, None, :] - b[:, None, :, :] + ig[:, None, :, :]
    dlog = jnp.where(causal[None, :, :, None], dlog, -jnp.inf)
    inter = b + m[:, None, :]
    m_t = jnp.maximum(inter, jnp.max(dlog, axis=2))
    dw = jnp.exp(dlog - m_t[:, :, None, :])
    wi = jnp.exp(inter - m_t)
    s = jnp.einsum('bthd,bshd->btsh', q, k) * dw
    num = jnp.einsum('btsh,bshe->bthe', s, v) + wi[..., None] * jnp.einsum('bthd,bhde->bthe', q, C)
    den = jnp.sum(s, axis=2) + wi * jnp.einsum('bthd,bhd->bth', q, n)
    h = num / jnp.maximum(jnp.abs(den), jnp.exp(-m_t))[..., None]
    m_new = m_t[:, -1]
    wc = jnp.exp(b[:, -1:] - b + ig - m_new[:, None])
    decay = jnp.exp(b[:, -1] + m - m_new)
    C_new = decay[..., None, None] * C + jnp.einsum('bsh,bshd,bshe->bhde', wc, k, v)
    n_new = decay[..., None] * n + jnp.einsum('bsh,bshd->bhd', wc, k)
    return (C_new, n_new, m_new), h


def _mlstm_chunked(q, k, v, ig, lf, C0, n0, m0):
    bsz, L, H, Dh = q.shape
    Q = _chunk_len(L)
    nc = L // Q

    def to_chunks(t):
        return jnp.moveaxis(t.reshape((bsz, nc, Q) + t.shape[2:]), 1, 0)

    xs = (to_chunks(q), to_chunks(k), to_chunks(v), to_chunks(ig), to_chunks(lf))
    (C, n, m), h = lax.scan(_mlstm_chunk, (C0, n0, m0), xs)
    h = jnp.moveaxis(h, 0, 1).reshape(bsz, L, H, Dh)
    return h, C, n, m


def _s5_combine(e1, e2):
    a1, b1 = e1
    a2, b2 = e2
    return a2 * a1, a2 * b1 + b2


def _s5_scan(u, h0_re, h0_im, lam_re, lam_im, b_re, b_im, c_re, c_im, log_dt, d_skip):
    f32 = jnp.float32
    bsz, L, _ = u.shape
    lam = lax.complex(lam_re.astype(f32), lam_im.astype(f32))
    step = jnp.exp(log_dt.astype(f32))[:, None]
    lam_bar = jnp.exp(lam * step)
    b_bar = ((lam_bar - 1.0) / lam)[..., None] * lax.complex(b_re.astype(f32), b_im.astype(f32))
    c_mat = lax.complex(c_re.astype(f32), c_im.astype(f32))
    uf = u.astype(f32)
    ug = uf.reshape(bsz, L, S5_GROUPS, S5_GROUP_CH).astype(jnp.complex64)
    bu = jnp.einsum('blgc,gpc->blgp', ug, b_bar)
    h0 = lax.complex(h0_re.astype(f32), h0_im.astype(f32))
    bu = bu.at[:, 0].add(lam_bar * h0)
    a = jnp.broadcast_to(lam_bar, bu.shape)
    _, h = lax.associative_scan(_s5_combine, (a, bu), axis=1)
    y = jnp.real(jnp.einsum('blgp,gcp->blgc', h, c_mat)).reshape(bsz, L, W_S5) + d_skip * uf
    h_last = h[:, -1]
    return y, jnp.real(h_last), jnp.imag(h_last)


def _decoder_layer(x, states, pos0, lw):
    (pool_buf, conv_buf, ssd_h, m_C, m_n, m_m, s5_re, s5_im) = states
    (w_in, w_pool, pool_scale, conv_w, conv_b, dt_bias, a_log, d_ssd,
     ig_bias, fg_bias, lam_re, lam_im, b_re, b_im, c_re, c_im, log_dt, d_s5, w_glu,
     w_br, w_out, ln_g, ln_b) = lw
    f32 = jnp.float32
    bsz, L, _ = x.shape
    proj = jnp.einsum('bld,de->ble', x, w_in)
    (pool_u, pool_z, xbc, dt_raw, ssd_z, q, k, v, ig_raw, fg_raw, og, m_z,
     s5_u, s5_z, gate_pre) = _split_columns(proj, IN_SPLITS)

    y_pool, new_pool = _pool_mixer(pool_u, pool_buf, pos0, w_pool, pool_scale)

    xbc_c, new_conv = _causal_dwconv(xbc, conv_buf, conv_w, conv_b)
    xs, Bm, Cm = _split_columns(xbc_c, (W_SSD, SSD_GROUPS * SSD_STATE, SSD_GROUPS * SSD_STATE))
    dt = jax.nn.softplus((dt_raw + dt_bias).astype(f32))
    y_ssd, new_ssd = _ssd_chunked(xs.reshape(bsz, L, SSD_HEADS, SSD_HEAD_DIM),
                                  Bm.reshape(bsz, L, SSD_GROUPS, SSD_STATE),
                                  Cm.reshape(bsz, L, SSD_GROUPS, SSD_STATE),
                                  dt, -jnp.exp(a_log.astype(f32)), d_ssd.astype(f32), ssd_h)
    y_ssd = y_ssd.reshape(bsz, L, W_SSD)

    qh = q.reshape(bsz, L, M_HEADS, M_HEAD_DIM).astype(f32)
    kh = k.reshape(bsz, L, M_HEADS, M_HEAD_DIM).astype(f32) * (M_HEAD_DIM ** -0.5)
    vh = v.reshape(bsz, L, M_HEADS, M_HEAD_DIM).astype(f32)
    ig = (ig_raw + ig_bias).astype(f32)
    lf = jax.nn.log_sigmoid((fg_raw + fg_bias).astype(f32))
    h_m, new_C, new_n, new_m = _mlstm_chunked(qh, kh, vh, ig, lf, m_C.astype(f32),
                                              m_n.astype(f32), m_m.astype(f32))
    y_m = jax.nn.sigmoid(og.astype(f32)) * h_m.reshape(bsz, L, W_MLSTM)

    y_s5, new_re, new_im = _s5_scan(s5_u, s5_re, s5_im, lam_re, lam_im, b_re, b_im,
                                    c_re, c_im, log_dt, d_s5)
    glu = jnp.einsum('blw,we->ble', jax.nn.gelu(y_s5), w_glu)
    y_s5 = glu[..., :W_S5] * jax.nn.sigmoid(glu[..., W_S5:])

    branches = (y_pool * jax.nn.silu(pool_z), y_ssd * jax.nn.silu(ssd_z),
                y_m * jax.nn.silu(m_z), y_s5 * jax.nn.silu(s5_z))
    gates = jax.nn.sigmoid(gate_pre.astype(f32)).reshape(bsz, L, N_BRANCH, D_MODEL)
    merged = gates[:, :, 0] * jnp.einsum('blw,wd->bld', branches[0], w_br[0])
    for bi in range(1, N_BRANCH):
        merged = merged + gates[:, :, bi] * jnp.einsum('blw,wd->bld', branches[bi], w_br[bi])
    out = jnp.einsum('bld,de->ble', merged, w_out)
    y = _layer_norm(ALPHA * x.astype(f32) + out, ln_g, ln_b).astype(x.dtype)
    return y, (new_pool, new_conv, new_ssd, new_C, new_n, new_m, new_re, new_im)


def _zero_states(bsz, dtype):
    f32 = jnp.float32
    return (jnp.zeros((bsz, POOL_BUF, W_POOL), dtype),
            jnp.zeros((bsz, SSD_CONV - 1, SSD_CONV_DIM), dtype),
            jnp.zeros((bsz, SSD_HEADS, SSD_HEAD_DIM, SSD_STATE), f32),
            jnp.zeros((bsz, M_HEADS, M_HEAD_DIM, M_HEAD_DIM), f32),
            jnp.zeros((bsz, M_HEADS, M_HEAD_DIM), f32),
            jnp.zeros((bsz, M_HEADS), f32),
            jnp.zeros((bsz, S5_GROUPS, S5_STATE), f32),
            jnp.zeros((bsz, S5_GROUPS, S5_STATE), f32))


def _stack_states(per_layer):
    return [jnp.stack([s[i] for s in per_layer], axis=0) for i in range(N_STATES)]


def setup_inputs(seed: int = 0) -> dict:
    key = jax.random.key(seed)
    ks = jax.random.split(key, 33)
    f32 = jnp.float32

    def nrm(k, shape, s):
        return s * jax.random.normal(k, shape, f32)

    dt0 = jnp.exp(jax.random.uniform(ks[15], (DEPTH, SSD_HEADS), f32,
                                     minval=math.log(1e-3), maxval=math.log(1e-1)))
    lam_im0 = jnp.broadcast_to(jnp.pi * jnp.arange(S5_STATE, dtype=f32), (DEPTH, S5_GROUPS, S5_STATE))
    return {
        'x_prompt': nrm(ks[0], (BATCH, SEQ, D_MODEL), 1.0),
        'x_sample': nrm(ks[1], (DEC_BATCH, DEC_SEQ, D_MODEL), 1.0),
        'state_pool': nrm(ks[2], (DEPTH, DEC_BATCH, POOL_BUF, W_POOL), 1.0),
        'state_ssd_conv': nrm(ks[3], (DEPTH, DEC_BATCH, SSD_CONV - 1, SSD_CONV_DIM), 1.0),
        'state_ssd': nrm(ks[4], (DEPTH, DEC_BATCH, SSD_HEADS, SSD_HEAD_DIM, SSD_STATE), 0.5),
        'state_mlstm_C': nrm(ks[5], (DEPTH, DEC_BATCH, M_HEADS, M_HEAD_DIM, M_HEAD_DIM), 0.5),
        'state_mlstm_n': nrm(ks[6], (DEPTH, DEC_BATCH, M_HEADS, M_HEAD_DIM), 0.5),
        'state_mlstm_m': nrm(ks[7], (DEPTH, DEC_BATCH, M_HEADS), 1.0),
        'state_s5_re': nrm(ks[8], (DEPTH, DEC_BATCH, S5_GROUPS, S5_STATE), 0.5),
        'state_s5_im': nrm(ks[9], (DEPTH, DEC_BATCH, S5_GROUPS, S5_STATE), 0.5),
        'w_in': nrm(ks[10], (DEPTH, D_MODEL, D_IN), D_MODEL ** -0.5),
        'w_pool': nrm(ks[11], (DEPTH, len(POOL_WINDOWS), POOL_GROUP, POOL_GROUP), POOL_GROUP ** -0.5),
        'pool_scale': 1.0 + nrm(ks[12], (DEPTH, W_POOL), 0.02),
        'conv_w': nrm(ks[13], (DEPTH, SSD_CONV, SSD_CONV_DIM), SSD_CONV ** -0.5),
        'conv_b': nrm(ks[14], (DEPTH, SSD_CONV_DIM), 0.02),
        'dt_bias': dt0 + jnp.log(-jnp.expm1(-dt0)),
        'a_log': jnp.log(jax.random.uniform(ks[16], (DEPTH, SSD_HEADS), f32, minval=1.0, maxval=16.0)),
        'd_ssd': 1.0 + nrm(ks[17], (DEPTH, SSD_HEADS), 0.1),
        'ig_bias': nrm(ks[18], (DEPTH, M_HEADS), 0.1),
        'fg_bias': jnp.linspace(3.0, 6.0, M_HEADS, dtype=f32)[None, :] + nrm(ks[19], (DEPTH, M_HEADS), 0.1),
        'lam_re': -0.5 + nrm(ks[20], (DEPTH, S5_GROUPS, S5_STATE), 0.01),
        'lam_im': lam_im0 + nrm(ks[21], (DEPTH, S5_GROUPS, S5_STATE), 0.01),
        'b_re': nrm(ks[22], (DEPTH, S5_GROUPS, S5_STATE, S5_GROUP_CH), (2.0 * S5_GROUP_CH) ** -0.5),
        'b_im': nrm(ks[23], (DEPTH, S5_GROUPS, S5_STATE, S5_GROUP_CH), (2.0 * S5_GROUP_CH) ** -0.5),
        'c_re': nrm(ks[24], (DEPTH, S5_GROUPS, S5_GROUP_CH, S5_STATE), (2.0 * S5_STATE) ** -0.5),
        'c_im': nrm(ks[25], (DEPTH, S5_GROUPS, S5_GROUP_CH, S5_STATE), (2.0 * S5_STATE) ** -0.5),
        'log_dt': jax.random.uniform(ks[26], (DEPTH, S5_GROUPS), f32,
                                     minval=math.log(1e-3), maxval=math.log(1e-1)),
        'd_s5': nrm(ks[27], (DEPTH, W_S5), 1.0),
        'w_glu': nrm(ks[28], (DEPTH, W_S5, 2 * W_S5), W_S5 ** -0.5),
        'w_br': nrm(ks[29], (DEPTH, N_BRANCH, BRANCH_WIDTH, D_MODEL), BETA * BRANCH_WIDTH ** -0.5),
        'w_out': nrm(ks[30], (DEPTH, D_MODEL, D_MODEL), BETA * D_MODEL ** -0.5),
        'ln_g': 1.0 + nrm(ks[31], (DEPTH, D_MODEL), 0.02),
        'ln_b': nrm(ks[32], (DEPTH, D_MODEL), 0.02),
    }


def reference(x_prompt, x_sample, state_pool, state_ssd_conv, state_ssd, state_mlstm_C,
              state_mlstm_n, state_mlstm_m, state_s5_re, state_s5_im,
              w_in, w_pool, pool_scale, conv_w, conv_b, dt_bias, a_log, d_ssd,
              ig_bias, fg_bias, lam_re, lam_im, b_re, b_im, c_re, c_im, log_dt, d_s5, w_glu,
              w_br, w_out, ln_g, ln_b):
    yp = x_prompt
    ys = x_sample
    new_p = []
    new_s = []
    for l in range(DEPTH):
        lw = (w_in[l], w_pool[l], pool_scale[l], conv_w[l], conv_b[l], dt_bias[l], a_log[l], d_ssd[l],
              ig_bias[l], fg_bias[l], lam_re[l], lam_im[l], b_re[l], b_im[l], c_re[l], c_im[l],
              log_dt[l], d_s5[l], w_glu[l], w_br[l], w_out[l], ln_g[l], ln_b[l])
        yp, sp = _decoder_layer(yp, _zero_states(yp.shape[0], yp.dtype), 0, lw)
        ss_in = (state_pool[l], state_ssd_conv[l], state_ssd[l], state_mlstm_C[l],
                 state_mlstm_n[l], state_mlstm_m[l], state_s5_re[l], state_s5_im[l])
        ys, ss = _decoder_layer(ys, ss_in, PAST_LEN, lw)
        new_p.append(sp)
        new_s.append(ss)
    pool_p, conv_p, ssd_p, C_p, n_p, m_p, re_p, im_p = _stack_states(new_p)
    pool_s, conv_s, ssd_s, C_s, n_s, m_s, re_s, im_s = _stack_states(new_s)
    return (yp, ys, pool_p, pool_s, conv_p, conv_s, ssd_p, ssd_s, C_p, C_s,
            n_p, n_s, m_p, m_s, re_p, re_s, im_p, im_s)
```

```python
import functools

import jax
import jax.numpy as jnp
from jax import lax
from jax.experimental import pallas as pl
from jax.experimental.pallas import tpu as pltpu

F32 = jnp.float32
BF16 = jnp.bfloat16

D_MODEL = 1024
DEPTH = 2
PAST_LEN = 16384
BW = 256
POOL_WINDOWS = (2, 4, 8, 16)
POOL_BUF = 15
HEADS = 4
HD = 64
SSD_STATE = 64
CONV_K = 4
CONV_DIM = 512
S5_GROUPS = 16
S5_CH = 16
S5_STATE = 64
S5_W = S5_GROUPS * S5_STATE
CHUNK = 128
ALPHA = (2.0 * DEPTH) ** 0.25
LN_EPS = 1e-5

SUBLANES = 8
LANES = 128
VMEM_LIMIT = 56 * 1024 * 1024

PRE_W = 1920
POST_W = 5376
O_PU, O_XBC, O_Q, O_K, O_V, O_SU, O_SM = 0, 256, 768, 1024, 1280, 1536, 1792
O_Z, O_OG, O_GATE = 0, 1024, 1280

R_LAM = 0
R_P = 8
R_M = 24
LAMP_ROWS = R_M + 3 * 16

V_LNG, V_LNB, V_MISC, V_CONVB, V_CONVW = 0, 1, 2, 3, 4


def _dot(a, b):
    return jnp.dot(a.astype(BF16), b.astype(BF16), preferred_element_type=F32)


def _dot_nt(a, b):
    return lax.dot_general(a.astype(BF16), b.astype(BF16), (((1,), (1,)), ((), ())),
                           preferred_element_type=F32)


def _softplus(x):
    return jnp.logaddexp(x, 0.0)


def _silu(x):
    return x * jax.nn.sigmoid(x)


def _expand_heads(cols, width=HD):
    rows = cols[0].shape[0]
    lane = lax.broadcasted_iota(jnp.int32, (rows, HEADS * width), 1)
    out = jnp.broadcast_to(cols[HEADS - 1], (rows, HEADS * width))
    for h in range(HEADS - 2, -1, -1):
        out = jnp.where(lane < (h + 1) * width, jnp.broadcast_to(cols[h], (rows, HEADS * width)), out)
    return out


def _pool_select(w2, w4, w8, w16):
    lane = lax.broadcasted_iota(jnp.int32, w2.shape, 1)
    return jnp.where(lane < 64, w2, jnp.where(lane < 128, w4, jnp.where(lane < 192, w8, w16)))


def _pool_window_row():
    lane = lax.broadcasted_iota(jnp.int32, (1, BW), 1)
    return jnp.where(lane < 64, 2, jnp.where(lane < 128, 4, jnp.where(lane < 192, 8, 16)))


def _small_block(sm, vec_ref):
    bias = vec_ref[V_MISC:V_MISC + 1, 768:896]
    alog = vec_ref[V_MISC:V_MISC + 1, 896:1024]
    v = sm + bias
    sp = _softplus(v)
    lsg = -_softplus(-v)
    a_row = -jnp.exp(alog)
    return v, sp, lsg, a_row


def _merge_tail(x, zs, ys, gate_fn, wbr_ref, wout_ref, vec_ref):
    merged = None
    for b in range(4):
        br = ys[b] * _silu(zs[b])
        pb = _dot(br, wbr_ref[b])
        gb = jax.nn.sigmoid(gate_fn(b))
        merged = gb * pb if merged is None else merged + gb * pb
    out = _dot(merged, wout_ref[...])
    r = ALPHA * x + out
    mu = jnp.mean(r, axis=-1, keepdims=True)
    var = jnp.mean(jnp.square(r - mu), axis=-1, keepdims=True)
    g = vec_ref[V_LNG:V_LNG + 1, :]
    b = vec_ref[V_LNB:V_LNB + 1, :]
    return (r - mu) * lax.rsqrt(var + LN_EPS) * g + b


def _s5_glu(y_s5, wglu_ref):
    glu = _dot(jax.nn.gelu(y_s5), wglu_ref[...])
    return glu[:, :BW] * jax.nn.sigmoid(glu[:, BW:])


def _s5_prep_kernel(lam_ref, ldt_ref, b_ref, lamp_ref, bbd_ref):
    lr = lam_ref[0:1, :]
    li = lam_ref[1:2, :]
    step = jnp.exp(ldt_ref[0:1, :])
    e = jnp.exp(lr * step)
    lbr = e * jnp.cos(li * step)
    lbi = e * jnp.sin(li * step)
    den = lr * lr + li * li
    nr = lbr - 1.0
    qr = (nr * lr + lbi * li) / den
    qi = (lbi * lr - nr * li) / den
    bre = b_ref[0:S5_CH, :]
    bim = b_ref[S5_CH:2 * S5_CH, :]
    bb = jnp.concatenate([qr * bre - qi * bim, qr * bim + qi * bre], axis=1)
    tiled = jnp.concatenate([bb] * S5_GROUPS, axis=0)
    rowg = lax.broadcasted_iota(jnp.int32, tiled.shape, 0) // S5_CH
    colg = (lax.broadcasted_iota(jnp.int32, tiled.shape, 1) % S5_W) // S5_STATE
    bbd_ref[...] = jnp.where(rowg == colg, tiled, 0.0).astype(BF16)

    pows = [(lbr, lbi)]
    for _ in range(7):
        pr, pi = pows[-1]
        pows.append((pr * lbr - pi * lbi, pr * lbi + pi * lbr))
    row = lax.broadcasted_iota(jnp.int32, (SUBLANES, S5_W), 0)

    def bcast(v):
        return jnp.broadcast_to(v, (SUBLANES, S5_W))

    p_re = bcast(pows[7][0])
    p_im = bcast(pows[7][1])
    for j in range(6, -1, -1):
        p_re = jnp.where(row == j, bcast(pows[j][0]), p_re)
        p_im = jnp.where(row == j, bcast(pows[j][1]), p_im)
    lamp_ref[R_LAM:R_LAM + SUBLANES, :] = jnp.where(row == 0, bcast(lbr), jnp.where(row == 1, bcast(lbi), 0.0))
    lamp_ref[R_P:R_P + SUBLANES, :] = p_re
    lamp_ref[R_P + SUBLANES:R_P + 2 * SUBLANES, :] = p_im
    for i, d in enumerate((1, 2, 4)):
        base = R_M + 16 * i
        lamp_ref[base:base + SUBLANES, :] = jnp.where(row >= d, bcast(pows[d - 1][0]), 0.0)
        lamp_ref[base + SUBLANES:base + 2 * SUBLANES, :] = jnp.where(row >= d, bcast(pows[d - 1][1]), 0.0)


def _s5_prep(lam2, ldt, bcat):
    return pl.pallas_call(
        _s5_prep_kernel,
        out_shape=(jax.ShapeDtypeStruct((LAMP_ROWS, S5_W), F32),
                   jax.ShapeDtypeStruct((BW, 2 * S5_W), BF16)),
        name="s5_prep",
    )(lam2, ldt, bcat)


def _prompt_kernel(x_ref, wpre_ref, wpost_ref, vec_ref, wpool_ref, lamp_ref, bbd_ref, cbd_ref,
                   wglu_ref, wbr_ref, wout_ref,
                   y_ref, pool_o, conv_o, ssd_o, mc_o, mn_o, mm_o, s5re_o, s5im_o,
                   poolh, convh, ssd_st, mc_st, nrow, mrow, s5cr, s5ci):
    T = CHUNK
    c = pl.program_id(1)
    last = pl.num_programs(1) - 1

    @pl.when(c == 0)
    def _():
        poolh[0:16, :] = jnp.zeros((16, BW), F32)
        convh[0:SUBLANES, :] = jnp.zeros((SUBLANES, CONV_DIM), F32)
        ssd_st[...] = jnp.zeros(ssd_st.shape, F32)
        mc_st[...] = jnp.zeros(mc_st.shape, F32)
        nrow[...] = jnp.zeros(nrow.shape, F32)
        mrow[...] = jnp.zeros(mrow.shape, F32)
        s5cr[...] = jnp.zeros(s5cr.shape, F32)
        s5ci[...] = jnp.zeros(s5ci.shape, F32)

    x = x_ref[0]
    xb = x.astype(BF16)

    def pre(lo, width):
        return jnp.dot(xb, wpre_ref[:, lo:lo + width], preferred_element_type=F32)

    def post(lo, width):
        return jnp.dot(xb, wpost_ref[:, lo:lo + width], preferred_element_type=F32)

    rows_i = lax.broadcasted_iota(jnp.int32, (T, T), 0)
    cols_i = lax.broadcasted_iota(jnp.int32, (T, T), 1)
    causal = rows_i >= cols_i

    pu = pre(O_PU, BW)
    poolh[16:16 + T, :] = pu

    def pool_ld(k):
        return poolh[16 - k:16 - k + T, :]

    w2 = pu + pool_ld(1)
    w4 = w2 + pool_ld(2) + pool_ld(3)
    w8 = w4
    for k in range(4, 8):
        w8 = w8 + pool_ld(k)
    w16 = w8
    for k in range(8, 16):
        w16 = w16 + pool_ld(k)
    pos = c * T + lax.broadcasted_iota(jnp.int32, (T, BW), 0)
    cnt = jnp.minimum(pos + 1, _pool_window_row()).astype(F32)
    dpool = _pool_select(w2, w4, w8, w16) / cnt - pu
    y_pool = _dot(dpool, wpool_ref[...]) * vec_ref[V_MISC:V_MISC + 1, 0:256]
    poolh[0:16, :] = poolh[T:T + 16, :]

    v_sm, sp, lsg, a_row = _small_block(pre(O_SM, LANES), vec_ref)
    lane_s = lax.broadcasted_iota(jnp.int32, (T, LANES), 1)
    g_blk = jnp.where(lane_s < 4, sp * a_row, jnp.where(lane_s < 8, v_sm, jnp.where(lane_s < 12, lsg, 0.0)))
    tril = causal.astype(BF16)
    g_hi = g_blk.astype(BF16)
    g_r1 = g_blk - g_hi.astype(F32)
    g_mid = g_r1.astype(BF16)
    g_lo = (g_r1 - g_mid.astype(F32)).astype(BF16)
    cum = (jnp.dot(tril, g_hi, preferred_element_type=F32)
           + jnp.dot(tril, g_mid, preferred_element_type=F32)
           + jnp.dot(tril, g_lo, preferred_element_type=F32))
    cum_t = cum.T
    g_t = g_blk.T

    convh[SUBLANES:SUBLANES + T, :] = pre(O_XBC, CONV_DIM)
    acc = vec_ref[V_CONVB:V_CONVB + 1, 0:CONV_DIM]
    for k in range(CONV_K):
        acc = acc + convh[5 + k:5 + k + T, :] * vec_ref[V_CONVW + k:V_CONVW + k + 1, 0:CONV_DIM]
    convh[0:SUBLANES, :] = convh[T:T + SUBLANES, :]
    xbc = _silu(acc)
    xs = xbc[:, 0:BW]
    dt_cols = [sp[:, h:h + 1] for h in range(HEADS)]
    acs_cols = [cum[:, h:h + 1] for h in range(HEADS)]
    acs_last = [cum[T - 1:T, h:h + 1] for h in range(HEADS)]
    xdt = xs * _expand_heads(dt_cols)
    xw_t = (xdt * _expand_heads([jnp.exp(acs_last[h] - acs_cols[h]) for h in range(HEADS)])).T
    d_ssd = vec_ref[V_MISC:V_MISC + 1, 256:512]
    y_heads = []
    for g in range(2):
        b_g = xbc[:, BW + g * SSD_STATE:BW + (g + 1) * SSD_STATE]
        c_g = xbc[:, BW + 128 + g * SSD_STATE:BW + 128 + (g + 1) * SSD_STATE]
        cb = _dot_nt(c_g, b_g)
        for h in (2 * g, 2 * g + 1):
            seg = acs_cols[h] - cum_t[h:h + 1, :]
            lmat = jnp.exp(jnp.where(causal, seg, -jnp.inf))
            y_diag = _dot(cb * lmat, xdt[:, h * HD:(h + 1) * HD])
            h_prev = ssd_st[h]
            y_off = jnp.exp(acs_cols[h]) * _dot_nt(c_g, h_prev)
            y_heads.append(y_diag + y_off)
            st = _dot(xw_t[h * HD:(h + 1) * HD, :], b_g)
            ssd_st[h] = jnp.exp(acs_last[h]) * h_prev + st
    y_ssd = jnp.concatenate(y_heads, axis=1) + d_ssd * xs

    q = pre(O_Q, BW)
    k = pre(O_K, BW) * (HD ** -0.5)
    v = pre(O_V, BW)
    h_heads = []
    wk_parts = []
    decays = []
    m_row = mrow[0:1, :]
    n_row = nrow[0:1, :]
    m_news = []
    for h in range(HEADS):
        qh = q[:, h * HD:(h + 1) * HD]
        kh = k[:, h * HD:(h + 1) * HD]
        vh = v[:, h * HD:(h + 1) * HD]
        b_col = cum[:, 8 + h:9 + h]
        ig_col = g_blk[:, 4 + h:5 + h]
        m_prev = m_row[:, h:h + 1]
        dlog = jnp.where(causal, b_col - cum_t[8 + h:9 + h, :] + g_t[4 + h:5 + h, :], -jnp.inf)
        inter = b_col + m_prev
        m_t = jnp.maximum(inter, jnp.max(dlog, axis=1, keepdims=True))
        dw = jnp.exp(dlog - m_t)
        wi = jnp.exp(inter - m_t)
        s = _dot_nt(qh, kh) * dw
        c_prev = mc_st[h]
        n_prev = n_row[:, h * HD:(h + 1) * HD]
        num = _dot(s, vh) + wi * _dot(qh, c_prev)
        den = jnp.sum(s, axis=1, keepdims=True) + wi * jnp.sum(qh * n_prev, axis=1, keepdims=True)
        h_heads.append(num / jnp.maximum(jnp.abs(den), jnp.exp(-m_t)))
        m_new = m_t[T - 1:T, :]
        b_last = b_col[T - 1:T, :]
        wk_parts.append(jnp.exp(b_last - b_col + ig_col - m_new))
        decays.append(jnp.exp(b_last + m_prev - m_new))
        m_news.append(m_new)
    lane_m = lax.broadcasted_iota(jnp.int32, (1, LANES), 1)
    m_row_new = jnp.zeros((1, LANES), F32)
    for h in range(HEADS):
        m_row_new = jnp.where(lane_m == h, jnp.broadcast_to(m_news[h], (1, LANES)), m_row_new)
    mrow[0:1, :] = m_row_new
    wk = k * _expand_heads(wk_parts)
    wk_t = wk.T
    for h in range(HEADS):
        mc_st[h] = decays[h] * mc_st[h] + _dot(wk_t[h * HD:(h + 1) * HD, :], v[:, h * HD:(h + 1) * HD])
    nrow[0:1, :] = _expand_heads(decays) * n_row + jnp.sum(wk, axis=0, keepdims=True)
    y_m = jax.nn.sigmoid(post(O_OG, BW)) * jnp.concatenate(h_heads, axis=1)

    su = pre(O_SU, BW)
    bu = _dot(su, bbd_ref[...])
    h_re = bu[:, 0:S5_W]
    h_im = bu[:, S5_W:2 * S5_W]
    nblk = T // SUBLANES
    for i, d in enumerate((1, 2, 4)):
        base = R_M + 16 * i
        m_re = jnp.concatenate([lamp_ref[base:base + SUBLANES, :]] * nblk, axis=0)
        m_im = jnp.concatenate([lamp_ref[base + SUBLANES:base + 2 * SUBLANES, :]] * nblk, axis=0)
        r_re = pltpu.roll(h_re, d, 0)
        r_im = pltpu.roll(h_im, d, 0)
        h_re, h_im = h_re + (m_re * r_re - m_im * r_im), h_im + (m_re * r_im + m_im * r_re)
    p_re = lamp_ref[R_P:R_P + SUBLANES, :]
    p_im = lamp_ref[R_P + SUBLANES:R_P + 2 * SUBLANES, :]
    cr = s5cr[0:1, :]
    ci = s5ci[0:1, :]
    blk_re = []
    blk_im = []
    for j in range(nblk):
        lo = j * SUBLANES
        br = h_re[lo:lo + SUBLANES, :] + (p_re * cr - p_im * ci)
        bi = h_im[lo:lo + SUBLANES, :] + (p_re * ci + p_im * cr)
        blk_re.append(br)
        blk_im.append(bi)
        cr = br[SUBLANES - 1:SUBLANES, :]
        ci = bi[SUBLANES - 1:SUBLANES, :]
    s5cr[0:1, :] = cr
    s5ci[0:1, :] = ci
    h_all = jnp.concatenate([jnp.concatenate(blk_re, axis=0), jnp.concatenate(blk_im, axis=0)], axis=1)
    y_s5 = _dot(h_all, cbd_ref[...]) + vec_ref[V_MISC:V_MISC + 1, 512:768] * su
    y_s5 = _s5_glu(y_s5, wglu_ref)

    zs = [post(O_Z + b * BW, BW) for b in range(4)]
    y_ref[0] = _merge_tail(x, zs, [y_pool, y_ssd, y_m, y_s5],
                           lambda b: post(O_GATE + b * D_MODEL, D_MODEL), wbr_ref, wout_ref, vec_ref)

    @pl.when(c == last)
    def _():
        pool_o[0] = poolh[1:16, :]
        conv_o[0] = convh[5:8, :]
        ssd_o[0] = ssd_st[...]
        mc_o[0] = mc_st[...]
        mn_o[0] = nrow[0:1, :]
        mm_o[0] = mrow[0:1, :]
        s5re_o[0] = s5cr[0:1, :]
        s5im_o[0] = s5ci[0:1, :]


def _const_spec(shape, layer):
    nd = len(shape)
    return pl.BlockSpec((None,) + tuple(shape), lambda *_: (layer,) + (0,) * nd,
                        pipeline_mode=pl.Buffered(1))


def _prompt_layer(x, layer, p):
    bsz, seq, _ = x.shape
    nc = seq // CHUNK
    T = CHUNK

    def bmap(nd):
        return lambda b, c: (b,) + (0,) * nd

    out_shape = (
        jax.ShapeDtypeStruct((bsz, seq, D_MODEL), F32),
        jax.ShapeDtypeStruct((bsz, POOL_BUF, BW), F32),
        jax.ShapeDtypeStruct((bsz, CONV_K - 1, CONV_DIM), F32),
        jax.ShapeDtypeStruct((bsz, HEADS, HD, SSD_STATE), F32),
        jax.ShapeDtypeStruct((bsz, HEADS, HD, HD), F32),
        jax.ShapeDtypeStruct((bsz, 1, BW), F32),
        jax.ShapeDtypeStruct((bsz, 1, LANES), F32),
        jax.ShapeDtypeStruct((bsz, 1, S5_W), F32),
        jax.ShapeDtypeStruct((bsz, 1, S5_W), F32),
    )
    out_specs = (
        pl.BlockSpec((1, T, D_MODEL), lambda b, c: (b, c, 0)),
        pl.BlockSpec((1, POOL_BUF, BW), bmap(2)),
        pl.BlockSpec((1, CONV_K - 1, CONV_DIM), bmap(2)),
        pl.BlockSpec((1, HEADS, HD, SSD_STATE), bmap(3)),
        pl.BlockSpec((1, HEADS, HD, HD), bmap(3)),
        pl.BlockSpec((1, 1, BW), bmap(2)),
        pl.BlockSpec((1, 1, LANES), bmap(2)),
        pl.BlockSpec((1, 1, S5_W), bmap(2)),
        pl.BlockSpec((1, 1, S5_W), bmap(2)),
    )
    in_specs = [
        pl.BlockSpec((1, T, D_MODEL), lambda b, c: (b, c, 0)),
        _const_spec((D_MODEL, PRE_W), layer),
        _const_spec((D_MODEL, POST_W), layer),
        _const_spec((16, D_MODEL), layer),
        _const_spec((BW, BW), layer),
        _const_spec((LAMP_ROWS, S5_W), layer),
        _const_spec((BW, 2 * S5_W), layer),
        _const_spec((2 * S5_W, BW), layer),
        _const_spec((BW, 2 * BW), layer),
        _const_spec((4, BW, D_MODEL), layer),
        _const_spec((D_MODEL, D_MODEL), layer),
    ]
    scratch = [
        pltpu.VMEM((T + 16, BW), F32),
        pltpu.VMEM((T + SUBLANES, CONV_DIM), F32),
        pltpu.VMEM((HEADS, HD, SSD_STATE), F32),
        pltpu.VMEM((HEADS, HD, HD), F32),
        pltpu.VMEM((SUBLANES, BW), F32),
        pltpu.VMEM((SUBLANES, LANES), F32),
        pltpu.VMEM((SUBLANES, S5_W), F32),
        pltpu.VMEM((SUBLANES, S5_W), F32),
    ]
    return pl.pallas_call(
        _prompt_kernel,
        grid=(bsz, nc),
        in_specs=in_specs,
        out_specs=out_specs,
        out_shape=out_shape,
        scratch_shapes=scratch,
        compiler_params=pltpu.CompilerParams(dimension_semantics=("arbitrary", "arbitrary"),
                                             vmem_limit_bytes=VMEM_LIMIT),
        name="prompt_layer",
    )(x, p["wpre"], p["wpost"], p["vec"], p["wpool"], p["lamp"], p["bbd"], p["cbd"],
      p["wglu"], p["wbr"], p["wout"])


RW_BC, RW_XDT, RW_Q, RW_WK, RW_V, RW_SDEC, RW_MDEC, RW_W = 0, 256, 512, 768, 1024, 1280, 1408, 1536
MID_W = 7 * BW


def _sample_pre_kernel(x_ref, wpre_ref, vec_ref, wpool_ref, lamp_ref, bbd_ref, cbd_ref,
                       pool_ref, conv_ref, n_ref, m_ref, s5re_ref, s5im_ref,
                       rows_o, mid_o, pool_o, conv_o, n_o, m_o, s5re_o, s5im_o):
    x = x_ref[...]
    nb = x.shape[0]
    pp = _dot(x, wpre_ref[...])
    pu = pp[:, O_PU:O_PU + BW]
    xbc_raw = pp[:, O_XBC:O_XBC + CONV_DIM]
    q = pp[:, O_Q:O_Q + BW]
    k = pp[:, O_K:O_K + BW] * (HD ** -0.5)
    v = pp[:, O_V:O_V + BW]
    su = pp[:, O_SU:O_SU + BW]

    hist = pool_ref[...]

    def ext(j):
        return hist[:, j * BW:(j + 1) * BW]

    w2 = pu + ext(14)
    w4 = w2 + ext(13) + ext(12)
    w8 = w4
    for j in range(11, 7, -1):
        w8 = w8 + ext(j)
    w16 = w8
    for j in range(7, -1, -1):
        w16 = w16 + ext(j)
    cnt = jnp.minimum(PAST_LEN + 1, _pool_window_row()).astype(F32)
    dpool = _pool_select(w2, w4, w8, w16) / cnt - pu
    y_pool = _dot(dpool, wpool_ref[...]) * vec_ref[V_MISC:V_MISC + 1, 0:256]
    pool_o[...] = jnp.concatenate([hist[:, BW:], pu], axis=1)

    ch = conv_ref[...]
    acc = vec_ref[V_CONVB:V_CONVB + 1, 0:CONV_DIM]
    for kk in range(CONV_K - 1):
        acc = acc + ch[:, kk * CONV_DIM:(kk + 1) * CONV_DIM] * vec_ref[V_CONVW + kk:V_CONVW + kk + 1, 0:CONV_DIM]
    acc = acc + xbc_raw * vec_ref[V_CONVW + CONV_K - 1:V_CONVW + CONV_K, 0:CONV_DIM]
    conv_o[...] = jnp.concatenate([ch[:, CONV_DIM:], xbc_raw], axis=1)
    xbc = _silu(acc)
    xs = xbc[:, 0:BW]
    bc = xbc[:, BW:2 * BW]

    v_sm, sp, lsg, a_row = _small_block(pp[:, O_SM:O_SM + LANES], vec_ref)
    sdec_blk = jnp.exp(sp * a_row)
    dt_cols = [sp[:, h:h + 1] for h in range(HEADS)]
    xdt = xs * _expand_heads(dt_cols)
    cb_cols = []
    for g in range(2):
        b_g = bc[:, g * SSD_STATE:(g + 1) * SSD_STATE]
        c_g = bc[:, 128 + g * SSD_STATE:128 + (g + 1) * SSD_STATE]
        cb_cols.append(jnp.sum(c_g * b_g, axis=1, keepdims=True))
    yssd_part = (_expand_heads([cb_cols[h // 2] for h in range(HEADS)]) * xdt
                 + vec_ref[V_MISC:V_MISC + 1, 256:512] * xs)
    sdec_exp = _expand_heads([sdec_blk[:, h:h + 1] for h in range(HEADS)])

    n0 = n_ref[...]
    m0 = m_ref[...]
    s_cols, wi_cols, dw_cols, dmax_cols, m_cols = [], [], [], [], []
    for h in range(HEADS):
        sl = slice(h * HD, (h + 1) * HD)
        ig = v_sm[:, 4 + h:5 + h]
        lf = lsg[:, 8 + h:9 + h]
        inter = lf + m0[:, h:h + 1]
        m_t = jnp.maximum(inter, (lf - lf) + ig)
        dw = jnp.exp((lf - lf) + ig - m_t)
        wi = jnp.exp(inter - m_t)
        s = jnp.sum(q[:, sl] * k[:, sl], axis=1, keepdims=True) * dw
        den = s + wi * jnp.sum(q[:, sl] * n0[:, sl], axis=1, keepdims=True)
        s_cols.append(s)
        wi_cols.append(wi)
        dw_cols.append(dw)
        dmax_cols.append(jnp.maximum(jnp.abs(den), jnp.exp(-m_t)))
        m_cols.append(m_t)
    wi_exp = _expand_heads(wi_cols)
    wk = _expand_heads(dw_cols) * k
    n_o[...] = wi_exp * n0 + wk
    lane = lax.broadcasted_iota(jnp.int32, (nb, LANES), 1)
    m_blk = jnp.zeros((nb, LANES), F32)
    wi_blk = jnp.zeros((nb, LANES), F32)
    for h in range(HEADS):
        m_blk = jnp.where(lane == h, jnp.broadcast_to(m_cols[h], (nb, LANES)), m_blk)
        wi_blk = jnp.where(lane == h, jnp.broadcast_to(wi_cols[h], (nb, LANES)), wi_blk)
    m_o[...] = m_blk

    bu = _dot(su, bbd_ref[...])
    lbr = lamp_ref[R_LAM:R_LAM + 1, :]
    lbi = lamp_ref[R_LAM + 1:R_LAM + 2, :]
    h0r = s5re_ref[...]
    h0i = s5im_ref[...]
    h_re = bu[:, 0:S5_W] + (lbr * h0r - lbi * h0i)
    h_im = bu[:, S5_W:2 * S5_W] + (lbr * h0i + lbi * h0r)
    s5re_o[...] = h_re
    s5im_o[...] = h_im
    y_s5 = _dot(jnp.concatenate([h_re, h_im], axis=1), cbd_ref[...]) + vec_ref[V_MISC:V_MISC + 1, 512:768] * su

    rows_o[...] = jnp.concatenate([bc, xdt, q, wk, v, sdec_blk, wi_blk], axis=1)
    mid_o[...] = jnp.concatenate([y_pool, yssd_part, sdec_exp, _expand_heads(s_cols) * v, wi_exp,
                                  _expand_heads(dmax_cols), y_s5], axis=1)


def _sample_pre(x, layer, p, pool2, conv2, n2, m2, s5re2, s5im2):
    nb = x.shape[0]
    full = lambda a: pl.BlockSpec(a.shape, lambda *_: (0,) * a.ndim)
    out_shape = (
        jax.ShapeDtypeStruct((nb, RW_W), F32),
        jax.ShapeDtypeStruct((nb, MID_W), F32),
        jax.ShapeDtypeStruct(pool2.shape, F32),
        jax.ShapeDtypeStruct(conv2.shape, F32),
        jax.ShapeDtypeStruct((nb, BW), F32),
        jax.ShapeDtypeStruct((nb, LANES), F32),
        jax.ShapeDtypeStruct((nb, S5_W), F32),
        jax.ShapeDtypeStruct((nb, S5_W), F32),
    )
    in_specs = [
        full(x),
        _const_spec((D_MODEL, PRE_W), layer),
        _const_spec((16, D_MODEL), layer),
        _const_spec((BW, BW), layer),
        _const_spec((LAMP_ROWS, S5_W), layer),
        _const_spec((BW, 2 * S5_W), layer),
        _const_spec((2 * S5_W, BW), layer),
        full(pool2), full(conv2), full(n2), full(m2), full(s5re2), full(s5im2),
    ]
    return pl.pallas_call(
        _sample_pre_kernel,
        grid=(1,),
        in_specs=in_specs,
        out_specs=tuple(pl.BlockSpec(s.shape, lambda *_: (0, 0)) for s in out_shape),
        out_shape=out_shape,
        compiler_params=pltpu.CompilerParams(dimension_semantics=("arbitrary",),
                                             vmem_limit_bytes=VMEM_LIMIT),
        name="sample_pre",
    )(x, p["wpre"], p["vec"], p["wpool"], p["lamp"], p["bbd"], p["cbd"],
      pool2, conv2, n2, m2, s5re2, s5im2)


SAMPLE_BLOCK = 8


def _sample_state_kernel(rows_ref, ssd_ref, mc_ref, sbo_ref, ssd_o, mc_o):
    ri = lax.broadcasted_iota(jnp.int32, (HD, HD), 0)
    ci = lax.broadcasted_iota(jnp.int32, (HD, HD), 1)
    eye = (ri == ci).astype(F32)

    def to_col(row):
        return jnp.sum(eye * row, axis=1, keepdims=True)

    def to_row(col):
        return jnp.sum(eye * col, axis=0, keepdims=True)

    for j in range(SAMPLE_BLOCK):
        row = rows_ref[j:j + 1, :]
        y_off, q_c = [], []
        for h in range(HEADS):
            g = h // 2
            b_row = row[:, RW_BC + g * SSD_STATE:RW_BC + (g + 1) * SSD_STATE]
            c_row = row[:, RW_BC + 128 + g * SSD_STATE:RW_BC + 128 + (g + 1) * SSD_STATE]
            xdt_row = row[:, RW_XDT + h * HD:RW_XDT + (h + 1) * HD]
            sdec = row[:, RW_SDEC + h:RW_SDEC + h + 1]
            hs = ssd_ref[j, h]
            y_off.append(to_row(jnp.sum(hs * c_row, axis=1, keepdims=True)))
            ssd_o[j, h] = sdec * hs + to_col(xdt_row) * b_row

            q_row = row[:, RW_Q + h * HD:RW_Q + (h + 1) * HD]
            wk_row = row[:, RW_WK + h * HD:RW_WK + (h + 1) * HD]
            v_row = row[:, RW_V + h * HD:RW_V + (h + 1) * HD]
            mdec = row[:, RW_MDEC + h:RW_MDEC + h + 1]
            cs = mc_ref[j, h]
            q_c.append(jnp.sum(to_col(q_row) * cs, axis=0, keepdims=True))
            mc_o[j, h] = mdec * cs + to_col(wk_row) * v_row
        sbo_ref[j:j + 1, :] = jnp.concatenate(y_off + q_c, axis=1)


def _sample_state(rows, ssd, mc):
    nb = rows.shape[0]
    blk = SAMPLE_BLOCK
    st_spec = pl.BlockSpec((blk, HEADS, HD, HD), lambda i: (i, 0, 0, 0))
    return pl.pallas_call(
        _sample_state_kernel,
        grid=(nb // blk,),
        in_specs=[pl.BlockSpec((blk, RW_W), lambda i: (i, 0)), st_spec, st_spec],
        out_specs=(pl.BlockSpec((blk, 2 * BW), lambda i: (i, 0)), st_spec, st_spec),
        out_shape=(jax.ShapeDtypeStruct((nb, 2 * BW), F32),
                   jax.ShapeDtypeStruct(ssd.shape, F32),
                   jax.ShapeDtypeStruct(mc.shape, F32)),
        compiler_params=pltpu.CompilerParams(dimension_semantics=("arbitrary",)),
        name="sample_state",
    )(rows, ssd, mc)


def _sample_post_kernel(x_ref, wpost_ref, vec_ref, mid_ref, sbo_ref, wglu_ref, wbr_ref, wout_ref, y_ref):
    x = x_ref[...]
    xb = x.astype(BF16)

    def post(lo, width):
        return jnp.dot(xb, wpost_ref[:, lo:lo + width], preferred_element_type=F32)

    def mid(i):
        return mid_ref[:, i * BW:(i + 1) * BW]

    y_pool, yssd_part, sdec_exp, sv, wi_exp, dmax_exp, y_s5 = [mid(i) for i in range(7)]
    y_ssd = yssd_part + sbo_ref[:, 0:BW] * sdec_exp
    hh = (sv + wi_exp * sbo_ref[:, BW:2 * BW]) / dmax_exp
    y_m = jax.nn.sigmoid(post(O_OG, BW)) * hh
    y_s5 = _s5_glu(y_s5, wglu_ref)
    zs = [post(O_Z + b * BW, BW) for b in range(4)]
    y_ref[...] = _merge_tail(x, zs, [y_pool, y_ssd, y_m, y_s5],
                             lambda b: post(O_GATE + b * D_MODEL, D_MODEL), wbr_ref, wout_ref, vec_ref)


def _sample_post(x, layer, p, mid, sbo):
    nb = x.shape[0]
    full = lambda a: pl.BlockSpec(a.shape, lambda *_: (0,) * a.ndim)
    return pl.pallas_call(
        _sample_post_kernel,
        grid=(1,),
        in_specs=[full(x), _const_spec((D_MODEL, POST_W), layer), _const_spec((16, D_MODEL), layer),
                  full(mid), full(sbo), _const_spec((BW, 2 * BW), layer),
                  _const_spec((4, BW, D_MODEL), layer), _const_spec((D_MODEL, D_MODEL), layer)],
        out_specs=pl.BlockSpec((nb, D_MODEL), lambda *_: (0, 0)),
        out_shape=jax.ShapeDtypeStruct((nb, D_MODEL), F32),
        compiler_params=pltpu.CompilerParams(dimension_semantics=("arbitrary",),
                                             vmem_limit_bytes=VMEM_LIMIT),
        name="sample_post",
    )(x, p["wpost"], p["vec"], mid, sbo, p["wglu"], p["wbr"], p["wout"])


def _prepare_params(w_in, w_pool, pool_scale, conv_w, conv_b, dt_bias, a_log, d_ssd, ig_bias, fg_bias,
                    lam_re, lam_im, b_re, b_im, c_re, c_im, log_dt, d_s5, w_glu, w_br, w_out, ln_g, ln_b):
    depth = w_in.shape[0]
    sizes = (256, 256, 512, 4, 256, 256, 256, 256, 4, 4, 256, 256, 256, 256, 4096)
    offs = [0]
    for s in sizes:
        offs.append(offs[-1] + s)
    col = lambda i: w_in[:, :, offs[i]:offs[i + 1]]
    small = jnp.concatenate([col(3), col(8), col(9), jnp.zeros((depth, D_MODEL, LANES - 12), w_in.dtype)], axis=2)
    wpre = jnp.concatenate([col(0), col(2), col(5), col(6), col(7), col(12), small], axis=2).astype(BF16)
    wpost = jnp.concatenate([col(1), col(4), col(11), col(13), col(10), col(14)], axis=2).astype(BF16)

    zeros = lambda n: jnp.zeros((depth, n), F32)
    misc = jnp.concatenate([pool_scale, jnp.repeat(d_ssd, HD, axis=1), d_s5,
                            dt_bias, ig_bias, fg_bias, zeros(LANES - 12),
                            a_log, zeros(LANES - HEADS)], axis=1)
    pad512 = lambda a: jnp.concatenate([a, jnp.zeros(a.shape[:-1] + (D_MODEL - CONV_DIM,), F32)], axis=-1)
    vec = jnp.concatenate([ln_g[:, None], ln_b[:, None], misc[:, None], pad512(conv_b)[:, None],
                           pad512(conv_w), jnp.zeros((depth, 16 - 4 - CONV_K, D_MODEL), F32)], axis=1)

    eye4 = jnp.eye(len(POOL_WINDOWS), dtype=F32)
    wpool = jnp.einsum("lgce,gh->lgche", w_pool, eye4).reshape(depth, BW, BW).astype(BF16)

    lam2 = jnp.stack([lam_re.reshape(depth, S5_W), lam_im.reshape(depth, S5_W)], axis=1)
    ldt = jnp.repeat(log_dt, S5_STATE, axis=1)[:, None]
    to_rows = lambda b: jnp.transpose(b, (0, 3, 1, 2)).reshape(depth, S5_CH, S5_W)
    bcat = jnp.concatenate([to_rows(b_re), to_rows(b_im)], axis=1)

    eye_g = jnp.eye(S5_GROUPS, dtype=F32)
    cb = lambda cm: jnp.einsum("lgcp,gh->lgphc", cm, eye_g).reshape(depth, S5_W, BW)
    cbd = jnp.concatenate([cb(c_re), -cb(c_im)], axis=1).astype(BF16)

    lamp, bbd = [], []
    for l in range(depth):
        lp, bb = _s5_prep(lam2[l], ldt[l], bcat[l])
        lamp.append(lp)
        bbd.append(bb)
    return dict(wpre=wpre, wpost=wpost, vec=vec, wpool=wpool, lamp=jnp.stack(lamp), bbd=jnp.stack(bbd),
                cbd=cbd, wglu=w_glu.astype(BF16), wbr=w_br.astype(BF16), wout=w_out.astype(BF16))


def kernel(x_prompt, x_sample, state_pool, state_ssd_conv, state_ssd, state_mlstm_C, state_mlstm_n, state_mlstm_m, state_s5_re, state_s5_im, w_in, w_pool, pool_scale, conv_w, conv_b, dt_bias, a_log, d_ssd, ig_bias, fg_bias, lam_re, lam_im, b_re, b_im, c_re, c_im, log_dt, d_s5, w_glu, w_br, w_out, ln_g, ln_b):
    p = _prepare_params(w_in, w_pool, pool_scale, conv_w, conv_b, dt_bias, a_log, d_ssd, ig_bias, fg_bias,
                        lam_re, lam_im, b_re, b_im, c_re, c_im, log_dt, d_s5, w_glu, w_br, w_out, ln_g, ln_b)
    depth = w_in.shape[0]
    bsz = x_prompt.shape[0]
    nb = x_sample.shape[0]
    yp = x_prompt
    ys = x_sample.reshape(nb, D_MODEL)
    outs_p = [[] for _ in range(8)]
    outs_s = [[] for _ in range(8)]
    for l in range(depth):
        yp, pool_p, conv_p, ssd_p, mc_p, mn_p, mm_p, re_p, im_p = _prompt_layer(yp, l, p)
        new_p = (pool_p, conv_p, ssd_p, mc_p, mn_p.reshape(bsz, HEADS, HD), mm_p[:, 0, :HEADS],
                 re_p.reshape(bsz, S5_GROUPS, S5_STATE), im_p.reshape(bsz, S5_GROUPS, S5_STATE))

        rows, mid, pool_s, conv_s, n_s, m_s, re_s, im_s = _sample_pre(
            ys, l, p,
            state_pool[l].reshape(nb, POOL_BUF * BW), state_ssd_conv[l].reshape(nb, (CONV_K - 1) * CONV_DIM),
            state_mlstm_n[l].reshape(nb, BW), state_mlstm_m[l],
            state_s5_re[l].reshape(nb, S5_W), state_s5_im[l].reshape(nb, S5_W))
        sbo, ssd_s, mc_s = _sample_state(rows, state_ssd[l], state_mlstm_C[l])
        ys = _sample_post(ys, l, p, mid, sbo)
        new_s = (pool_s.reshape(nb, POOL_BUF, BW), conv_s.reshape(nb, CONV_K - 1, CONV_DIM), ssd_s, mc_s,
                 n_s.reshape(nb, HEADS, HD), m_s[:, :HEADS],
                 re_s.reshape(nb, S5_GROUPS, S5_STATE), im_s.reshape(nb, S5_GROUPS, S5_STATE))
        for i in range(8):
            outs_p[i].append(new_p[i])
            outs_s[i].append(new_s[i])
    sp = [jnp.stack(o, axis=0) for o in outs_p]
    ss = [jnp.stack(o, axis=0) for o in outs_s]
    out = [yp, ys.reshape(nb, 1, D_MODEL)]
    for i in range(8):
        out.append(sp[i])
        out.append(ss[i])
    return tuple(out)
```

```python
import functools

import jax
import jax.numpy as jnp
from jax import lax
from jax.experimental import pallas as pl
from jax.experimental.pallas import tpu as pltpu

F32 = jnp.float32
BF16 = jnp.bfloat16

D_MODEL = 1024
DEPTH = 2
PAST_LEN = 16384
BW = 256
POOL_WINDOWS = (2, 4, 8, 16)
POOL_BUF = 15
HEADS = 4
HD = 64
SSD_STATE = 64
CONV_K = 4
CONV_DIM = 512
S5_GROUPS = 16
S5_CH = 16
S5_STATE = 64
S5_W = S5_GROUPS * S5_STATE
CHUNK = 128
ALPHA = (2.0 * DEPTH) ** 0.25
LN_EPS = 1e-5

SUBLANES = 8
LANES = 128
VMEM_LIMIT = 56 * 1024 * 1024

PRE_W = 1920
POST_W = 5376
O_PU, O_XBC, O_Q, O_K, O_V, O_SU, O_SM = 0, 256, 768, 1024, 1280, 1536, 1792
O_Z, O_OG, O_GATE = 0, 1024, 1280

R_LAM = 0
R_P = 8
R_M = 24
LAMP_ROWS = R_M + 3 * 16

V_LNG, V_LNB, V_MISC, V_CONVB, V_CONVW = 0, 1, 2, 3, 4


def _dot(a, b):
    return jnp.dot(a.astype(BF16), b.astype(BF16), preferred_element_type=F32)


def _dot_nt(a, b):
    return lax.dot_general(a.astype(BF16), b.astype(BF16), (((1,), (1,)), ((), ())),
                           preferred_element_type=F32)


def _softplus(x):
    return jnp.logaddexp(x, 0.0)


def _silu(x):
    return x * jax.nn.sigmoid(x)


def _expand_heads(cols, width=HD):
    rows = cols[0].shape[0]
    lane = lax.broadcasted_iota(jnp.int32, (rows, HEADS * width), 1)
    out = jnp.broadcast_to(cols[HEADS - 1], (rows, HEADS * width))
    for h in range(HEADS - 2, -1, -1):
        out = jnp.where(lane < (h + 1) * width, jnp.broadcast_to(cols[h], (rows, HEADS * width)), out)
    return out


def _pool_select(w2, w4, w8, w16):
    lane = lax.broadcasted_iota(jnp.int32, w2.shape, 1)
    return jnp.where(lane < 64, w2, jnp.where(lane < 128, w4, jnp.where(lane < 192, w8, w16)))


def _pool_window_row():
    lane = lax.broadcasted_iota(jnp.int32, (1, BW), 1)
    return jnp.where(lane < 64, 2, jnp.where(lane < 128, 4, jnp.where(lane < 192, 8, 16)))


def _small_block(sm, vec_ref):
    bias = vec_ref[V_MISC:V_MISC + 1, 768:896]
    alog = vec_ref[V_MISC:V_MISC + 1, 896:1024]
    v = sm + bias
    sp = _softplus(v)
    lsg = -_softplus(-v)
    a_row = -jnp.exp(alog)
    return v, sp, lsg, a_row


def _merge_tail(x, zs, ys, gates_raw, wbr_ref, wout_ref, vec_ref):
    merged = None
    for b in range(4):
        br = ys[b] * _silu(zs[b])
        pb = _dot(br, wbr_ref[b])
        gb = jax.nn.sigmoid(gates_raw[b])
        merged = gb * pb if merged is None else merged + gb * pb
    out = _dot(merged, wout_ref[...])
    r = ALPHA * x + out
    mu = jnp.mean(r, axis=-1, keepdims=True)
    var = jnp.mean(jnp.square(r - mu), axis=-1, keepdims=True)
    g = vec_ref[V_LNG:V_LNG + 1, :]
    b = vec_ref[V_LNB:V_LNB + 1, :]
    return (r - mu) * lax.rsqrt(var + LN_EPS) * g + b


def _s5_glu(y_s5, wglu_ref):
    glu = _dot(jax.nn.gelu(y_s5), wglu_ref[...])
    return glu[:, :BW] * jax.nn.sigmoid(glu[:, BW:])


def _s5_prep_kernel(lam_ref, ldt_ref, b_ref, lamp_ref, bbd_ref):
    lr = lam_ref[0:1, :]
    li = lam_ref[1:2, :]
    step = jnp.exp(ldt_ref[0:1, :])
    e = jnp.exp(lr * step)
    lbr = e * jnp.cos(li * step)
    lbi = e * jnp.sin(li * step)
    den = lr * lr + li * li
    nr = lbr - 1.0
    qr = (nr * lr + lbi * li) / den
    qi = (lbi * lr - nr * li) / den
    bre = b_ref[0:S5_CH, :]
    bim = b_ref[S5_CH:2 * S5_CH, :]
    bb = jnp.concatenate([qr * bre - qi * bim, qr * bim + qi * bre], axis=1)
    tiled = jnp.concatenate([bb] * S5_GROUPS, axis=0)
    rowg = lax.broadcasted_iota(jnp.int32, tiled.shape, 0) // S5_CH
    colg = (lax.broadcasted_iota(jnp.int32, tiled.shape, 1) % S5_W) // S5_STATE
    bbd_ref[...] = jnp.where(rowg == colg, tiled, 0.0).astype(BF16)

    pows = [(lbr, lbi)]
    for _ in range(7):
        pr, pi = pows[-1]
        pows.append((pr * lbr - pi * lbi, pr * lbi + pi * lbr))
    row = lax.broadcasted_iota(jnp.int32, (SUBLANES, S5_W), 0)

    def bcast(v):
        return jnp.broadcast_to(v, (SUBLANES, S5_W))

    p_re = bcast(pows[7][0])
    p_im = bcast(pows[7][1])
    for j in range(6, -1, -1):
        p_re = jnp.where(row == j, bcast(pows[j][0]), p_re)
        p_im = jnp.where(row == j, bcast(pows[j][1]), p_im)
    lamp_ref[R_LAM:R_LAM + SUBLANES, :] = jnp.where(row == 0, bcast(lbr), jnp.where(row == 1, bcast(lbi), 0.0))
    lamp_ref[R_P:R_P + SUBLANES, :] = p_re
    lamp_ref[R_P + SUBLANES:R_P + 2 * SUBLANES, :] = p_im
    for i, d in enumerate((1, 2, 4)):
        base = R_M + 16 * i
        lamp_ref[base:base + SUBLANES, :] = jnp.where(row >= d, bcast(pows[d - 1][0]), 0.0)
        lamp_ref[base + SUBLANES:base + 2 * SUBLANES, :] = jnp.where(row >= d, bcast(pows[d - 1][1]), 0.0)


def _s5_prep(lam2, ldt, bcat):
    return pl.pallas_call(
        _s5_prep_kernel,
        out_shape=(jax.ShapeDtypeStruct((LAMP_ROWS, S5_W), F32),
                   jax.ShapeDtypeStruct((BW, 2 * S5_W), BF16)),
        name="s5_prep",
    )(lam2, ldt, bcat)


def _prompt_kernel(x_ref, wpre_ref, wpost_ref, vec_ref, wpool_ref, lamp_ref, bbd_ref, cbd_ref,
                   wglu_ref, wbr_ref, wout_ref,
                   y_ref, pool_o, conv_o, ssd_o, mc_o, mn_o, mm_o, s5re_o, s5im_o,
                   poolh, convh, ssd_st, mc_st, nrow, mrow, s5cr, s5ci):
    T = CHUNK
    c = pl.program_id(1)
    last = pl.num_programs(1) - 1

    @pl.when(c == 0)
    def _():
        poolh[0:16, :] = jnp.zeros((16, BW), F32)
        convh[0:SUBLANES, :] = jnp.zeros((SUBLANES, CONV_DIM), F32)
        ssd_st[...] = jnp.zeros(ssd_st.shape, F32)
        mc_st[...] = jnp.zeros(mc_st.shape, F32)
        nrow[...] = jnp.zeros(nrow.shape, F32)
        mrow[...] = jnp.zeros(mrow.shape, F32)
        s5cr[...] = jnp.zeros(s5cr.shape, F32)
        s5ci[...] = jnp.zeros(s5ci.shape, F32)

    x = x_ref[0]
    xb = x.astype(BF16)

    def pre(lo, width):
        return jnp.dot(xb, wpre_ref[:, lo:lo + width], preferred_element_type=F32)

    def post(lo, width):
        return jnp.dot(xb, wpost_ref[:, lo:lo + width], preferred_element_type=F32)

    rows_i = lax.broadcasted_iota(jnp.int32, (T, T), 0)
    cols_i = lax.broadcasted_iota(jnp.int32, (T, T), 1)
    causal = rows_i >= cols_i

    pu = pre(O_PU, BW)
    poolh[16:16 + T, :] = pu

    def pool_ld(k):
        return poolh[16 - k:16 - k + T, :]

    w2 = pu + pool_ld(1)
    w4 = w2 + pool_ld(2) + pool_ld(3)
    w8 = w4
    for k in range(4, 8):
        w8 = w8 + pool_ld(k)
    w16 = w8
    for k in range(8, 16):
        w16 = w16 + pool_ld(k)
    pos = c * T + lax.broadcasted_iota(jnp.int32, (T, BW), 0)
    cnt = jnp.minimum(pos + 1, _pool_window_row()).astype(F32)
    dpool = _pool_select(w2, w4, w8, w16) / cnt - pu
    y_pool = _dot(dpool, wpool_ref[...]) * vec_ref[V_MISC:V_MISC + 1, 0:256]
    poolh[0:16, :] = poolh[T:T + 16, :]

    v_sm, sp, lsg, a_row = _small_block(pre(O_SM, LANES), vec_ref)
    lane_s = lax.broadcasted_iota(jnp.int32, (T, LANES), 1)
    g_blk = jnp.where(lane_s < 4, sp * a_row, jnp.where(lane_s < 8, v_sm, jnp.where(lane_s < 12, lsg, 0.0)))
    tril = causal.astype(BF16)
    g_hi = g_blk.astype(BF16)
    g_r1 = g_blk - g_hi.astype(F32)
    g_mid = g_r1.astype(BF16)
    g_lo = (g_r1 - g_mid.astype(F32)).astype(BF16)
    cum = (jnp.dot(tril, g_hi, preferred_element_type=F32)
           + jnp.dot(tril, g_mid, preferred_element_type=F32)
           + jnp.dot(tril, g_lo, preferred_element_type=F32))
    cum_t = cum.T
    g_t = g_blk.T

    y_s5, gates_raw = _s5_chunk_scan(pre(O_SU, BW), lambda b: post(O_GATE + b * D_MODEL, D_MODEL),
                                     lamp_ref, bbd_ref, cbd_ref, vec_ref, s5cr, s5ci)
    zs = [post(O_Z + b * BW, BW) for b in range(4)]

    convh[SUBLANES:SUBLANES + T, :] = pre(O_XBC, CONV_DIM)
    acc = vec_ref[V_CONVB:V_CONVB + 1, 0:CONV_DIM]
    for k in range(CONV_K):
        acc = acc + convh[5 + k:5 + k + T, :] * vec_ref[V_CONVW + k:V_CONVW + k + 1, 0:CONV_DIM]
    convh[0:SUBLANES, :] = convh[T:T + SUBLANES, :]
    xbc = _silu(acc)
    xs = xbc[:, 0:BW]
    dt_cols = [sp[:, h:h + 1] for h in range(HEADS)]
    acs_cols = [cum[:, h:h + 1] for h in range(HEADS)]
    acs_last = [cum[T - 1:T, h:h + 1] for h in range(HEADS)]
    xdt = xs * _expand_heads(dt_cols)
    xw_t = (xdt * _expand_heads([jnp.exp(acs_last[h] - acs_cols[h]) for h in range(HEADS)])).T
    d_ssd = vec_ref[V_MISC:V_MISC + 1, 256:512]
    q = pre(O_Q, BW)
    k = pre(O_K, BW) * (HD ** -0.5)
    v = pre(O_V, BW)
    m_row = mrow[0:1, :]
    n_row = nrow[0:1, :]
    hsl = [slice(h * HD, (h + 1) * HD) for h in range(HEADS)]
    b_gs = [xbc[:, BW + g * SSD_STATE:BW + (g + 1) * SSD_STATE] for g in range(2)]
    c_gs = [xbc[:, BW + 128 + g * SSD_STATE:BW + 128 + (g + 1) * SSD_STATE] for g in range(2)]

    cbs = [_dot_nt(c_gs[g], b_gs[g]) for g in range(2)]
    offs = [_dot_nt(c_gs[g], ssd_st[2 * g:2 * g + 2].reshape(2 * HD, SSD_STATE)) for g in range(2)]
    qks = [_dot_nt(q[:, hsl[h]], k[:, hsl[h]]) for h in range(HEADS)]
    qcs = [_dot(q[:, hsl[h]], mc_st[h]) for h in range(HEADS)]
    gms = [cbs[h // 2] * jnp.exp(jnp.where(causal, acs_cols[h] - cum_t[h:h + 1, :], -jnp.inf))
           for h in range(HEADS)]
    b_cols = [cum[:, 8 + h:9 + h] for h in range(HEADS)]
    m_prevs = [m_row[:, h:h + 1] for h in range(HEADS)]
    dlogs = [jnp.where(causal, b_cols[h] - cum_t[8 + h:9 + h, :] + g_t[4 + h:5 + h, :], -jnp.inf)
             for h in range(HEADS)]
    inters = [b_cols[h] + m_prevs[h] for h in range(HEADS)]
    m_ts = [jnp.maximum(inters[h], jnp.max(dlogs[h], axis=1, keepdims=True)) for h in range(HEADS)]
    ss = [qks[h] * jnp.exp(dlogs[h] - m_ts[h]) for h in range(HEADS)]
    wis = [jnp.exp(inters[h] - m_ts[h]) for h in range(HEADS)]
    y_diags = [_dot(gms[h], xdt[:, hsl[h]]) for h in range(HEADS)]
    svs = [_dot(ss[h], v[:, hsl[h]]) for h in range(HEADS)]
    y_off = jnp.concatenate(offs, axis=1) * _expand_heads([jnp.exp(acs_cols[h]) for h in range(HEADS)])
    y_ssd = jnp.concatenate(y_diags, axis=1) + y_off + d_ssd * xs
    h_heads, wk_parts, decays, m_news = [], [], [], []
    for h in range(HEADS):
        num = svs[h] + wis[h] * qcs[h]
        den = (jnp.sum(ss[h], axis=1, keepdims=True)
               + wis[h] * jnp.sum(q[:, hsl[h]] * n_row[:, hsl[h]], axis=1, keepdims=True))
        h_heads.append(num / jnp.maximum(jnp.abs(den), jnp.exp(-m_ts[h])))
        m_new = m_ts[h][T - 1:T, :]
        b_last = b_cols[h][T - 1:T, :]
        wk_parts.append(jnp.exp(b_last - b_cols[h] + g_blk[:, 4 + h:5 + h] - m_new))
        decays.append(jnp.exp(b_last + m_prevs[h] - m_new))
        m_news.append(m_new)
    for h in range(HEADS):
        ssd_st[h] = jnp.exp(acs_last[h]) * ssd_st[h] + _dot(xw_t[hsl[h], :], b_gs[h // 2])
    lane_m = lax.broadcasted_iota(jnp.int32, (1, LANES), 1)
    m_row_new = jnp.zeros((1, LANES), F32)
    for h in range(HEADS):
        m_row_new = jnp.where(lane_m == h, jnp.broadcast_to(m_news[h], (1, LANES)), m_row_new)
    mrow[0:1, :] = m_row_new
    wk = k * _expand_heads(wk_parts)
    wk_t = wk.T
    for h in range(HEADS):
        mc_st[h] = decays[h] * mc_st[h] + _dot(wk_t[h * HD:(h + 1) * HD, :], v[:, h * HD:(h + 1) * HD])
    nrow[0:1, :] = _expand_heads(decays) * n_row + jnp.sum(wk, axis=0, keepdims=True)
    y_m = jax.nn.sigmoid(post(O_OG, BW)) * jnp.concatenate(h_heads, axis=1)
    y_s5 = _s5_glu(y_s5, wglu_ref)

    y_ref[0] = _merge_tail(x, zs, [y_pool, y_ssd, y_m, y_s5], gates_raw, wbr_ref, wout_ref, vec_ref)

    @pl.when(c == last)
    def _():
        pool_o[0] = poolh[1:16, :]
        conv_o[0] = convh[5:8, :]
        ssd_o[0] = ssd_st[...]
        mc_o[0] = mc_st[...]
        mn_o[0] = nrow[0:1, :]
        mm_o[0] = mrow[0:1, :]
        s5re_o[0] = s5cr[0:1, :]
        s5im_o[0] = s5ci[0:1, :]


def _s5_chunk_scan(su, gate_mm, lamp_ref, bbd_ref, cbd_ref, vec_ref, s5cr, s5ci):
    T = su.shape[0]
    gates_raw = []
    bu = _dot(su, bbd_ref[...])
    h_re = bu[:, 0:S5_W]
    h_im = bu[:, S5_W:2 * S5_W]
    nblk = T // SUBLANES
    for i, d in enumerate((1, 2, 4)):
        base = R_M + 16 * i
        m_re = jnp.concatenate([lamp_ref[base:base + SUBLANES, :]] * nblk, axis=0)
        m_im = jnp.concatenate([lamp_ref[base + SUBLANES:base + 2 * SUBLANES, :]] * nblk, axis=0)
        r_re = pltpu.roll(h_re, d, 0)
        r_im = pltpu.roll(h_im, d, 0)
        h_re, h_im = h_re + (m_re * r_re - m_im * r_im), h_im + (m_re * r_im + m_im * r_re)
        gates_raw.append(gate_mm(i))
    p_re = lamp_ref[R_P:R_P + SUBLANES, :]
    p_im = lamp_ref[R_P + SUBLANES:R_P + 2 * SUBLANES, :]
    cr = s5cr[0:1, :]
    ci = s5ci[0:1, :]
    blk_re = []
    blk_im = []
    for j in range(nblk):
        lo = j * SUBLANES
        br = h_re[lo:lo + SUBLANES, :] + (p_re * cr - p_im * ci)
        bi = h_im[lo:lo + SUBLANES, :] + (p_re * ci + p_im * cr)
        blk_re.append(br)
        blk_im.append(bi)
        cr = br[SUBLANES - 1:SUBLANES, :]
        ci = bi[SUBLANES - 1:SUBLANES, :]
    s5cr[0:1, :] = cr
    s5ci[0:1, :] = ci
    gates_raw.append(gate_mm(3))
    h_all = jnp.concatenate([jnp.concatenate(blk_re, axis=0), jnp.concatenate(blk_im, axis=0)], axis=1)
    y_s5 = _dot(h_all, cbd_ref[...]) + vec_ref[V_MISC:V_MISC + 1, 512:768] * su
    return y_s5, gates_raw


def _const_spec(shape, layer):
    nd = len(shape)
    return pl.BlockSpec((None,) + tuple(shape), lambda *_: (layer,) + (0,) * nd,
                        pipeline_mode=pl.Buffered(1))


def _prompt_layer(x, layer, p):
    bsz, seq, _ = x.shape
    nc = seq // CHUNK
    T = CHUNK

    def bmap(nd):
        return lambda b, c: (b,) + (0,) * nd

    out_shape = (
        jax.ShapeDtypeStruct((bsz, seq, D_MODEL), F32),
        jax.ShapeDtypeStruct((bsz, POOL_BUF, BW), F32),
        jax.ShapeDtypeStruct((bsz, CONV_K - 1, CONV_DIM), F32),
        jax.ShapeDtypeStruct((bsz, HEADS, HD, SSD_STATE), F32),
        jax.ShapeDtypeStruct((bsz, HEADS, HD, HD), F32),
        jax.ShapeDtypeStruct((bsz, 1, BW), F32),
        jax.ShapeDtypeStruct((bsz, 1, LANES), F32),
        jax.ShapeDtypeStruct((bsz, 1, S5_W), F32),
        jax.ShapeDtypeStruct((bsz, 1, S5_W), F32),
    )
    out_specs = (
        pl.BlockSpec((1, T, D_MODEL), lambda b, c: (b, c, 0)),
        pl.BlockSpec((1, POOL_BUF, BW), bmap(2)),
        pl.BlockSpec((1, CONV_K - 1, CONV_DIM), bmap(2)),
        pl.BlockSpec((1, HEADS, HD, SSD_STATE), bmap(3)),
        pl.BlockSpec((1, HEADS, HD, HD), bmap(3)),
        pl.BlockSpec((1, 1, BW), bmap(2)),
        pl.BlockSpec((1, 1, LANES), bmap(2)),
        pl.BlockSpec((1, 1, S5_W), bmap(2)),
        pl.BlockSpec((1, 1, S5_W), bmap(2)),
    )
    in_specs = [
        pl.BlockSpec((1, T, D_MODEL), lambda b, c: (b, c, 0)),
        _const_spec((D_MODEL, PRE_W), layer),
        _const_spec((D_MODEL, POST_W), layer),
        _const_spec((16, D_MODEL), layer),
        _const_spec((BW, BW), layer),
        _const_spec((LAMP_ROWS, S5_W), layer),
        _const_spec((BW, 2 * S5_W), layer),
        _const_spec((2 * S5_W, BW), layer),
        _const_spec((BW, 2 * BW), layer),
        _const_spec((4, BW, D_MODEL), layer),
        _const_spec((D_MODEL, D_MODEL), layer),
    ]
    scratch = [
        pltpu.VMEM((T + 16, BW), F32),
        pltpu.VMEM((T + SUBLANES, CONV_DIM), F32),
        pltpu.VMEM((HEADS, HD, SSD_STATE), F32),
        pltpu.VMEM((HEADS, HD, HD), F32),
        pltpu.VMEM((SUBLANES, BW), F32),
        pltpu.VMEM((SUBLANES, LANES), F32),
        pltpu.VMEM((SUBLANES, S5_W), F32),
        pltpu.VMEM((SUBLANES, S5_W), F32),
    ]
    return pl.pallas_call(
        _prompt_kernel,
        grid=(bsz, nc),
        in_specs=in_specs,
        out_specs=out_specs,
        out_shape=out_shape,
        scratch_shapes=scratch,
        compiler_params=pltpu.CompilerParams(dimension_semantics=("arbitrary", "arbitrary"),
                                             vmem_limit_bytes=VMEM_LIMIT),
        name="prompt_layer",
    )(x, p["wpre"], p["wpost"], p["vec"], p["wpool"], p["lamp"], p["bbd"], p["cbd"],
      p["wglu"], p["wbr"], p["wout"])


RW_BC, RW_XDT, RW_Q, RW_WK, RW_V, RW_SDEC, RW_MDEC, RW_W = 0, 256, 512, 768, 1024, 1280, 1408, 1536
MID_W = 7 * BW


def _sample_pre_kernel(x_ref, wpre_ref, vec_ref, wpool_ref, lamp_ref, bbd_ref, cbd_ref,
                       pool_ref, conv_ref, n_ref, m_ref, s5re_ref, s5im_ref,
                       rows_o, mid_o, pool_o, conv_o, n_o, m_o, s5re_o, s5im_o):
    x = x_ref[...]
    nb = x.shape[0]
    pp = _dot(x, wpre_ref[...])
    pu = pp[:, O_PU:O_PU + BW]
    xbc_raw = pp[:, O_XBC:O_XBC + CONV_DIM]
    q = pp[:, O_Q:O_Q + BW]
    k = pp[:, O_K:O_K + BW] * (HD ** -0.5)
    v = pp[:, O_V:O_V + BW]
    su = pp[:, O_SU:O_SU + BW]

    hist = pool_ref[...]

    def ext(j):
        return hist[:, j * BW:(j + 1) * BW]

    w2 = pu + ext(14)
    w4 = w2 + ext(13) + ext(12)
    w8 = w4
    for j in range(11, 7, -1):
        w8 = w8 + ext(j)
    w16 = w8
    for j in range(7, -1, -1):
        w16 = w16 + ext(j)
    cnt = jnp.minimum(PAST_LEN + 1, _pool_window_row()).astype(F32)
    dpool = _pool_select(w2, w4, w8, w16) / cnt - pu
    y_pool = _dot(dpool, wpool_ref[...]) * vec_ref[V_MISC:V_MISC + 1, 0:256]
    pool_o[...] = jnp.concatenate([hist[:, BW:], pu], axis=1)

    ch = conv_ref[...]
    acc = vec_ref[V_CONVB:V_CONVB + 1, 0:CONV_DIM]
    for kk in range(CONV_K - 1):
        acc = acc + ch[:, kk * CONV_DIM:(kk + 1) * CONV_DIM] * vec_ref[V_CONVW + kk:V_CONVW + kk + 1, 0:CONV_DIM]
    acc = acc + xbc_raw * vec_ref[V_CONVW + CONV_K - 1:V_CONVW + CONV_K, 0:CONV_DIM]
    conv_o[...] = jnp.concatenate([ch[:, CONV_DIM:], xbc_raw], axis=1)
    xbc = _silu(acc)
    xs = xbc[:, 0:BW]
    bc = xbc[:, BW:2 * BW]

    v_sm, sp, lsg, a_row = _small_block(pp[:, O_SM:O_SM + LANES], vec_ref)
    sdec_blk = jnp.exp(sp * a_row)
    dt_cols = [sp[:, h:h + 1] for h in range(HEADS)]
    xdt = xs * _expand_heads(dt_cols)
    cb_cols = []
    for g in range(2):
        b_g = bc[:, g * SSD_STATE:(g + 1) * SSD_STATE]
        c_g = bc[:, 128 + g * SSD_STATE:128 + (g + 1) * SSD_STATE]
        cb_cols.append(jnp.sum(c_g * b_g, axis=1, keepdims=True))
    yssd_part = (_expand_heads([cb_cols[h // 2] for h in range(HEADS)]) * xdt
                 + vec_ref[V_MISC:V_MISC + 1, 256:512] * xs)
    sdec_exp = _expand_heads([sdec_blk[:, h:h + 1] for h in range(HEADS)])

    n0 = n_ref[...]
    m0 = m_ref[...]
    s_cols, wi_cols, dw_cols, dmax_cols, m_cols = [], [], [], [], []
    for h in range(HEADS):
        sl = slice(h * HD, (h + 1) * HD)
        ig = v_sm[:, 4 + h:5 + h]
        lf = lsg[:, 8 + h:9 + h]
        inter = lf + m0[:, h:h + 1]
        m_t = jnp.maximum(inter, (lf - lf) + ig)
        dw = jnp.exp((lf - lf) + ig - m_t)
        wi = jnp.exp(inter - m_t)
        s = jnp.sum(q[:, sl] * k[:, sl], axis=1, keepdims=True) * dw
        den = s + wi * jnp.sum(q[:, sl] * n0[:, sl], axis=1, keepdims=True)
        s_cols.append(s)
        wi_cols.append(wi)
        dw_cols.append(dw)
        dmax_cols.append(jnp.maximum(jnp.abs(den), jnp.exp(-m_t)))
        m_cols.append(m_t)
    wi_exp = _expand_heads(wi_cols)
    wk = _expand_heads(dw_cols) * k
    n_o[...] = wi_exp * n0 + wk
    lane = lax.broadcasted_iota(jnp.int32, (nb, LANES), 1)
    m_blk = jnp.zeros((nb, LANES), F32)
    wi_blk = jnp.zeros((nb, LANES), F32)
    for h in range(HEADS):
        m_blk = jnp.where(lane == h, jnp.broadcast_to(m_cols[h], (nb, LANES)), m_blk)
        wi_blk = jnp.where(lane == h, jnp.broadcast_to(wi_cols[h], (nb, LANES)), wi_blk)
    m_o[...] = m_blk

    bu = _dot(su, bbd_ref[...])
    lbr = lamp_ref[R_LAM:R_LAM + 1, :]
    lbi = lamp_ref[R_LAM + 1:R_LAM + 2, :]
    h0r = s5re_ref[...]
    h0i = s5im_ref[...]
    h_re = bu[:, 0:S5_W] + (lbr * h0r - lbi * h0i)
    h_im = bu[:, S5_W:2 * S5_W] + (lbr * h0i + lbi * h0r)
    s5re_o[...] = h_re
    s5im_o[...] = h_im
    y_s5 = _dot(jnp.concatenate([h_re, h_im], axis=1), cbd_ref[...]) + vec_ref[V_MISC:V_MISC + 1, 512:768] * su

    rows_o[...] = jnp.concatenate([bc, xdt, q, wk, v, sdec_blk, wi_blk], axis=1)
    mid_o[...] = jnp.concatenate([y_pool, yssd_part, sdec_exp, _expand_heads(s_cols) * v, wi_exp,
                                  _expand_heads(dmax_cols), y_s5], axis=1)


def _sample_pre(x, layer, p, pool2, conv2, n2, m2, s5re2, s5im2):
    nb = x.shape[0]
    full = lambda a: pl.BlockSpec(a.shape, lambda *_: (0,) * a.ndim)
    out_shape = (
        jax.ShapeDtypeStruct((nb, RW_W), F32),
        jax.ShapeDtypeStruct((nb, MID_W), F32),
        jax.ShapeDtypeStruct(pool2.shape, F32),
        jax.ShapeDtypeStruct(conv2.shape, F32),
        jax.ShapeDtypeStruct((nb, BW), F32),
        jax.ShapeDtypeStruct((nb, LANES), F32),
        jax.ShapeDtypeStruct((nb, S5_W), F32),
        jax.ShapeDtypeStruct((nb, S5_W), F32),
    )
    in_specs = [
        full(x),
        _const_spec((D_MODEL, PRE_W), layer),
        _const_spec((16, D_MODEL), layer),
        _const_spec((BW, BW), layer),
        _const_spec((LAMP_ROWS, S5_W), layer),
        _const_spec((BW, 2 * S5_W), layer),
        _const_spec((2 * S5_W, BW), layer),
        full(pool2), full(conv2), full(n2), full(m2), full(s5re2), full(s5im2),
    ]
    return pl.pallas_call(
        _sample_pre_kernel,
        grid=(1,),
        in_specs=in_specs,
        out_specs=tuple(pl.BlockSpec(s.shape, lambda *_: (0, 0)) for s in out_shape),
        out_shape=out_shape,
        compiler_params=pltpu.CompilerParams(dimension_semantics=("arbitrary",),
                                             vmem_limit_bytes=VMEM_LIMIT),
        name="sample_pre",
    )(x, p["wpre"], p["vec"], p["wpool"], p["lamp"], p["bbd"], p["cbd"],
      pool2, conv2, n2, m2, s5re2, s5im2)


SAMPLE_BLOCK = 8


def _sample_state_kernel(first_layer, rows_ref, ssd_ref, mc_ref, *rest):
    if first_layer:
        sbo_ref, ssd_o, mc_o = rest
        ssd_o[1] = ssd_ref[1]
        mc_o[1] = mc_ref[1]
        ssd_in, mc_in, ssd_out, mc_out = ssd_ref.at[0], mc_ref.at[0], ssd_o.at[0], mc_o.at[0]
    else:
        _, _, sbo_ref, ssd_out, mc_out = rest
        ssd_in, mc_in = ssd_ref, mc_ref
    rows = rows_ref[...]
    xdt_t = rows[:, RW_XDT:RW_XDT + BW].T
    wk_t = rows[:, RW_WK:RW_WK + BW].T
    c_b = rows[:, RW_BC + 128:RW_BC + 256].astype(BF16)
    q_b = rows[:, RW_Q:RW_Q + BW].astype(BF16)
    for j in range(SAMPLE_BLOCK):
        y_off, q_c = [], []
        for g in range(2):
            hs2 = ssd_in[j, 2 * g:2 * g + 2].reshape(2 * HD, SSD_STATE)
            r = lax.dot_general(c_b[:, g * SSD_STATE:(g + 1) * SSD_STATE], hs2.astype(BF16),
                                (((1,), (1,)), ((), ())), preferred_element_type=F32)
            y_off.append(r[j:j + 1, :])
        for h in range(HEADS):
            g = h // 2
            b_row = rows[j:j + 1, RW_BC + g * SSD_STATE:RW_BC + (g + 1) * SSD_STATE]
            sdec = rows[j:j + 1, RW_SDEC + h:RW_SDEC + h + 1]
            ssd_out[j, h] = sdec * ssd_in[j, h] + xdt_t[h * HD:(h + 1) * HD, j:j + 1] * b_row

            cs = mc_in[j, h]
            r = jnp.dot(q_b[:, h * HD:(h + 1) * HD], cs.astype(BF16), preferred_element_type=F32)
            q_c.append(r[j:j + 1, :])
            v_row = rows[j:j + 1, RW_V + h * HD:RW_V + (h + 1) * HD]
            mdec = rows[j:j + 1, RW_MDEC + h:RW_MDEC + h + 1]
            mc_out[j, h] = mdec * cs + wk_t[h * HD:(h + 1) * HD, j:j + 1] * v_row
        sbo_ref[j:j + 1, :] = jnp.concatenate(y_off + q_c, axis=1)


def _sample_state(rows, layer, ssd_all, mc_all, prev=None):
    nb = rows.shape[0]
    blk = SAMPLE_BLOCK
    depth = ssd_all.shape[0]
    tile = (HEADS, HD, HD)
    rows_spec = pl.BlockSpec((blk, RW_W), lambda i: (i, 0))
    sbo_spec = pl.BlockSpec((blk, 2 * BW), lambda i: (i, 0))
    out_shape = (jax.ShapeDtypeStruct((nb, 2 * BW), F32),
                 jax.ShapeDtypeStruct(ssd_all.shape, F32),
                 jax.ShapeDtypeStruct(mc_all.shape, F32))
    params = pltpu.CompilerParams(dimension_semantics=("arbitrary",))
    if prev is None:
        assert layer == 0 and depth == 2
        both = pl.BlockSpec((depth, blk) + tile, lambda i: (0, i, 0, 0, 0))
        return pl.pallas_call(
            functools.partial(_sample_state_kernel, True),
            grid=(nb // blk,),
            in_specs=[rows_spec, both, both],
            out_specs=(sbo_spec, both, both),
            out_shape=out_shape,
            compiler_params=params,
            name="sample_state_first",
        )(rows, ssd_all, mc_all)
    one = pl.BlockSpec((None, blk) + tile, lambda i: (layer, i, 0, 0, 0))
    anywhere = pl.BlockSpec(memory_space=pl.ANY)
    return pl.pallas_call(
        functools.partial(_sample_state_kernel, False),
        grid=(nb // blk,),
        in_specs=[rows_spec, one, one, anywhere, anywhere],
        out_specs=(sbo_spec, one, one),
        out_shape=out_shape,
        input_output_aliases={3: 1, 4: 2},
        compiler_params=params,
        name="sample_state_next",
    )(rows, ssd_all, mc_all, prev[0], prev[1])


def _sample_post_kernel(x_ref, wpost_ref, vec_ref, mid_ref, sbo_ref, wglu_ref, wbr_ref, wout_ref, y_ref):
    x = x_ref[...]
    xb = x.astype(BF16)

    def post(lo, width):
        return jnp.dot(xb, wpost_ref[:, lo:lo + width], preferred_element_type=F32)

    def mid(i):
        return mid_ref[:, i * BW:(i + 1) * BW]

    y_pool, yssd_part, sdec_exp, sv, wi_exp, dmax_exp, y_s5 = [mid(i) for i in range(7)]
    y_ssd = yssd_part + sbo_ref[:, 0:BW] * sdec_exp
    hh = (sv + wi_exp * sbo_ref[:, BW:2 * BW]) / dmax_exp
    y_m = jax.nn.sigmoid(post(O_OG, BW)) * hh
    y_s5 = _s5_glu(y_s5, wglu_ref)
    zs = [post(O_Z + b * BW, BW) for b in range(4)]
    gates_raw = [post(O_GATE + b * D_MODEL, D_MODEL) for b in range(4)]
    y_ref[...] = _merge_tail(x, zs, [y_pool, y_ssd, y_m, y_s5], gates_raw, wbr_ref, wout_ref, vec_ref)


def _sample_post(x, layer, p, mid, sbo):
    nb = x.shape[0]
    full = lambda a: pl.BlockSpec(a.shape, lambda *_: (0,) * a.ndim)
    return pl.pallas_call(
        _sample_post_kernel,
        grid=(1,),
        in_specs=[full(x), _const_spec((D_MODEL, POST_W), layer), _const_spec((16, D_MODEL), layer),
                  full(mid), full(sbo), _const_spec((BW, 2 * BW), layer),
                  _const_spec((4, BW, D_MODEL), layer), _const_spec((D_MODEL, D_MODEL), layer)],
        out_specs=pl.BlockSpec((nb, D_MODEL), lambda *_: (0, 0)),
        out_shape=jax.ShapeDtypeStruct((nb, D_MODEL), F32),
        compiler_params=pltpu.CompilerParams(dimension_semantics=("arbitrary",),
                                             vmem_limit_bytes=VMEM_LIMIT),
        name="sample_post",
    )(x, p["wpost"], p["vec"], mid, sbo, p["wglu"], p["wbr"], p["wout"])


def _prepare_params(w_in, w_pool, pool_scale, conv_w, conv_b, dt_bias, a_log, d_ssd, ig_bias, fg_bias,
                    lam_re, lam_im, b_re, b_im, c_re, c_im, log_dt, d_s5, w_glu, w_br, w_out, ln_g, ln_b):
    depth = w_in.shape[0]
    sizes = (256, 256, 512, 4, 256, 256, 256, 256, 4, 4, 256, 256, 256, 256, 4096)
    offs = [0]
    for s in sizes:
        offs.append(offs[-1] + s)
    col = lambda i: w_in[:, :, offs[i]:offs[i + 1]]
    small = jnp.concatenate([col(3), col(8), col(9), jnp.zeros((depth, D_MODEL, LANES - 12), w_in.dtype)], axis=2)
    wpre = jnp.concatenate([col(0), col(2), col(5), col(6), col(7), col(12), small], axis=2).astype(BF16)
    wpost = jnp.concatenate([col(1), col(4), col(11), col(13), col(10), col(14)], axis=2).astype(BF16)

    zeros = lambda n: jnp.zeros((depth, n), F32)
    misc = jnp.concatenate([pool_scale, jnp.repeat(d_ssd, HD, axis=1), d_s5,
                            dt_bias, ig_bias, fg_bias, zeros(LANES - 12),
                            a_log, zeros(LANES - HEADS)], axis=1)
    pad512 = lambda a: jnp.concatenate([a, jnp.zeros(a.shape[:-1] + (D_MODEL - CONV_DIM,), F32)], axis=-1)
    vec = jnp.concatenate([ln_g[:, None], ln_b[:, None], misc[:, None], pad512(conv_b)[:, None],
                           pad512(conv_w), jnp.zeros((depth, 16 - 4 - CONV_K, D_MODEL), F32)], axis=1)

    eye4 = jnp.eye(len(POOL_WINDOWS), dtype=F32)
    wpool = jnp.einsum("lgce,gh->lgche", w_pool, eye4).reshape(depth, BW, BW).astype(BF16)

    lam2 = jnp.stack([lam_re.reshape(depth, S5_W), lam_im.reshape(depth, S5_W)], axis=1)
    ldt = jnp.repeat(log_dt, S5_STATE, axis=1)[:, None]
    to_rows = lambda b: jnp.transpose(b, (0, 3, 1, 2)).reshape(depth, S5_CH, S5_W)
    bcat = jnp.concatenate([to_rows(b_re), to_rows(b_im)], axis=1)

    eye_g = jnp.eye(S5_GROUPS, dtype=F32)
    cb = lambda cm: jnp.einsum("lgcp,gh->lgphc", cm, eye_g).reshape(depth, S5_W, BW)
    cbd = jnp.concatenate([cb(c_re), -cb(c_im)], axis=1).astype(BF16)

    lamp, bbd = [], []
    for l in range(depth):
        lp, bb = _s5_prep(lam2[l], ldt[l], bcat[l])
        lamp.append(lp)
        bbd.append(bb)
    return dict(wpre=wpre, wpost=wpost, vec=vec, wpool=wpool, lamp=jnp.stack(lamp), bbd=jnp.stack(bbd),
                cbd=cbd, wglu=w_glu.astype(BF16), wbr=w_br.astype(BF16), wout=w_out.astype(BF16))


def kernel(x_prompt, x_sample, state_pool, state_ssd_conv, state_ssd, state_mlstm_C, state_mlstm_n, state_mlstm_m, state_s5_re, state_s5_im, w_in, w_pool, pool_scale, conv_w, conv_b, dt_bias, a_log, d_ssd, ig_bias, fg_bias, lam_re, lam_im, b_re, b_im, c_re, c_im, log_dt, d_s5, w_glu, w_br, w_out, ln_g, ln_b):
    p = _prepare_params(w_in, w_pool, pool_scale, conv_w, conv_b, dt_bias, a_log, d_ssd, ig_bias, fg_bias,
                        lam_re, lam_im, b_re, b_im, c_re, c_im, log_dt, d_s5, w_glu, w_br, w_out, ln_g, ln_b)
    depth = w_in.shape[0]
    bsz = x_prompt.shape[0]
    nb = x_sample.shape[0]
    yp = x_prompt
    ys = x_sample.reshape(nb, D_MODEL)
    outs_p = [[] for _ in range(8)]
    outs_s = [[] for _ in range(8)]
    mats_s = None
    for l in range(depth):
        yp, pool_p, conv_p, ssd_p, mc_p, mn_p, mm_p, re_p, im_p = _prompt_layer(yp, l, p)
        new_p = (pool_p, conv_p, ssd_p, mc_p, mn_p.reshape(bsz, HEADS, HD), mm_p[:, 0, :HEADS],
                 re_p.reshape(bsz, S5_GROUPS, S5_STATE), im_p.reshape(bsz, S5_GROUPS, S5_STATE))

        rows, mid, pool_s, conv_s, n_s, m_s, re_s, im_s = _sample_pre(
            ys, l, p,
            state_pool[l].reshape(nb, POOL_BUF * BW), state_ssd_conv[l].reshape(nb, (CONV_K - 1) * CONV_DIM),
            state_mlstm_n[l].reshape(nb, BW), state_mlstm_m[l],
            state_s5_re[l].reshape(nb, S5_W), state_s5_im[l].reshape(nb, S5_W))
        sbo, ssd_s, mc_s = _sample_state(rows, l, state_ssd, state_mlstm_C, mats_s)
        mats_s = (ssd_s, mc_s)
        ys = _sample_post(ys, l, p, mid, sbo)
        new_s = (pool_s.reshape(nb, POOL_BUF, BW), conv_s.reshape(nb, CONV_K - 1, CONV_DIM), None, None,
                 n_s.reshape(nb, HEADS, HD), m_s[:, :HEADS],
                 re_s.reshape(nb, S5_GROUPS, S5_STATE), im_s.reshape(nb, S5_GROUPS, S5_STATE))
        for i in range(8):
            outs_p[i].append(new_p[i])
            outs_s[i].append(new_s[i])
    sp = [jnp.stack(o, axis=0) for o in outs_p]
    ss = [mats_s[i - 2] if i in (2, 3) else jnp.stack(outs_s[i], axis=0) for i in range(8)]
    out = [yp, ys.reshape(nb, 1, D_MODEL)]
    for i in range(8):
        out.append(sp[i])
        out.append(ss[i])
    return tuple(out)
```

```python
import functools

import jax
import jax.numpy as jnp
from jax import lax
from jax.experimental import pallas as pl
from jax.experimental.pallas import tpu as pltpu

F32 = jnp.float32
BF16 = jnp.bfloat16

D_MODEL = 1024
DEPTH = 2
PAST_LEN = 16384
BW = 256
POOL_WINDOWS = (2, 4, 8, 16)
POOL_BUF = 15
POOL_PAD = 32
HEADS = 4
HD = 64
SSD_STATE = 64
CONV_K = 4
CONV_DIM = 512
S5_GROUPS = 16
S5_CH = 16
S5_STATE = 64
S5_W = S5_GROUPS * S5_STATE
CHUNK = 128
ALPHA = (2.0 * DEPTH) ** 0.25
LN_EPS = 1e-5

SUBLANES = 8
LANES = 128
VMEM_LIMIT = 56 * 1024 * 1024

PRE_W = 1920
POST_W = 5376
O_PU, O_XBC, O_Q, O_K, O_V, O_SU, O_SM = 0, 256, 768, 1024, 1280, 1536, 1792
O_Z, O_OG, O_GATE = 0, 1024, 1280

R_LAM = 0
R_P = 8
R_M = 24
LAMP_ROWS = R_M + 3 * 16

V_LNG, V_LNB, V_MISC, V_CONVB, V_CONVW = 0, 1, 2, 3, 4


def _dot(a, b):
    return jnp.dot(a.astype(BF16), b.astype(BF16), preferred_element_type=F32)


def _dot_nt(a, b):
    return lax.dot_general(a.astype(BF16), b.astype(BF16), (((1,), (1,)), ((), ())),
                           preferred_element_type=F32)


def _softplus(x):
    return jnp.logaddexp(x, 0.0)


def _sigmoid(x):
    return 0.5 * jnp.tanh(0.5 * x) + 0.5


def _silu(x):
    return x * _sigmoid(x)


def _expand_heads(cols, width=HD):
    rows = cols[0].shape[0]
    lane = lax.broadcasted_iota(jnp.int32, (rows, HEADS * width), 1)
    out = jnp.broadcast_to(cols[HEADS - 1], (rows, HEADS * width))
    for h in range(HEADS - 2, -1, -1):
        out = jnp.where(lane < (h + 1) * width, jnp.broadcast_to(cols[h], (rows, HEADS * width)), out)
    return out


def _pool_select(w2, w4, w8, w16):
    lane = lax.broadcasted_iota(jnp.int32, w2.shape, 1)
    return jnp.where(lane < 64, w2, jnp.where(lane < 128, w4, jnp.where(lane < 192, w8, w16)))


def _pool_window_row():
    lane = lax.broadcasted_iota(jnp.int32, (1, BW), 1)
    return jnp.where(lane < 64, 2, jnp.where(lane < 128, 4, jnp.where(lane < 192, 8, 16)))


def _small_block(sm, vec_ref):
    bias = vec_ref[V_MISC:V_MISC + 1, 768:896]
    alog = vec_ref[V_MISC:V_MISC + 1, 896:1024]
    v = sm + bias
    sp = _softplus(v)
    lsg = -_softplus(-v)
    a_row = -jnp.exp(alog)
    return v, sp, lsg, a_row


def _merge_tail(x, z_acts, ys, gates, wbr_ref, wout_ref, vec_ref):
    merged = None
    for b in range(4):
        br = ys[b] * z_acts[b][...]
        pb = _dot(br, wbr_ref[b])
        gb = gates[b][...]
        merged = gb * pb if merged is None else merged + gb * pb
    out = _dot(merged, wout_ref[...])
    r = ALPHA * x + out
    mu = jnp.mean(r, axis=-1, keepdims=True)
    var = jnp.mean(jnp.square(r - mu), axis=-1, keepdims=True)
    g = vec_ref[V_LNG:V_LNG + 1, :]
    b = vec_ref[V_LNB:V_LNB + 1, :]
    return (r - mu) * lax.rsqrt(var + LN_EPS) * g + b


def _s5_glu(y_s5, wglu_ref):
    glu = _dot(jax.nn.gelu(y_s5), wglu_ref[...])
    return glu[:, :BW] * _sigmoid(glu[:, BW:])


def _s5_prep_kernel(lam_ref, ldt_ref, b_ref, lamp_ref, bbd_ref):
    lr = lam_ref[0:1, :]
    li = lam_ref[1:2, :]
    step = jnp.exp(ldt_ref[0:1, :])
    e = jnp.exp(lr * step)
    lbr = e * jnp.cos(li * step)
    lbi = e * jnp.sin(li * step)
    den = lr * lr + li * li
    nr = lbr - 1.0
    qr = (nr * lr + lbi * li) / den
    qi = (lbi * lr - nr * li) / den
    bre = b_ref[0:S5_CH, :]
    bim = b_ref[S5_CH:2 * S5_CH, :]
    bb = jnp.concatenate([qr * bre - qi * bim, qr * bim + qi * bre], axis=1)
    tiled = jnp.concatenate([bb] * S5_GROUPS, axis=0)
    rowg = lax.broadcasted_iota(jnp.int32, tiled.shape, 0) // S5_CH
    colg = (lax.broadcasted_iota(jnp.int32, tiled.shape, 1) % S5_W) // S5_STATE
    bbd_ref[...] = jnp.where(rowg == colg, tiled, 0.0).astype(BF16)

    pows = [(lbr, lbi)]
    for _ in range(7):
        pr, pi = pows[-1]
        pows.append((pr * lbr - pi * lbi, pr * lbi + pi * lbr))
    row = lax.broadcasted_iota(jnp.int32, (SUBLANES, S5_W), 0)

    def bcast(v):
        return jnp.broadcast_to(v, (SUBLANES, S5_W))

    p_re = bcast(pows[7][0])
    p_im = bcast(pows[7][1])
    for j in range(6, -1, -1):
        p_re = jnp.where(row == j, bcast(pows[j][0]), p_re)
        p_im = jnp.where(row == j, bcast(pows[j][1]), p_im)
    lamp_ref[R_LAM:R_LAM + SUBLANES, :] = jnp.where(row == 0, bcast(lbr), jnp.where(row == 1, bcast(lbi), 0.0))
    lamp_ref[R_P:R_P + SUBLANES, :] = p_re
    lamp_ref[R_P + SUBLANES:R_P + 2 * SUBLANES, :] = p_im
    for i, d in enumerate((1, 2, 4)):
        base = R_M + 16 * i
        lamp_ref[base:base + SUBLANES, :] = jnp.where(row >= d, bcast(pows[d - 1][0]), 0.0)
        lamp_ref[base + SUBLANES:base + 2 * SUBLANES, :] = jnp.where(row >= d, bcast(pows[d - 1][1]), 0.0)


def _s5_prep(lam2, ldt, bcat):
    return pl.pallas_call(
        _s5_prep_kernel,
        out_shape=(jax.ShapeDtypeStruct((LAMP_ROWS, S5_W), F32),
                   jax.ShapeDtypeStruct((BW, 2 * S5_W), BF16)),
        name="s5_prep",
    )(lam2, ldt, bcat)


def _prompt_kernel(x_ref, wpre_ref, wpost_ref, vec_ref, wpool_ref, lamp_ref, bbd_ref, cbd_ref,
                   wglu_ref, wbr_ref, wout_ref,
                   y_ref, pool_o, conv_o, ssd_o, mc_o, mn_o, mm_o, s5re_o, s5im_o,
                   poolh, poolw, convh, ssd_st, mc_st, nrow, mrow, s5cr, s5ci, gsc):
    T = CHUNK
    c = pl.program_id(1)
    last = pl.num_programs(1) - 1

    @pl.when(c == 0)
    def _():
        poolh[0:POOL_PAD, :] = jnp.zeros((POOL_PAD, BW), F32)
        convh[0:SUBLANES, :] = jnp.zeros((SUBLANES, CONV_DIM), F32)
        ssd_st[...] = jnp.zeros(ssd_st.shape, F32)
        mc_st[...] = jnp.zeros(mc_st.shape, F32)
        nrow[...] = jnp.zeros(nrow.shape, F32)
        mrow[...] = jnp.zeros(mrow.shape, F32)
        s5cr[...] = jnp.zeros(s5cr.shape, F32)
        s5ci[...] = jnp.zeros(s5ci.shape, F32)

    x = x_ref[0]
    xb = x.astype(BF16)

    def pre(lo, width):
        return jnp.dot(xb, wpre_ref[:, lo:lo + width], preferred_element_type=F32)

    def post(lo, width):
        return jnp.dot(xb, wpost_ref[:, lo:lo + width], preferred_element_type=F32)

    rows_i = lax.broadcasted_iota(jnp.int32, (T, T), 0)
    cols_i = lax.broadcasted_iota(jnp.int32, (T, T), 1)
    causal = rows_i >= cols_i

    n_rows = POOL_PAD + T

    def pool_mix(pu):
        poolh[POOL_PAD:n_rows, :] = pu
        poolw[0, 8:n_rows, :] = poolh[8:n_rows, :] + poolh[7:n_rows - 1, :]
        poolw[1, 16:n_rows, :] = poolw[0, 16:n_rows, :] + poolw[0, 14:n_rows - 2, :]
        poolw[2, 24:n_rows, :] = poolw[1, 24:n_rows, :] + poolw[1, 20:n_rows - 4, :]
        w2 = poolw[0, POOL_PAD:n_rows, :]
        w4 = poolw[1, POOL_PAD:n_rows, :]
        w8 = poolw[2, POOL_PAD:n_rows, :]
        w16 = w8 + poolw[2, POOL_PAD - 8:n_rows - 8, :]
        pos = c * T + lax.broadcasted_iota(jnp.int32, (T, BW), 0)
        cnt = jnp.minimum(pos + 1, _pool_window_row()).astype(F32)
        dpool = _pool_select(w2, w4, w8, w16) / cnt - pu
        poolh[0:POOL_PAD, :] = poolh[T:n_rows, :]
        return _dot(dpool, wpool_ref[...]) * vec_ref[V_MISC:V_MISC + 1, 0:256]

    def conv_mix(xbc_raw):
        convh[SUBLANES:SUBLANES + T, :] = xbc_raw
        acc = vec_ref[V_CONVB:V_CONVB + 1, 0:CONV_DIM]
        for kk in range(CONV_K):
            acc = acc + convh[5 + kk:5 + kk + T, :] * vec_ref[V_CONVW + kk:V_CONVW + kk + 1, 0:CONV_DIM]
        convh[0:SUBLANES, :] = convh[T:T + SUBLANES, :]
        return _silu(acc)

    v_sm, sp, lsg, a_row = _small_block(pre(O_SM, LANES), vec_ref)
    lane_s = lax.broadcasted_iota(jnp.int32, (T, LANES), 1)
    g_blk = jnp.where(lane_s < 4, sp * a_row, jnp.where(lane_s < 8, v_sm, jnp.where(lane_s < 12, lsg, 0.0)))
    tril = causal.astype(BF16)
    g_hi = g_blk.astype(BF16)
    g_r1 = g_blk - g_hi.astype(F32)
    g_mid = g_r1.astype(BF16)
    g_lo = (g_r1 - g_mid.astype(F32)).astype(BF16)
    cum = (jnp.dot(tril, g_hi, preferred_element_type=F32)
           + jnp.dot(tril, g_mid, preferred_element_type=F32)
           + jnp.dot(tril, g_lo, preferred_element_type=F32))
    cum_t = cum.T
    g_t = g_blk.T

    proj = {}
    fillers = [
        lambda: proj.update(pu=pre(O_PU, BW), xbc=pre(O_XBC, CONV_DIM)),
        lambda: proj.update(q=pre(O_Q, BW), k=pre(O_K, BW) * (HD ** -0.5)),
        lambda: proj.update(v=pre(O_V, BW), zs=[post(O_Z + b * BW, BW) for b in range(4)]),
        lambda: proj.update(og=post(O_OG, BW)),
    ]
    y_s5 = _s5_chunk_scan(pre(O_SU, BW), fillers, lamp_ref, bbd_ref, cbd_ref, vec_ref, s5cr, s5ci)
    zs = proj["zs"]

    dyn0 = jnp.minimum(c, 0)
    gate_raw = post(O_GATE, D_MODEL)
    y_pool = pool_mix(proj["pu"])
    gsc[dyn0] = _sigmoid(gate_raw)
    gate_raw = post(O_GATE + D_MODEL, D_MODEL)
    xbc = conv_mix(proj["xbc"])
    gsc[dyn0 + 1] = _sigmoid(gate_raw)
    gate_raw = post(O_GATE + 2 * D_MODEL, D_MODEL)
    gsc[dyn0 + 2] = _sigmoid(gate_raw)
    gate_raw = post(O_GATE + 3 * D_MODEL, D_MODEL)
    gsc[dyn0 + 3] = _sigmoid(gate_raw)
    gates = [gsc.at[dyn0 + b] for b in range(4)]

    xs = xbc[:, 0:BW]
    dt_cols = [sp[:, h:h + 1] for h in range(HEADS)]
    acs_cols = [cum[:, h:h + 1] for h in range(HEADS)]
    acs_last = [cum[T - 1:T, h:h + 1] for h in range(HEADS)]
    xdt = xs * _expand_heads(dt_cols)
    xw_t = (xdt * _expand_heads([jnp.exp(acs_last[h] - acs_cols[h]) for h in range(HEADS)])).T
    d_ssd = vec_ref[V_MISC:V_MISC + 1, 256:512]
    q, k, v = proj["q"], proj["k"], proj["v"]
    m_row = mrow[0:1, :]
    n_row = nrow[0:1, :]
    hsl = [slice(h * HD, (h + 1) * HD) for h in range(HEADS)]
    b_gs = [xbc[:, BW + g * SSD_STATE:BW + (g + 1) * SSD_STATE] for g in range(2)]
    c_gs = [xbc[:, BW + 128 + g * SSD_STATE:BW + 128 + (g + 1) * SSD_STATE] for g in range(2)]

    cbs = [_dot_nt(c_gs[g], b_gs[g]) for g in range(2)]
    offs = [_dot_nt(c_gs[g], ssd_st[2 * g:2 * g + 2].reshape(2 * HD, SSD_STATE)) for g in range(2)]
    qks = [_dot_nt(q[:, hsl[h]], k[:, hsl[h]]) for h in range(HEADS)]
    qcs = [_dot(q[:, hsl[h]], mc_st[h]) for h in range(HEADS)]
    gms = [cbs[h // 2] * jnp.exp(jnp.where(causal, acs_cols[h] - cum_t[h:h + 1, :], -jnp.inf))
           for h in range(HEADS)]
    b_cols = [cum[:, 8 + h:9 + h] for h in range(HEADS)]
    m_prevs = [m_row[:, h:h + 1] for h in range(HEADS)]
    dlogs = [jnp.where(causal, b_cols[h] - cum_t[8 + h:9 + h, :] + g_t[4 + h:5 + h, :], -jnp.inf)
             for h in range(HEADS)]
    inters = [b_cols[h] + m_prevs[h] for h in range(HEADS)]
    m_ts = [jnp.maximum(inters[h], jnp.max(dlogs[h], axis=1, keepdims=True)) for h in range(HEADS)]
    ss = [qks[h] * jnp.exp(dlogs[h] - m_ts[h]) for h in range(HEADS)]
    wis = [jnp.exp(inters[h] - m_ts[h]) for h in range(HEADS)]
    y_diags = [_dot(gms[h], xdt[:, hsl[h]]) for h in range(HEADS)]
    svs = [_dot(ss[h], v[:, hsl[h]]) for h in range(HEADS)]
    y_off = jnp.concatenate(offs, axis=1) * _expand_heads([jnp.exp(acs_cols[h]) for h in range(HEADS)])
    y_ssd = jnp.concatenate(y_diags, axis=1) + y_off + d_ssd * xs
    h_heads, wk_parts, decays, m_news = [], [], [], []
    for h in range(HEADS):
        num = svs[h] + wis[h] * qcs[h]
        den = (jnp.sum(ss[h], axis=1, keepdims=True)
               + wis[h] * jnp.sum(q[:, hsl[h]] * n_row[:, hsl[h]], axis=1, keepdims=True))
        h_heads.append(num / jnp.maximum(jnp.abs(den), jnp.exp(-m_ts[h])))
        m_new = m_ts[h][T - 1:T, :]
        b_last = b_cols[h][T - 1:T, :]
        wk_parts.append(jnp.exp(b_last - b_cols[h] + g_blk[:, 4 + h:5 + h] - m_new))
        decays.append(jnp.exp(b_last + m_prevs[h] - m_new))
        m_news.append(m_new)
    for h in range(HEADS):
        ssd_st[h] = jnp.exp(acs_last[h]) * ssd_st[h] + _dot(xw_t[hsl[h], :], b_gs[h // 2])
    lane_m = lax.broadcasted_iota(jnp.int32, (1, LANES), 1)
    m_row_new = jnp.zeros((1, LANES), F32)
    for h in range(HEADS):
        m_row_new = jnp.where(lane_m == h, jnp.broadcast_to(m_news[h], (1, LANES)), m_row_new)
    mrow[0:1, :] = m_row_new
    wk = k * _expand_heads(wk_parts)
    wk_t = wk.T
    for h in range(HEADS):
        mc_st[h] = decays[h] * mc_st[h] + _dot(wk_t[h * HD:(h + 1) * HD, :], v[:, h * HD:(h + 1) * HD])
    nrow[0:1, :] = _expand_heads(decays) * n_row + jnp.sum(wk, axis=0, keepdims=True)
    y_m = _sigmoid(proj["og"]) * jnp.concatenate(h_heads, axis=1)
    y_s5 = _s5_glu(y_s5, wglu_ref)

    z_acts = [_silu(z) for z in zs]
    y_ref[0] = _merge_tail(x, z_acts, [y_pool, y_ssd, y_m, y_s5], gates, wbr_ref, wout_ref, vec_ref)

    @pl.when(c == last)
    def _():
        pool_o[0] = poolh[POOL_PAD - POOL_BUF:POOL_PAD, :]
        conv_o[0] = convh[5:8, :]
        ssd_o[0] = ssd_st[...]
        mc_o[0] = mc_st[...]
        mn_o[0] = nrow[0:1, :]
        mm_o[0] = mrow[0:1, :]
        s5re_o[0] = s5cr[0:1, :]
        s5im_o[0] = s5ci[0:1, :]


def _s5_chunk_scan(su, fillers, lamp_ref, bbd_ref, cbd_ref, vec_ref, s5cr, s5ci):
    T = su.shape[0]
    bu = _dot(su, bbd_ref[...])
    nblk = T // SUBLANES
    h_re = bu[:, 0:S5_W].reshape(nblk, SUBLANES, S5_W)
    h_im = bu[:, S5_W:2 * S5_W].reshape(nblk, SUBLANES, S5_W)
    for i, d in enumerate((1, 2, 4)):
        base = R_M + 16 * i
        m_re = lamp_ref[base:base + SUBLANES, :]
        m_im = lamp_ref[base + SUBLANES:base + 2 * SUBLANES, :]
        r_re = pltpu.roll(h_re, d, 1)
        r_im = pltpu.roll(h_im, d, 1)
        h_re, h_im = h_re + (m_re * r_re - m_im * r_im), h_im + (m_re * r_im + m_im * r_re)
        fillers[i]()
    p_re = lamp_ref[R_P:R_P + SUBLANES, :]
    p_im = lamp_ref[R_P + SUBLANES:R_P + 2 * SUBLANES, :]
    cr = s5cr[0:1, :]
    ci = s5ci[0:1, :]
    blk_re = []
    blk_im = []
    for j in range(nblk):
        br = h_re[j] + (p_re * cr - p_im * ci)
        bi = h_im[j] + (p_re * ci + p_im * cr)
        blk_re.append(br)
        blk_im.append(bi)
        cr = br[SUBLANES - 1:SUBLANES, :]
        ci = bi[SUBLANES - 1:SUBLANES, :]
    s5cr[0:1, :] = cr
    s5ci[0:1, :] = ci
    fillers[3]()
    h_all = jnp.concatenate([jnp.concatenate(blk_re, axis=0), jnp.concatenate(blk_im, axis=0)], axis=1)
    return _dot(h_all, cbd_ref[...]) + vec_ref[V_MISC:V_MISC + 1, 512:768] * su


def _const_spec(shape, layer):
    nd = len(shape)
    return pl.BlockSpec((None,) + tuple(shape), lambda *_: (layer,) + (0,) * nd,
                        pipeline_mode=pl.Buffered(1))


def _prompt_layer(x, layer, p):
    bsz, seq, _ = x.shape
    nc = seq // CHUNK
    T = CHUNK

    def bmap(nd):
        return lambda b, c: (b,) + (0,) * nd

    out_shape = (
        jax.ShapeDtypeStruct((bsz, seq, D_MODEL), F32),
        jax.ShapeDtypeStruct((bsz, POOL_BUF, BW), F32),
        jax.ShapeDtypeStruct((bsz, CONV_K - 1, CONV_DIM), F32),
        jax.ShapeDtypeStruct((bsz, HEADS, HD, SSD_STATE), F32),
        jax.ShapeDtypeStruct((bsz, HEADS, HD, HD), F32),
        jax.ShapeDtypeStruct((bsz, 1, BW), F32),
        jax.ShapeDtypeStruct((bsz, 1, LANES), F32),
        jax.ShapeDtypeStruct((bsz, 1, S5_W), F32),
        jax.ShapeDtypeStruct((bsz, 1, S5_W), F32),
    )
    out_specs = (
        pl.BlockSpec((1, T, D_MODEL), lambda b, c: (b, c, 0)),
        pl.BlockSpec((1, POOL_BUF, BW), bmap(2)),
        pl.BlockSpec((1, CONV_K - 1, CONV_DIM), bmap(2)),
        pl.BlockSpec((1, HEADS, HD, SSD_STATE), bmap(3)),
        pl.BlockSpec((1, HEADS, HD, HD), bmap(3)),
        pl.BlockSpec((1, 1, BW), bmap(2)),
        pl.BlockSpec((1, 1, LANES), bmap(2)),
        pl.BlockSpec((1, 1, S5_W), bmap(2)),
        pl.BlockSpec((1, 1, S5_W), bmap(2)),
    )
    in_specs = [
        pl.BlockSpec((1, T, D_MODEL), lambda b, c: (b, c, 0)),
        _const_spec((D_MODEL, PRE_W), layer),
        _const_spec((D_MODEL, POST_W), layer),
        _const_spec((16, D_MODEL), layer),
        _const_spec((BW, BW), layer),
        _const_spec((LAMP_ROWS, S5_W), layer),
        _const_spec((BW, 2 * S5_W), layer),
        _const_spec((2 * S5_W, BW), layer),
        _const_spec((BW, 2 * BW), layer),
        _const_spec((4, BW, D_MODEL), layer),
        _const_spec((D_MODEL, D_MODEL), layer),
    ]
    scratch = [
        pltpu.VMEM((T + POOL_PAD, BW), F32),
        pltpu.VMEM((3, T + POOL_PAD, BW), F32),
        pltpu.VMEM((T + SUBLANES, CONV_DIM), F32),
        pltpu.VMEM((HEADS, HD, SSD_STATE), F32),
        pltpu.VMEM((HEADS, HD, HD), F32),
        pltpu.VMEM((SUBLANES, BW), F32),
        pltpu.VMEM((SUBLANES, LANES), F32),
        pltpu.VMEM((SUBLANES, S5_W), F32),
        pltpu.VMEM((SUBLANES, S5_W), F32),
        pltpu.VMEM((4, T, D_MODEL), F32),
    ]
    return pl.pallas_call(
        _prompt_kernel,
        grid=(bsz, nc),
        in_specs=in_specs,
        out_specs=out_specs,
        out_shape=out_shape,
        scratch_shapes=scratch,
        compiler_params=pltpu.CompilerParams(dimension_semantics=("arbitrary", "arbitrary"),
                                             vmem_limit_bytes=VMEM_LIMIT),
        name="prompt_layer",
    )(x, p["wpre"], p["wpost"], p["vec"], p["wpool"], p["lamp"], p["bbd"], p["cbd"],
      p["wglu"], p["wbr"], p["wout"])


RW_BC, RW_XDT, RW_Q, RW_WK, RW_V, RW_SDEC, RW_MDEC, RW_W = 0, 256, 512, 768, 1024, 1280, 1408, 1536
MID_W = 7 * BW


def _sample_pre_kernel(x_ref, wpre_ref, vec_ref, wpool_ref, lamp_ref, bbd_ref, cbd_ref,
                       pool_ref, conv_ref, n_ref, m_ref, s5re_ref, s5im_ref,
                       rows_o, mid_o, pool_o, conv_o, n_o, m_o, s5re_o, s5im_o):
    x = x_ref[...]
    nb = x.shape[0]
    pp = _dot(x, wpre_ref[...])
    pu = pp[:, O_PU:O_PU + BW]
    xbc_raw = pp[:, O_XBC:O_XBC + CONV_DIM]
    q = pp[:, O_Q:O_Q + BW]
    k = pp[:, O_K:O_K + BW] * (HD ** -0.5)
    v = pp[:, O_V:O_V + BW]
    su = pp[:, O_SU:O_SU + BW]

    hist = pool_ref[...]

    def ext(j):
        return hist[:, j * BW:(j + 1) * BW]

    w2 = pu + ext(14)
    w4 = w2 + ext(13) + ext(12)
    w8 = w4
    for j in range(11, 7, -1):
        w8 = w8 + ext(j)
    w16 = w8
    for j in range(7, -1, -1):
        w16 = w16 + ext(j)
    cnt = jnp.minimum(PAST_LEN + 1, _pool_window_row()).astype(F32)
    dpool = _pool_select(w2, w4, w8, w16) / cnt - pu
    y_pool = _dot(dpool, wpool_ref[...]) * vec_ref[V_MISC:V_MISC + 1, 0:256]
    pool_o[...] = jnp.concatenate([hist[:, BW:], pu], axis=1)

    ch = conv_ref[...]
    acc = vec_ref[V_CONVB:V_CONVB + 1, 0:CONV_DIM]
    for kk in range(CONV_K - 1):
        acc = acc + ch[:, kk * CONV_DIM:(kk + 1) * CONV_DIM] * vec_ref[V_CONVW + kk:V_CONVW + kk + 1, 0:CONV_DIM]
    acc = acc + xbc_raw * vec_ref[V_CONVW + CONV_K - 1:V_CONVW + CONV_K, 0:CONV_DIM]
    conv_o[...] = jnp.concatenate([ch[:, CONV_DIM:], xbc_raw], axis=1)
    xbc = _silu(acc)
    xs = xbc[:, 0:BW]
    bc = xbc[:, BW:2 * BW]

    v_sm, sp, lsg, a_row = _small_block(pp[:, O_SM:O_SM + LANES], vec_ref)
    sdec_blk = jnp.exp(sp * a_row)
    dt_cols = [sp[:, h:h + 1] for h in range(HEADS)]
    xdt = xs * _expand_heads(dt_cols)
    cb_cols = []
    for g in range(2):
        b_g = bc[:, g * SSD_STATE:(g + 1) * SSD_STATE]
        c_g = bc[:, 128 + g * SSD_STATE:128 + (g + 1) * SSD_STATE]
        cb_cols.append(jnp.sum(c_g * b_g, axis=1, keepdims=True))
    yssd_part = (_expand_heads([cb_cols[h // 2] for h in range(HEADS)]) * xdt
                 + vec_ref[V_MISC:V_MISC + 1, 256:512] * xs)
    sdec_exp = _expand_heads([sdec_blk[:, h:h + 1] for h in range(HEADS)])

    n0 = n_ref[...]
    m0 = m_ref[...]
    s_cols, wi_cols, dw_cols, dmax_cols, m_cols = [], [], [], [], []
    for h in range(HEADS):
        sl = slice(h * HD, (h + 1) * HD)
        ig = v_sm[:, 4 + h:5 + h]
        lf = lsg[:, 8 + h:9 + h]
        inter = lf + m0[:, h:h + 1]
        m_t = jnp.maximum(inter, (lf - lf) + ig)
        dw = jnp.exp((lf - lf) + ig - m_t)
        wi = jnp.exp(inter - m_t)
        s = jnp.sum(q[:, sl] * k[:, sl], axis=1, keepdims=True) * dw
        den = s + wi * jnp.sum(q[:, sl] * n0[:, sl], axis=1, keepdims=True)
        s_cols.append(s)
        wi_cols.append(wi)
        dw_cols.append(dw)
        dmax_cols.append(jnp.maximum(jnp.abs(den), jnp.exp(-m_t)))
        m_cols.append(m_t)
    wi_exp = _expand_heads(wi_cols)
    wk = _expand_heads(dw_cols) * k
    n_o[...] = wi_exp * n0 + wk
    lane = lax.broadcasted_iota(jnp.int32, (nb, LANES), 1)
    m_blk = jnp.zeros((nb, LANES), F32)
    wi_blk = jnp.zeros((nb, LANES), F32)
    for h in range(HEADS):
        m_blk = jnp.where(lane == h, jnp.broadcast_to(m_cols[h], (nb, LANES)), m_blk)
        wi_blk = jnp.where(lane == h, jnp.broadcast_to(wi_cols[h], (nb, LANES)), wi_blk)
    m_o[...] = m_blk

    bu = _dot(su, bbd_ref[...])
    lbr = lamp_ref[R_LAM:R_LAM + 1, :]
    lbi = lamp_ref[R_LAM + 1:R_LAM + 2, :]
    h0r = s5re_ref[...]
    h0i = s5im_ref[...]
    h_re = bu[:, 0:S5_W] + (lbr * h0r - lbi * h0i)
    h_im = bu[:, S5_W:2 * S5_W] + (lbr * h0i + lbi * h0r)
    s5re_o[...] = h_re
    s5im_o[...] = h_im
    y_s5 = _dot(jnp.concatenate([h_re, h_im], axis=1), cbd_ref[...]) + vec_ref[V_MISC:V_MISC + 1, 512:768] * su

    rows_o[...] = jnp.concatenate([bc, xdt, q, wk, v, sdec_blk, wi_blk], axis=1)
    mid_o[...] = jnp.concatenate([y_pool, yssd_part, sdec_exp, _expand_heads(s_cols) * v, wi_exp,
                                  _expand_heads(dmax_cols), y_s5], axis=1)


def _sample_pre(x, layer, p, pool2, conv2, n2, m2, s5re2, s5im2):
    nb = x.shape[0]
    full = lambda a: pl.BlockSpec(a.shape, lambda *_: (0,) * a.ndim)
    out_shape = (
        jax.ShapeDtypeStruct((nb, RW_W), F32),
        jax.ShapeDtypeStruct((nb, MID_W), F32),
        jax.ShapeDtypeStruct(pool2.shape, F32),
        jax.ShapeDtypeStruct(conv2.shape, F32),
        jax.ShapeDtypeStruct((nb, BW), F32),
        jax.ShapeDtypeStruct((nb, LANES), F32),
        jax.ShapeDtypeStruct((nb, S5_W), F32),
        jax.ShapeDtypeStruct((nb, S5_W), F32),
    )
    in_specs = [
        full(x),
        _const_spec((D_MODEL, PRE_W), layer),
        _const_spec((16, D_MODEL), layer),
        _const_spec((BW, BW), layer),
        _const_spec((LAMP_ROWS, S5_W), layer),
        _const_spec((BW, 2 * S5_W), layer),
        _const_spec((2 * S5_W, BW), layer),
        full(pool2), full(conv2), full(n2), full(m2), full(s5re2), full(s5im2),
    ]
    return pl.pallas_call(
        _sample_pre_kernel,
        grid=(1,),
        in_specs=in_specs,
        out_specs=tuple(pl.BlockSpec(s.shape, lambda *_: (0, 0)) for s in out_shape),
        out_shape=out_shape,
        compiler_params=pltpu.CompilerParams(dimension_semantics=("arbitrary",),
                                             vmem_limit_bytes=VMEM_LIMIT),
        name="sample_pre",
    )(x, p["wpre"], p["vec"], p["wpool"], p["lamp"], p["bbd"], p["cbd"],
      pool2, conv2, n2, m2, s5re2, s5im2)


SAMPLE_BLOCK = 8


def _sample_state_kernel(first_layer, rows_ref, ssd_ref, mc_ref, *rest):
    if first_layer:
        sbo_ref, ssd_o, mc_o = rest
        ssd_o[1] = ssd_ref[1]
        mc_o[1] = mc_ref[1]
        ssd_in, mc_in, ssd_out, mc_out = ssd_ref.at[0], mc_ref.at[0], ssd_o.at[0], mc_o.at[0]
    else:
        _, _, sbo_ref, ssd_out, mc_out = rest
        ssd_in, mc_in = ssd_ref, mc_ref
    rows = rows_ref[...]
    xdt_t = rows[:, RW_XDT:RW_XDT + BW].T
    wk_t = rows[:, RW_WK:RW_WK + BW].T
    c_b = rows[:, RW_BC + 128:RW_BC + 256].astype(BF16)
    q_b = rows[:, RW_Q:RW_Q + BW].astype(BF16)
    for j in range(SAMPLE_BLOCK):
        y_off, q_c = [], []
        for g in range(2):
            hs2 = ssd_in[j, 2 * g:2 * g + 2].reshape(2 * HD, SSD_STATE)
            r = lax.dot_general(c_b[:, g * SSD_STATE:(g + 1) * SSD_STATE], hs2.astype(BF16),
                                (((1,), (1,)), ((), ())), preferred_element_type=F32)
            y_off.append(r[j:j + 1, :])
        for h in range(HEADS):
            g = h // 2
            b_row = rows[j:j + 1, RW_BC + g * SSD_STATE:RW_BC + (g + 1) * SSD_STATE]
            sdec = rows[j:j + 1, RW_SDEC + h:RW_SDEC + h + 1]
            ssd_out[j, h] = sdec * ssd_in[j, h] + xdt_t[h * HD:(h + 1) * HD, j:j + 1] * b_row

            cs = mc_in[j, h]
            r = jnp.dot(q_b[:, h * HD:(h + 1) * HD], cs.astype(BF16), preferred_element_type=F32)
            q_c.append(r[j:j + 1, :])
            v_row = rows[j:j + 1, RW_V + h * HD:RW_V + (h + 1) * HD]
            mdec = rows[j:j + 1, RW_MDEC + h:RW_MDEC + h + 1]
            mc_out[j, h] = mdec * cs + wk_t[h * HD:(h + 1) * HD, j:j + 1] * v_row
        sbo_ref[j:j + 1, :] = jnp.concatenate(y_off + q_c, axis=1)


def _sample_state(rows, layer, ssd_all, mc_all, prev=None):
    nb = rows.shape[0]
    blk = SAMPLE_BLOCK
    depth = ssd_all.shape[0]
    tile = (HEADS, HD, HD)
    rows_spec = pl.BlockSpec((blk, RW_W), lambda i: (i, 0))
    sbo_spec = pl.BlockSpec((blk, 2 * BW), lambda i: (i, 0))
    out_shape = (jax.ShapeDtypeStruct((nb, 2 * BW), F32),
                 jax.ShapeDtypeStruct(ssd_all.shape, F32),
                 jax.ShapeDtypeStruct(mc_all.shape, F32))
    params = pltpu.CompilerParams(dimension_semantics=("arbitrary",))
    if prev is None:
        assert layer == 0 and depth == 2
        both = pl.BlockSpec((depth, blk) + tile, lambda i: (0, i, 0, 0, 0))
        return pl.pallas_call(
            functools.partial(_sample_state_kernel, True),
            grid=(nb // blk,),
            in_specs=[rows_spec, both, both],
            out_specs=(sbo_spec, both, both),
            out_shape=out_shape,
            compiler_params=params,
            name="sample_state_first",
        )(rows, ssd_all, mc_all)
    one = pl.BlockSpec((None, blk) + tile, lambda i: (layer, i, 0, 0, 0))
    anywhere = pl.BlockSpec(memory_space=pl.ANY)
    return pl.pallas_call(
        functools.partial(_sample_state_kernel, False),
        grid=(nb // blk,),
        in_specs=[rows_spec, one, one, anywhere, anywhere],
        out_specs=(sbo_spec, one, one),
        out_shape=out_shape,
        input_output_aliases={3: 1, 4: 2},
        compiler_params=params,
        name="sample_state_next",
    )(rows, ssd_all, mc_all, prev[0], prev[1])


def _sample_post_kernel(x_ref, wpost_ref, vec_ref, mid_ref, sbo_ref, wglu_ref, wbr_ref, wout_ref, y_ref):
    x = x_ref[...]
    xb = x.astype(BF16)

    def post(lo, width):
        return jnp.dot(xb, wpost_ref[:, lo:lo + width], preferred_element_type=F32)

    def mid(i):
        return mid_ref[:, i * BW:(i + 1) * BW]

    y_pool, yssd_part, sdec_exp, sv, wi_exp, dmax_exp, y_s5 = [mid(i) for i in range(7)]
    y_ssd = yssd_part + sbo_ref[:, 0:BW] * sdec_exp
    hh = (sv + wi_exp * sbo_ref[:, BW:2 * BW]) / dmax_exp
    y_m = _sigmoid(post(O_OG, BW)) * hh
    y_s5 = _s5_glu(y_s5, wglu_ref)
    z_acts = [_silu(post(O_Z + b * BW, BW)) for b in range(4)]
    gates = [_sigmoid(post(O_GATE + b * D_MODEL, D_MODEL)) for b in range(4)]
    y_ref[...] = _merge_tail(x, z_acts, [y_pool, y_ssd, y_m, y_s5], gates, wbr_ref, wout_ref, vec_ref)


def _sample_post(x, layer, p, mid, sbo):
    nb = x.shape[0]
    full = lambda a: pl.BlockSpec(a.shape, lambda *_: (0,) * a.ndim)
    return pl.pallas_call(
        _sample_post_kernel,
        grid=(1,),
        in_specs=[full(x), _const_spec((D_MODEL, POST_W), layer), _const_spec((16, D_MODEL), layer),
                  full(mid), full(sbo), _const_spec((BW, 2 * BW), layer),
                  _const_spec((4, BW, D_MODEL), layer), _const_spec((D_MODEL, D_MODEL), layer)],
        out_specs=pl.BlockSpec((nb, D_MODEL), lambda *_: (0, 0)),
        out_shape=jax.ShapeDtypeStruct((nb, D_MODEL), F32),
        compiler_params=pltpu.CompilerParams(dimension_semantics=("arbitrary",),
                                             vmem_limit_bytes=VMEM_LIMIT),
        name="sample_post",
    )(x, p["wpost"], p["vec"], mid, sbo, p["wglu"], p["wbr"], p["wout"])


def _prepare_params(w_in, w_pool, pool_scale, conv_w, conv_b, dt_bias, a_log, d_ssd, ig_bias, fg_bias,
                    lam_re, lam_im, b_re, b_im, c_re, c_im, log_dt, d_s5, w_glu, w_br, w_out, ln_g, ln_b):
    depth = w_in.shape[0]
    sizes = (256, 256, 512, 4, 256, 256, 256, 256, 4, 4, 256, 256, 256, 256, 4096)
    offs = [0]
    for s in sizes:
        offs.append(offs[-1] + s)
    col = lambda i: w_in[:, :, offs[i]:offs[i + 1]]
    small = jnp.concatenate([col(3), col(8), col(9), jnp.zeros((depth, D_MODEL, LANES - 12), w_in.dtype)], axis=2)
    wpre = jnp.concatenate([col(0), col(2), col(5), col(6), col(7), col(12), small], axis=2).astype(BF16)
    wpost = jnp.concatenate([col(1), col(4), col(11), col(13), col(10), col(14)], axis=2).astype(BF16)

    zeros = lambda n: jnp.zeros((depth, n), F32)
    misc = jnp.concatenate([pool_scale, jnp.repeat(d_ssd, HD, axis=1), d_s5,
                            dt_bias, ig_bias, fg_bias, zeros(LANES - 12),
                            a_log, zeros(LANES - HEADS)], axis=1)
    pad512 = lambda a: jnp.concatenate([a, jnp.zeros(a.shape[:-1] + (D_MODEL - CONV_DIM,), F32)], axis=-1)
    vec = jnp.concatenate([ln_g[:, None], ln_b[:, None], misc[:, None], pad512(conv_b)[:, None],
                           pad512(conv_w), jnp.zeros((depth, 16 - 4 - CONV_K, D_MODEL), F32)], axis=1)

    eye4 = jnp.eye(len(POOL_WINDOWS), dtype=F32)
    wpool = jnp.einsum("lgce,gh->lgche", w_pool, eye4).reshape(depth, BW, BW).astype(BF16)

    lam2 = jnp.stack([lam_re.reshape(depth, S5_W), lam_im.reshape(depth, S5_W)], axis=1)
    ldt = jnp.repeat(log_dt, S5_STATE, axis=1)[:, None]
    to_rows = lambda b: jnp.transpose(b, (0, 3, 1, 2)).reshape(depth, S5_CH, S5_W)
    bcat = jnp.concatenate([to_rows(b_re), to_rows(b_im)], axis=1)

    eye_g = jnp.eye(S5_GROUPS, dtype=F32)
    cb = lambda cm: jnp.einsum("lgcp,gh->lgphc", cm, eye_g).reshape(depth, S5_W, BW)
    cbd = jnp.concatenate([cb(c_re), -cb(c_im)], axis=1).astype(BF16)

    lamp, bbd = [], []
    for l in range(depth):
        lp, bb = _s5_prep(lam2[l], ldt[l], bcat[l])
        lamp.append(lp)
        bbd.append(bb)
    return dict(wpre=wpre, wpost=wpost, vec=vec, wpool=wpool, lamp=jnp.stack(lamp), bbd=jnp.stack(bbd),
                cbd=cbd, wglu=w_glu.astype(BF16), wbr=w_br.astype(BF16), wout=w_out.astype(BF16))


def kernel(x_prompt, x_sample, state_pool, state_ssd_conv, state_ssd, state_mlstm_C, state_mlstm_n, state_mlstm_m, state_s5_re, state_s5_im, w_in, w_pool, pool_scale, conv_w, conv_b, dt_bias, a_log, d_ssd, ig_bias, fg_bias, lam_re, lam_im, b_re, b_im, c_re, c_im, log_dt, d_s5, w_glu, w_br, w_out, ln_g, ln_b):
    p = _prepare_params(w_in, w_pool, pool_scale, conv_w, conv_b, dt_bias, a_log, d_ssd, ig_bias, fg_bias,
                        lam_re, lam_im, b_re, b_im, c_re, c_im, log_dt, d_s5, w_glu, w_br, w_out, ln_g, ln_b)
    depth = w_in.shape[0]
    bsz = x_prompt.shape[0]
    nb = x_sample.shape[0]
    yp = x_prompt
    ys = x_sample.reshape(nb, D_MODEL)
    outs_p = [[] for _ in range(8)]
    outs_s = [[] for _ in range(8)]
    mats_s = None
    for l in range(depth):
        yp, pool_p, conv_p, ssd_p, mc_p, mn_p, mm_p, re_p, im_p = _prompt_layer(yp, l, p)
        new_p = (pool_p, conv_p, ssd_p, mc_p, mn_p.reshape(bsz, HEADS, HD), mm_p[:, 0, :HEADS],
                 re_p.reshape(bsz, S5_GROUPS, S5_STATE), im_p.reshape(bsz, S5_GROUPS, S5_STATE))

        rows, mid, pool_s, conv_s, n_s, m_s, re_s, im_s = _sample_pre(
            ys, l, p,
            state_pool[l].reshape(nb, POOL_BUF * BW), state_ssd_conv[l].reshape(nb, (CONV_K - 1) * CONV_DIM),
            state_mlstm_n[l].reshape(nb, BW), state_mlstm_m[l],
            state_s5_re[l].reshape(nb, S5_W), state_s5_im[l].reshape(nb, S5_W))
        sbo, ssd_s, mc_s = _sample_state(rows, l, state_ssd, state_mlstm_C, mats_s)
        mats_s = (ssd_s, mc_s)
        ys = _sample_post(ys, l, p, mid, sbo)
        new_s = (pool_s.reshape(nb, POOL_BUF, BW), conv_s.reshape(nb, CONV_K - 1, CONV_DIM), None, None,
                 n_s.reshape(nb, HEADS, HD), m_s[:, :HEADS],
                 re_s.reshape(nb, S5_GROUPS, S5_STATE), im_s.reshape(nb, S5_GROUPS, S5_STATE))
        for i in range(8):
            outs_p[i].append(new_p[i])
            outs_s[i].append(new_s[i])
    sp = [jnp.stack(o, axis=0) for o in outs_p]
    ss = [mats_s[i - 2] if i in (2, 3) else jnp.stack(outs_s[i], axis=0) for i in range(8)]
    out = [yp, ys.reshape(nb, 1, D_MODEL)]
    for i in range(8):
        out.append(sp[i])
        out.append(ss[i])
    return tuple(out)
```

```python
import functools

import jax
import jax.numpy as jnp
from jax import lax
from jax.experimental import pallas as pl
from jax.experimental.pallas import tpu as pltpu

F32 = jnp.float32
BF16 = jnp.bfloat16

D_MODEL = 1024
DEPTH = 2
PAST_LEN = 16384
BW = 256
POOL_WINDOWS = (2, 4, 8, 16)
POOL_BUF = 15
POOL_PAD = 32
HEADS = 4
HD = 64
SSD_STATE = 64
CONV_K = 4
CONV_DIM = 512
S5_GROUPS = 16
S5_CH = 16
S5_STATE = 64
S5_W = S5_GROUPS * S5_STATE
CHUNK = 128
TILE = 512
ALPHA = (2.0 * DEPTH) ** 0.25
LN_EPS = 1e-5

SUBLANES = 8
LANES = 128
VMEM_LIMIT = 62 * 1024 * 1024

PRE_W = 1920
POST_W = 5376
O_PU, O_XBC, O_Q, O_K, O_V, O_SU, O_SM = 0, 256, 768, 1024, 1280, 1536, 1792
O_Z, O_OG, O_GATE = 0, 1024, 1280

R_LAM = 0
R_P = 8
R_M = 24
LAMP_ROWS = R_M + 3 * 16

V_LNG, V_LNB, V_MISC, V_CONVB, V_CONVW = 0, 1, 2, 3, 4


def _dot(a, b):
    return jnp.dot(a.astype(BF16), b.astype(BF16), preferred_element_type=F32)


def _dot_nt(a, b):
    return lax.dot_general(a.astype(BF16), b.astype(BF16), (((1,), (1,)), ((), ())),
                           preferred_element_type=F32)


def _softplus(x):
    return jnp.logaddexp(x, 0.0)


def _sigmoid(x):
    return 0.5 * jnp.tanh(0.5 * x) + 0.5


def _silu(x):
    return x * _sigmoid(x)


def _expand_heads(cols, width=HD):
    rows = cols[0].shape[0]
    lane = lax.broadcasted_iota(jnp.int32, (rows, HEADS * width), 1)
    out = jnp.broadcast_to(cols[HEADS - 1], (rows, HEADS * width))
    for h in range(HEADS - 2, -1, -1):
        out = jnp.where(lane < (h + 1) * width, jnp.broadcast_to(cols[h], (rows, HEADS * width)), out)
    return out


def _pool_select(w2, w4, w8, w16):
    lane = lax.broadcasted_iota(jnp.int32, w2.shape, 1)
    return jnp.where(lane < 64, w2, jnp.where(lane < 128, w4, jnp.where(lane < 192, w8, w16)))


def _pool_window_row():
    lane = lax.broadcasted_iota(jnp.int32, (1, BW), 1)
    return jnp.where(lane < 64, 2, jnp.where(lane < 128, 4, jnp.where(lane < 192, 8, 16)))


def _small_block(sm, vec_ref):
    bias = vec_ref[V_MISC:V_MISC + 1, 768:896]
    alog = vec_ref[V_MISC:V_MISC + 1, 896:1024]
    v = sm + bias
    sp = _softplus(v)
    lsg = -_softplus(-v)
    a_row = -jnp.exp(alog)
    return v, sp, lsg, a_row


def _merge_tail(x, z_acts, ys, gates, wbr_ref, wout_ref, vec_ref):
    merged = None
    for b in range(4):
        br = ys[b] * z_acts[b]
        pb = _dot(br, wbr_ref[b])
        gb = gates[b][...]
        merged = gb * pb if merged is None else merged + gb * pb
    out = _dot(merged, wout_ref[...])
    r = ALPHA * x + out
    mu = jnp.mean(r, axis=-1, keepdims=True)
    var = jnp.mean(jnp.square(r - mu), axis=-1, keepdims=True)
    g = vec_ref[V_LNG:V_LNG + 1, :]
    b = vec_ref[V_LNB:V_LNB + 1, :]
    return (r - mu) * lax.rsqrt(var + LN_EPS) * g + b


def _s5_glu(y_s5, wglu_ref):
    glu = _dot(jax.nn.gelu(y_s5), wglu_ref[...])
    return glu[:, :BW] * _sigmoid(glu[:, BW:])


def _s5_prep_kernel(lam_ref, ldt_ref, b_ref, lamp_ref, bbd_ref):
    lr = lam_ref[0:1, :]
    li = lam_ref[1:2, :]
    step = jnp.exp(ldt_ref[0:1, :])
    e = jnp.exp(lr * step)
    lbr = e * jnp.cos(li * step)
    lbi = e * jnp.sin(li * step)
    den = lr * lr + li * li
    nr = lbr - 1.0
    qr = (nr * lr + lbi * li) / den
    qi = (lbi * lr - nr * li) / den
    bre = b_ref[0:S5_CH, :]
    bim = b_ref[S5_CH:2 * S5_CH, :]
    bb = jnp.concatenate([qr * bre - qi * bim, qr * bim + qi * bre], axis=1)
    tiled = jnp.concatenate([bb] * S5_GROUPS, axis=0)
    rowg = lax.broadcasted_iota(jnp.int32, tiled.shape, 0) // S5_CH
    colg = (lax.broadcasted_iota(jnp.int32, tiled.shape, 1) % S5_W) // S5_STATE
    bbd_ref[...] = jnp.where(rowg == colg, tiled, 0.0).astype(BF16)

    pows = [(lbr, lbi)]
    for _ in range(7):
        pr, pi = pows[-1]
        pows.append((pr * lbr - pi * lbi, pr * lbi + pi * lbr))
    row = lax.broadcasted_iota(jnp.int32, (SUBLANES, S5_W), 0)

    def bcast(v):
        return jnp.broadcast_to(v, (SUBLANES, S5_W))

    p_re = bcast(pows[7][0])
    p_im = bcast(pows[7][1])
    for j in range(6, -1, -1):
        p_re = jnp.where(row == j, bcast(pows[j][0]), p_re)
        p_im = jnp.where(row == j, bcast(pows[j][1]), p_im)
    lamp_ref[R_LAM:R_LAM + SUBLANES, :] = jnp.where(row == 0, bcast(lbr), jnp.where(row == 1, bcast(lbi), 0.0))
    lamp_ref[R_P:R_P + SUBLANES, :] = p_re
    lamp_ref[R_P + SUBLANES:R_P + 2 * SUBLANES, :] = p_im
    for i, d in enumerate((1, 2, 4)):
        base = R_M + 16 * i
        lamp_ref[base:base + SUBLANES, :] = jnp.where(row >= d, bcast(pows[d - 1][0]), 0.0)
        lamp_ref[base + SUBLANES:base + 2 * SUBLANES, :] = jnp.where(row >= d, bcast(pows[d - 1][1]), 0.0)


def _s5_prep(lam2, ldt, bcat):
    return pl.pallas_call(
        _s5_prep_kernel,
        out_shape=(jax.ShapeDtypeStruct((LAMP_ROWS, S5_W), F32),
                   jax.ShapeDtypeStruct((BW, 2 * S5_W), BF16)),
        name="s5_prep",
    )(lam2, ldt, bcat)


def _prompt_kernel(x_ref, wpre_ref, wpost_ref, vec_ref, wpool_ref, lamp_ref, bbd_ref, cbd_ref,
                   wglu_ref, wbr_ref, wout_ref,
                   y_ref, pool_o, conv_o, ssd_o, mc_o, mn_o, mm_o, s5re_o, s5im_o,
                   poolh, poolw, convh, ssd_st, mc_st, nrow, mrow, s5cr, s5ci, gsc):
    T = x_ref.shape[1]
    Q = CHUNK
    c = pl.program_id(1)
    last = pl.num_programs(1) - 1

    @pl.when(c == 0)
    def _():
        poolh[0:POOL_PAD, :] = jnp.zeros((POOL_PAD, BW), F32)
        convh[0:SUBLANES, :] = jnp.zeros((SUBLANES, CONV_DIM), F32)
        ssd_st[...] = jnp.zeros(ssd_st.shape, F32)
        mc_st[...] = jnp.zeros(mc_st.shape, F32)
        nrow[...] = jnp.zeros(nrow.shape, F32)
        mrow[...] = jnp.zeros(mrow.shape, F32)
        s5cr[...] = jnp.zeros(s5cr.shape, F32)
        s5ci[...] = jnp.zeros(s5ci.shape, F32)

    x = x_ref[0]
    xb = x.astype(BF16)

    def pre(lo, width):
        return jnp.dot(xb, wpre_ref[:, lo:lo + width], preferred_element_type=F32)

    def post(lo, width):
        return jnp.dot(xb, wpost_ref[:, lo:lo + width], preferred_element_type=F32)

    rows_i = lax.broadcasted_iota(jnp.int32, (Q, Q), 0)
    cols_i = lax.broadcasted_iota(jnp.int32, (Q, Q), 1)
    causal = rows_i >= cols_i

    n_rows = POOL_PAD + T

    def pool_mix(pu):
        poolh[POOL_PAD:n_rows, :] = pu
        poolw[0, 8:n_rows, :] = poolh[8:n_rows, :] + poolh[7:n_rows - 1, :]
        poolw[1, 16:n_rows, :] = poolw[0, 16:n_rows, :] + poolw[0, 14:n_rows - 2, :]
        poolw[2, 24:n_rows, :] = poolw[1, 24:n_rows, :] + poolw[1, 20:n_rows - 4, :]
        w2 = poolw[0, POOL_PAD:n_rows, :]
        w4 = poolw[1, POOL_PAD:n_rows, :]
        w8 = poolw[2, POOL_PAD:n_rows, :]
        w16 = w8 + poolw[2, POOL_PAD - 8:n_rows - 8, :]
        pos = c * T + lax.broadcasted_iota(jnp.int32, (T, BW), 0)
        cnt = jnp.minimum(pos + 1, _pool_window_row()).astype(F32)
        dpool = _pool_select(w2, w4, w8, w16) / cnt - pu
        poolh[0:POOL_PAD, :] = poolh[T:n_rows, :]
        return _dot(dpool, wpool_ref[...]) * vec_ref[V_MISC:V_MISC + 1, 0:256]

    def conv_mix(xbc_raw):
        convh[SUBLANES:SUBLANES + T, :] = xbc_raw
        acc = vec_ref[V_CONVB:V_CONVB + 1, 0:CONV_DIM]
        for kk in range(CONV_K):
            acc = acc + convh[5 + kk:5 + kk + T, :] * vec_ref[V_CONVW + kk:V_CONVW + kk + 1, 0:CONV_DIM]
        convh[0:SUBLANES, :] = convh[T:T + SUBLANES, :]
        return _silu(acc)

    v_sm, sp_all, lsg, a_row = _small_block(pre(O_SM, LANES), vec_ref)
    lane_s = lax.broadcasted_iota(jnp.int32, (T, LANES), 1)
    g_all = jnp.where(lane_s < 4, sp_all * a_row, jnp.where(lane_s < 8, v_sm, jnp.where(lane_s < 12, lsg, 0.0)))
    tril = causal.astype(BF16)
    g_hi = g_all.astype(BF16)
    g_r1 = g_all - g_hi.astype(F32)
    g_mid = g_r1.astype(BF16)
    g_lo = (g_r1 - g_mid.astype(F32)).astype(BF16)
    cums = []
    for s in range(T // Q):
        rs = slice(s * Q, (s + 1) * Q)
        cum = (jnp.dot(tril, g_hi[rs], preferred_element_type=F32)
               + jnp.dot(tril, g_mid[rs], preferred_element_type=F32)
               + jnp.dot(tril, g_lo[rs], preferred_element_type=F32))
        cums.append((cum, cum.T, g_all[rs].T))

    proj = {}
    fillers = [
        lambda: proj.update(pu=pre(O_PU, BW), xbc=pre(O_XBC, CONV_DIM)),
        lambda: proj.update(q=pre(O_Q, BW), k=pre(O_K, BW) * (HD ** -0.5)),
        lambda: proj.update(v=pre(O_V, BW), zs=[post(O_Z + b * BW, BW) for b in range(4)]),
        lambda: proj.update(og=post(O_OG, BW)),
    ]
    y_s5 = _s5_chunk_scan(pre(O_SU, BW), fillers, lamp_ref, bbd_ref, cbd_ref, vec_ref, s5cr, s5ci)
    zs = proj["zs"]

    dyn0 = jnp.minimum(c, 0)
    gate_raw = post(O_GATE, D_MODEL)
    y_pool = pool_mix(proj["pu"])
    gsc[dyn0] = _sigmoid(gate_raw)
    gate_raw = post(O_GATE + D_MODEL, D_MODEL)
    xbc = conv_mix(proj["xbc"])
    gsc[dyn0 + 1] = _sigmoid(gate_raw)
    gate_raw = post(O_GATE + 2 * D_MODEL, D_MODEL)
    gsc[dyn0 + 2] = _sigmoid(gate_raw)
    gate_raw = post(O_GATE + 3 * D_MODEL, D_MODEL)
    gsc[dyn0 + 3] = _sigmoid(gate_raw)
    gates = [gsc.at[dyn0 + b] for b in range(4)]

    y_ssd_parts, h_m_parts = [], []
    for s in range(T // Q):
        y_c, h_c = _ssd_mlstm_chunk(
            xbc[s * Q:(s + 1) * Q], proj["q"][s * Q:(s + 1) * Q], proj["k"][s * Q:(s + 1) * Q],
            proj["v"][s * Q:(s + 1) * Q], sp_all[s * Q:(s + 1) * Q], g_all[s * Q:(s + 1) * Q],
            cums[s], causal, vec_ref, ssd_st, mc_st, nrow, mrow)
        y_ssd_parts.append(y_c)
        h_m_parts.append(h_c)
    y_ssd = jnp.concatenate(y_ssd_parts, axis=0)
    y_m = _sigmoid(proj["og"]) * jnp.concatenate(h_m_parts, axis=0)
    y_s5 = _s5_glu(y_s5, wglu_ref)

    z_acts = [_silu(z) for z in zs]
    y_ref[0] = _merge_tail(x, z_acts, [y_pool, y_ssd, y_m, y_s5], gates, wbr_ref, wout_ref, vec_ref)

    @pl.when(c == last)
    def _():
        pool_o[0] = poolh[POOL_PAD - POOL_BUF:POOL_PAD, :]
        conv_o[0] = convh[5:8, :]
        ssd_o[0] = ssd_st[...]
        mc_o[0] = mc_st[...]
        mn_o[0] = nrow[0:1, :]
        mm_o[0] = mrow[0:1, :]
        s5re_o[0] = s5cr[0:1, :]
        s5im_o[0] = s5ci[0:1, :]


def _ssd_mlstm_chunk(xbc, q, k, v, sp, g_blk, cums, causal, vec_ref, ssd_st, mc_st, nrow, mrow):
    T = xbc.shape[0]
    cum, cum_t, g_t = cums
    xs = xbc[:, 0:BW]
    dt_cols = [sp[:, h:h + 1] for h in range(HEADS)]
    acs_cols = [cum[:, h:h + 1] for h in range(HEADS)]
    acs_last = [cum[T - 1:T, h:h + 1] for h in range(HEADS)]
    xdt = xs * _expand_heads(dt_cols)
    xw_t = (xdt * _expand_heads([jnp.exp(acs_last[h] - acs_cols[h]) for h in range(HEADS)])).T
    d_ssd = vec_ref[V_MISC:V_MISC + 1, 256:512]
    m_row = mrow[0:1, :]
    n_row = nrow[0:1, :]
    hsl = [slice(h * HD, (h + 1) * HD) for h in range(HEADS)]
    b_gs = [xbc[:, BW + g * SSD_STATE:BW + (g + 1) * SSD_STATE] for g in range(2)]
    c_gs = [xbc[:, BW + 128 + g * SSD_STATE:BW + 128 + (g + 1) * SSD_STATE] for g in range(2)]

    cbs = [_dot_nt(c_gs[g], b_gs[g]) for g in range(2)]
    offs = [_dot_nt(c_gs[g], ssd_st[2 * g:2 * g + 2].reshape(2 * HD, SSD_STATE)) for g in range(2)]
    qks = [_dot_nt(q[:, hsl[h]], k[:, hsl[h]]) for h in range(HEADS)]
    qcs = [_dot(q[:, hsl[h]], mc_st[h]) for h in range(HEADS)]
    gms = [cbs[h // 2] * jnp.exp(jnp.where(causal, acs_cols[h] - cum_t[h:h + 1, :], -jnp.inf))
           for h in range(HEADS)]
    b_cols = [cum[:, 8 + h:9 + h] for h in range(HEADS)]
    m_prevs = [m_row[:, h:h + 1] for h in range(HEADS)]
    dlogs = [jnp.where(causal, b_cols[h] - cum_t[8 + h:9 + h, :] + g_t[4 + h:5 + h, :], -jnp.inf)
             for h in range(HEADS)]
    inters = [b_cols[h] + m_prevs[h] for h in range(HEADS)]
    m_ts = [jnp.maximum(inters[h], jnp.max(dlogs[h], axis=1, keepdims=True)) for h in range(HEADS)]
    ss = [qks[h] * jnp.exp(dlogs[h] - m_ts[h]) for h in range(HEADS)]
    wis = [jnp.exp(inters[h] - m_ts[h]) for h in range(HEADS)]
    y_diags = [_dot(gms[h], xdt[:, hsl[h]]) for h in range(HEADS)]
    svs = [_dot(ss[h], v[:, hsl[h]]) for h in range(HEADS)]
    y_off = jnp.concatenate(offs, axis=1) * _expand_heads([jnp.exp(acs_cols[h]) for h in range(HEADS)])
    y_ssd = jnp.concatenate(y_diags, axis=1) + y_off + d_ssd * xs
    h_heads, wk_parts, decays, m_news = [], [], [], []
    for h in range(HEADS):
        num = svs[h] + wis[h] * qcs[h]
        den = (jnp.sum(ss[h], axis=1, keepdims=True)
               + wis[h] * jnp.sum(q[:, hsl[h]] * n_row[:, hsl[h]], axis=1, keepdims=True))
        h_heads.append(num / jnp.maximum(jnp.abs(den), jnp.exp(-m_ts[h])))
        m_new = m_ts[h][T - 1:T, :]
        b_last = b_cols[h][T - 1:T, :]
        wk_parts.append(jnp.exp(b_last - b_cols[h] + g_blk[:, 4 + h:5 + h] - m_new))
        decays.append(jnp.exp(b_last + m_prevs[h] - m_new))
        m_news.append(m_new)
    for h in range(HEADS):
        ssd_st[h] = jnp.exp(acs_last[h]) * ssd_st[h] + _dot(xw_t[hsl[h], :], b_gs[h // 2])
    lane_m = lax.broadcasted_iota(jnp.int32, (1, LANES), 1)
    m_row_new = jnp.zeros((1, LANES), F32)
    for h in range(HEADS):
        m_row_new = jnp.where(lane_m == h, jnp.broadcast_to(m_news[h], (1, LANES)), m_row_new)
    mrow[0:1, :] = m_row_new
    wk = k * _expand_heads(wk_parts)
    wk_t = wk.T
    for h in range(HEADS):
        mc_st[h] = decays[h] * mc_st[h] + _dot(wk_t[h * HD:(h + 1) * HD, :], v[:, h * HD:(h + 1) * HD])
    nrow[0:1, :] = _expand_heads(decays) * n_row + jnp.sum(wk, axis=0, keepdims=True)
    return y_ssd, jnp.concatenate(h_heads, axis=1)


def _s5_chunk_scan(su, fillers, lamp_ref, bbd_ref, cbd_ref, vec_ref, s5cr, s5ci):
    T = su.shape[0]
    bu = _dot(su, bbd_ref[...])
    nblk = T // SUBLANES
    h_re = bu[:, 0:S5_W].reshape(nblk, SUBLANES, S5_W)
    h_im = bu[:, S5_W:2 * S5_W].reshape(nblk, SUBLANES, S5_W)
    for i, d in enumerate((1, 2, 4)):
        base = R_M + 16 * i
        m_re = lamp_ref[base:base + SUBLANES, :]
        m_im = lamp_ref[base + SUBLANES:base + 2 * SUBLANES, :]
        r_re = pltpu.roll(h_re, d, 1)
        r_im = pltpu.roll(h_im, d, 1)
        h_re, h_im = h_re + (m_re * r_re - m_im * r_im), h_im + (m_re * r_im + m_im * r_re)
        fillers[i]()
    p_re = lamp_ref[R_P:R_P + SUBLANES, :]
    p_im = lamp_ref[R_P + SUBLANES:R_P + 2 * SUBLANES, :]
    cr = s5cr[0:1, :]
    ci = s5ci[0:1, :]
    blk_re = []
    blk_im = []
    for j in range(nblk):
        br = h_re[j] + (p_re * cr - p_im * ci)
        bi = h_im[j] + (p_re * ci + p_im * cr)
        blk_re.append(br)
        blk_im.append(bi)
        cr = br[SUBLANES - 1:SUBLANES, :]
        ci = bi[SUBLANES - 1:SUBLANES, :]
    s5cr[0:1, :] = cr
    s5ci[0:1, :] = ci
    fillers[3]()
    h_all = jnp.concatenate([jnp.concatenate(blk_re, axis=0), jnp.concatenate(blk_im, axis=0)], axis=1)
    return _dot(h_all, cbd_ref[...]) + vec_ref[V_MISC:V_MISC + 1, 512:768] * su


def _const_spec(shape, layer):
    nd = len(shape)
    return pl.BlockSpec((None,) + tuple(shape), lambda *_: (layer,) + (0,) * nd,
                        pipeline_mode=pl.Buffered(1))


def _prompt_layer(x, layer, p):
    bsz, seq, _ = x.shape
    T = TILE if seq % TILE == 0 else CHUNK
    nc = seq // T

    def bmap(nd):
        return lambda b, c: (b,) + (0,) * nd

    out_shape = (
        jax.ShapeDtypeStruct((bsz, seq, D_MODEL), F32),
        jax.ShapeDtypeStruct((bsz, POOL_BUF, BW), F32),
        jax.ShapeDtypeStruct((bsz, CONV_K - 1, CONV_DIM), F32),
        jax.ShapeDtypeStruct((bsz, HEADS, HD, SSD_STATE), F32),
        jax.ShapeDtypeStruct((bsz, HEADS, HD, HD), F32),
        jax.ShapeDtypeStruct((bsz, 1, BW), F32),
        jax.ShapeDtypeStruct((bsz, 1, LANES), F32),
        jax.ShapeDtypeStruct((bsz, 1, S5_W), F32),
        jax.ShapeDtypeStruct((bsz, 1, S5_W), F32),
    )
    out_specs = (
        pl.BlockSpec((1, T, D_MODEL), lambda b, c: (b, c, 0)),
        pl.BlockSpec((1, POOL_BUF, BW), bmap(2)),
        pl.BlockSpec((1, CONV_K - 1, CONV_DIM), bmap(2)),
        pl.BlockSpec((1, HEADS, HD, SSD_STATE), bmap(3)),
        pl.BlockSpec((1, HEADS, HD, HD), bmap(3)),
        pl.BlockSpec((1, 1, BW), bmap(2)),
        pl.BlockSpec((1, 1, LANES), bmap(2)),
        pl.BlockSpec((1, 1, S5_W), bmap(2)),
        pl.BlockSpec((1, 1, S5_W), bmap(2)),
    )
    in_specs = [
        pl.BlockSpec((1, T, D_MODEL), lambda b, c: (b, c, 0)),
        _const_spec((D_MODEL, PRE_W), layer),
        _const_spec((D_MODEL, POST_W), layer),
        _const_spec((16, D_MODEL), layer),
        _const_spec((BW, BW), layer),
        _const_spec((LAMP_ROWS, S5_W), layer),
        _const_spec((BW, 2 * S5_W), layer),
        _const_spec((2 * S5_W, BW), layer),
        _const_spec((BW, 2 * BW), layer),
        _const_spec((4, BW, D_MODEL), layer),
        _const_spec((D_MODEL, D_MODEL), layer),
    ]
    scratch = [
        pltpu.VMEM((T + POOL_PAD, BW), F32),
        pltpu.VMEM((3, T + POOL_PAD, BW), F32),
        pltpu.VMEM((T + SUBLANES, CONV_DIM), F32),
        pltpu.VMEM((HEADS, HD, SSD_STATE), F32),
        pltpu.VMEM((HEADS, HD, HD), F32),
        pltpu.VMEM((SUBLANES, BW), F32),
        pltpu.VMEM((SUBLANES, LANES), F32),
        pltpu.VMEM((SUBLANES, S5_W), F32),
        pltpu.VMEM((SUBLANES, S5_W), F32),
        pltpu.VMEM((4, T, D_MODEL), F32),
    ]
    return pl.pallas_call(
        _prompt_kernel,
        grid=(bsz, nc),
        in_specs=in_specs,
        out_specs=out_specs,
        out_shape=out_shape,
        scratch_shapes=scratch,
        compiler_params=pltpu.CompilerParams(dimension_semantics=("arbitrary", "arbitrary"),
                                             vmem_limit_bytes=VMEM_LIMIT),
        name="prompt_layer",
    )(x, p["wpre"], p["wpost"], p["vec"], p["wpool"], p["lamp"], p["bbd"], p["cbd"],
      p["wglu"], p["wbr"], p["wout"])


RW_BC, RW_XDT, RW_Q, RW_WK, RW_V, RW_SDEC, RW_MDEC, RW_W = 0, 256, 512, 768, 1024, 1280, 1408, 1536
MID_W = 7 * BW


def _sample_pre_kernel(x_ref, wpre_ref, vec_ref, wpool_ref, lamp_ref, bbd_ref, cbd_ref,
                       pool_ref, conv_ref, n_ref, m_ref, s5re_ref, s5im_ref,
                       rows_o, mid_o, pool_o, conv_o, n_o, m_o, s5re_o, s5im_o):
    x = x_ref[...]
    nb = x.shape[0]
    pp = _dot(x, wpre_ref[...])
    pu = pp[:, O_PU:O_PU + BW]
    xbc_raw = pp[:, O_XBC:O_XBC + CONV_DIM]
    q = pp[:, O_Q:O_Q + BW]
    k = pp[:, O_K:O_K + BW] * (HD ** -0.5)
    v = pp[:, O_V:O_V + BW]
    su = pp[:, O_SU:O_SU + BW]

    def ext(j):
        return pool_ref[j]

    w2 = pu + ext(14)
    w4 = w2 + ext(13) + ext(12)
    w8 = w4
    for j in range(11, 7, -1):
        w8 = w8 + ext(j)
    w16 = w8
    for j in range(7, -1, -1):
        w16 = w16 + ext(j)
    cnt = jnp.minimum(PAST_LEN + 1, _pool_window_row()).astype(F32)
    dpool = _pool_select(w2, w4, w8, w16) / cnt - pu
    y_pool = _dot(dpool, wpool_ref[...]) * vec_ref[V_MISC:V_MISC + 1, 0:256]
    for j in range(POOL_BUF - 1):
        pool_o[j] = pool_ref[j + 1]
    pool_o[POOL_BUF - 1] = pu

    acc = vec_ref[V_CONVB:V_CONVB + 1, 0:CONV_DIM]
    for kk in range(CONV_K - 1):
        acc = acc + conv_ref[kk] * vec_ref[V_CONVW + kk:V_CONVW + kk + 1, 0:CONV_DIM]
    acc = acc + xbc_raw * vec_ref[V_CONVW + CONV_K - 1:V_CONVW + CONV_K, 0:CONV_DIM]
    for kk in range(CONV_K - 2):
        conv_o[kk] = conv_ref[kk + 1]
    conv_o[CONV_K - 2] = xbc_raw
    xbc = _silu(acc)
    xs = xbc[:, 0:BW]
    bc = xbc[:, BW:2 * BW]

    v_sm, sp, lsg, a_row = _small_block(pp[:, O_SM:O_SM + LANES], vec_ref)
    sdec_blk = jnp.exp(sp * a_row)
    dt_cols = [sp[:, h:h + 1] for h in range(HEADS)]
    xdt = xs * _expand_heads(dt_cols)
    cb_cols = []
    for g in range(2):
        b_g = bc[:, g * SSD_STATE:(g + 1) * SSD_STATE]
        c_g = bc[:, 128 + g * SSD_STATE:128 + (g + 1) * SSD_STATE]
        cb_cols.append(jnp.sum(c_g * b_g, axis=1, keepdims=True))
    yssd_part = (_expand_heads([cb_cols[h // 2] for h in range(HEADS)]) * xdt
                 + vec_ref[V_MISC:V_MISC + 1, 256:512] * xs)
    sdec_exp = _expand_heads([sdec_blk[:, h:h + 1] for h in range(HEADS)])

    n0 = n_ref[...].T
    m0 = jnp.concatenate([m_ref[...], jnp.zeros((SUBLANES - HEADS, nb), F32)], axis=0).T
    s_cols, wi_cols, dw_cols, dmax_cols, m_cols = [], [], [], [], []
    for h in range(HEADS):
        sl = slice(h * HD, (h + 1) * HD)
        ig = v_sm[:, 4 + h:5 + h]
        lf = lsg[:, 8 + h:9 + h]
        inter = lf + m0[:, h:h + 1]
        m_t = jnp.maximum(inter, (lf - lf) + ig)
        dw = jnp.exp((lf - lf) + ig - m_t)
        wi = jnp.exp(inter - m_t)
        s = jnp.sum(q[:, sl] * k[:, sl], axis=1, keepdims=True) * dw
        den = s + wi * jnp.sum(q[:, sl] * n0[:, sl], axis=1, keepdims=True)
        s_cols.append(s)
        wi_cols.append(wi)
        dw_cols.append(dw)
        dmax_cols.append(jnp.maximum(jnp.abs(den), jnp.exp(-m_t)))
        m_cols.append(m_t)
    wi_exp = _expand_heads(wi_cols)
    wk = _expand_heads(dw_cols) * k
    n_o[...] = (wi_exp * n0 + wk).T
    lane = lax.broadcasted_iota(jnp.int32, (nb, LANES), 1)
    m_blk = jnp.zeros((nb, LANES), F32)
    wi_blk = jnp.zeros((nb, LANES), F32)
    for h in range(HEADS):
        m_blk = jnp.where(lane == h, jnp.broadcast_to(m_cols[h], (nb, LANES)), m_blk)
        wi_blk = jnp.where(lane == h, jnp.broadcast_to(wi_cols[h], (nb, LANES)), wi_blk)
    m_o[...] = m_blk.T[0:HEADS, :]

    bu = _dot(su, bbd_ref[...])
    lbr = lamp_ref[R_LAM:R_LAM + 1, :]
    lbi = lamp_ref[R_LAM + 1:R_LAM + 2, :]
    h0r = s5re_ref[...].T
    h0i = s5im_ref[...].T
    h_re = bu[:, 0:S5_W] + (lbr * h0r - lbi * h0i)
    h_im = bu[:, S5_W:2 * S5_W] + (lbr * h0i + lbi * h0r)
    s5re_o[...] = h_re.T
    s5im_o[...] = h_im.T
    y_s5 = _dot(jnp.concatenate([h_re, h_im], axis=1), cbd_ref[...]) + vec_ref[V_MISC:V_MISC + 1, 512:768] * su

    rows_o[...] = jnp.concatenate([bc, xdt, q, wk, v, sdec_blk, wi_blk], axis=1).T
    mid_o[...] = jnp.concatenate([y_pool, yssd_part, sdec_exp, _expand_heads(s_cols) * v, wi_exp,
                                  _expand_heads(dmax_cols), y_s5], axis=1)


def _sample_pre(x, layer, p, pool2, conv2, n2, m2, s5re2, s5im2):
    nb = x.shape[0]
    full = lambda a: pl.BlockSpec(a.shape, lambda *_: (0,) * a.ndim)
    of_layer = lambda a: pl.BlockSpec((None,) + a.shape[1:], lambda *_: (layer,) + (0,) * (a.ndim - 1))
    out_shape = (
        jax.ShapeDtypeStruct((RW_W, nb), F32),
        jax.ShapeDtypeStruct((nb, MID_W), F32),
        jax.ShapeDtypeStruct(pool2.shape[1:], F32),
        jax.ShapeDtypeStruct(conv2.shape[1:], F32),
        jax.ShapeDtypeStruct((BW, nb), F32),
        jax.ShapeDtypeStruct((HEADS, nb), F32),
        jax.ShapeDtypeStruct((S5_W, nb), F32),
        jax.ShapeDtypeStruct((S5_W, nb), F32),
    )
    in_specs = [
        full(x),
        _const_spec((D_MODEL, PRE_W), layer),
        _const_spec((16, D_MODEL), layer),
        _const_spec((BW, BW), layer),
        _const_spec((LAMP_ROWS, S5_W), layer),
        _const_spec((BW, 2 * S5_W), layer),
        _const_spec((2 * S5_W, BW), layer),
        of_layer(pool2), of_layer(conv2), of_layer(n2), of_layer(m2), of_layer(s5re2), of_layer(s5im2),
    ]
    return pl.pallas_call(
        _sample_pre_kernel,
        grid=(1,),
        in_specs=in_specs,
        out_specs=tuple(pl.BlockSpec(s.shape, lambda *_, nd=len(s.shape): (0,) * nd) for s in out_shape),
        out_shape=out_shape,
        compiler_params=pltpu.CompilerParams(dimension_semantics=("arbitrary",),
                                             vmem_limit_bytes=VMEM_LIMIT),
        name="sample_pre",
    )(x, p["wpre"], p["vec"], p["wpool"], p["lamp"], p["bbd"], p["cbd"],
      pool2, conv2, n2, m2, s5re2, s5im2)


def _sample_state_kernel(first_layer, rows_ref, ssd_ref, mc_ref, *rest):
    h = pl.program_id(0)
    part = pl.program_id(1)
    if first_layer:
        yoff_ref, qc_ref, ssd_o, mc_o = rest
        ssd_o[1] = ssd_ref[1]
        mc_o[1] = mc_ref[1]
        ssd_in, mc_in, ssd_out, mc_out = ssd_ref.at[0], mc_ref.at[0], ssd_o.at[0], mc_o.at[0]
    else:
        _, _, yoff_ref, qc_ref, ssd_out, mc_out = rest
        ssd_in, mc_in = ssd_ref, mc_ref
    g64 = pl.multiple_of((h // 2) * SSD_STATE, SSD_STATE)
    h64 = pl.multiple_of(h * HD, HD)
    b_t = rows_ref[pl.ds(RW_BC + g64, SSD_STATE), :]
    c_t = rows_ref[pl.ds(RW_BC + 128 + g64, SSD_STATE), :]
    v_t = rows_ref[pl.ds(RW_V + h64, HD), :]
    sdec = rows_ref[pl.ds(RW_SDEC + h, 1), :]
    mdec = rows_ref[pl.ds(RW_MDEC + h, 1), :]
    base = h64 + part * STATE_ROWS

    def body(i, acc):
        hs = ssd_in[i]
        yoff_ref[pl.ds(i, 1), :] = jnp.sum(hs * c_t, axis=0, keepdims=True)
        ssd_out[i] = sdec * hs + rows_ref[pl.ds(RW_XDT + base + i, 1), :] * b_t
        cs = mc_in[i]
        mc_out[i] = mdec * cs + rows_ref[pl.ds(RW_WK + base + i, 1), :] * v_t
        return acc + rows_ref[pl.ds(RW_Q + base + i, 1), :] * cs

    acc = lax.fori_loop(0, STATE_ROWS, body, jnp.zeros((HD, rows_ref.shape[1]), F32), unroll=4)

    @pl.when(part == 0)
    def _():
        qc_ref[...] = acc

    @pl.when(part != 0)
    def _():
        qc_ref[...] = qc_ref[...] + acc


STATE_ROWS = 32


def _sample_state(rows_t, layer, ssd_all, mc_all, prev=None):
    nb = rows_t.shape[1]
    depth = ssd_all.shape[0]
    parts = HD // STATE_ROWS
    tile = (STATE_ROWS, HD, nb)
    rows_spec = pl.BlockSpec(rows_t.shape, lambda h, s: (0, 0))
    yoff_spec = pl.BlockSpec((STATE_ROWS, nb), lambda h, s: (h * parts + s, 0))
    qc_spec = pl.BlockSpec((HD, nb), lambda h, s: (h, 0))
    out_shape = (jax.ShapeDtypeStruct((BW, nb), F32),
                 jax.ShapeDtypeStruct((BW, nb), F32),
                 jax.ShapeDtypeStruct(ssd_all.shape, F32),
                 jax.ShapeDtypeStruct(mc_all.shape, F32))
    params = pltpu.CompilerParams(dimension_semantics=("arbitrary", "arbitrary"),
                                  vmem_limit_bytes=VMEM_LIMIT)
    if prev is None:
        assert layer == 0 and depth == 2
        both = pl.BlockSpec((depth, None) + tile, lambda h, s: (0, h, s, 0, 0))
        return pl.pallas_call(
            functools.partial(_sample_state_kernel, True),
            grid=(HEADS, parts),
            in_specs=[rows_spec, both, both],
            out_specs=(yoff_spec, qc_spec, both, both),
            out_shape=out_shape,
            compiler_params=params,
            name="sample_state_first",
        )(rows_t, ssd_all, mc_all)
    one = pl.BlockSpec((None, None) + tile, lambda h, s: (layer, h, s, 0, 0))
    anywhere = pl.BlockSpec(memory_space=pl.ANY)
    return pl.pallas_call(
        functools.partial(_sample_state_kernel, False),
        grid=(HEADS, parts),
        in_specs=[rows_spec, one, one, anywhere, anywhere],
        out_specs=(yoff_spec, qc_spec, one, one),
        out_shape=out_shape,
        input_output_aliases={3: 2, 4: 3},
        compiler_params=params,
        name="sample_state_next",
    )(rows_t, ssd_all, mc_all, prev[0], prev[1])


def _sample_post_kernel(x_ref, wpost_ref, vec_ref, mid_ref, yoff_ref, qc_ref, wglu_ref, wbr_ref, wout_ref, y_ref):
    x = x_ref[...]
    xb = x.astype(BF16)

    def post(lo, width):
        return jnp.dot(xb, wpost_ref[:, lo:lo + width], preferred_element_type=F32)

    def mid(i):
        return mid_ref[:, i * BW:(i + 1) * BW]

    y_pool, yssd_part, sdec_exp, sv, wi_exp, dmax_exp, y_s5 = [mid(i) for i in range(7)]
    y_ssd = yssd_part + yoff_ref[...].T * sdec_exp
    hh = (sv + wi_exp * qc_ref[...].T) / dmax_exp
    y_m = _sigmoid(post(O_OG, BW)) * hh
    y_s5 = _s5_glu(y_s5, wglu_ref)
    z_acts = [_silu(post(O_Z + b * BW, BW)) for b in range(4)]
    gates = [_sigmoid(post(O_GATE + b * D_MODEL, D_MODEL)) for b in range(4)]
    y_ref[...] = _merge_tail(x, z_acts, [y_pool, y_ssd, y_m, y_s5], gates, wbr_ref, wout_ref, vec_ref)


def _sample_post(x, layer, p, mid, yoff_t, qc_t):
    nb = x.shape[0]
    full = lambda a: pl.BlockSpec(a.shape, lambda *_: (0,) * a.ndim)
    return pl.pallas_call(
        _sample_post_kernel,
        grid=(1,),
        in_specs=[full(x), _const_spec((D_MODEL, POST_W), layer), _const_spec((16, D_MODEL), layer),
                  full(mid), full(yoff_t), full(qc_t), _const_spec((BW, 2 * BW), layer),
                  _const_spec((4, BW, D_MODEL), layer), _const_spec((D_MODEL, D_MODEL), layer)],
        out_specs=pl.BlockSpec((nb, D_MODEL), lambda *_: (0, 0)),
        out_shape=jax.ShapeDtypeStruct((nb, D_MODEL), F32),
        compiler_params=pltpu.CompilerParams(dimension_semantics=("arbitrary",),
                                             vmem_limit_bytes=VMEM_LIMIT),
        name="sample_post",
    )(x, p["wpost"], p["vec"], mid, yoff_t, qc_t, p["wglu"], p["wbr"], p["wout"])


def _prepare_params(w_in, w_pool, pool_scale, conv_w, conv_b, dt_bias, a_log, d_ssd, ig_bias, fg_bias,
                    lam_re, lam_im, b_re, b_im, c_re, c_im, log_dt, d_s5, w_glu, w_br, w_out, ln_g, ln_b):
    depth = w_in.shape[0]
    sizes = (256, 256, 512, 4, 256, 256, 256, 256, 4, 4, 256, 256, 256, 256, 4096)
    offs = [0]
    for s in sizes:
        offs.append(offs[-1] + s)
    col = lambda i: w_in[:, :, offs[i]:offs[i + 1]]
    small = jnp.concatenate([col(3), col(8), col(9), jnp.zeros((depth, D_MODEL, LANES - 12), w_in.dtype)], axis=2)
    wpre = jnp.concatenate([col(0), col(2), col(5), col(6), col(7), col(12), small], axis=2).astype(BF16)
    wpost = jnp.concatenate([col(1), col(4), col(11), col(13), col(10), col(14)], axis=2).astype(BF16)

    zeros = lambda n: jnp.zeros((depth, n), F32)
    misc = jnp.concatenate([pool_scale, jnp.repeat(d_ssd, HD, axis=1), d_s5,
                            dt_bias, ig_bias, fg_bias, zeros(LANES - 12),
                            a_log, zeros(LANES - HEADS)], axis=1)
    pad512 = lambda a: jnp.concatenate([a, jnp.zeros(a.shape[:-1] + (D_MODEL - CONV_DIM,), F32)], axis=-1)
    vec = jnp.concatenate([ln_g[:, None], ln_b[:, None], misc[:, None], pad512(conv_b)[:, None],
                           pad512(conv_w), jnp.zeros((depth, 16 - 4 - CONV_K, D_MODEL), F32)], axis=1)

    eye4 = jnp.eye(len(POOL_WINDOWS), dtype=F32)
    wpool = jnp.einsum("lgce,gh->lgche", w_pool, eye4).reshape(depth, BW, BW).astype(BF16)

    lam2 = jnp.stack([lam_re.reshape(depth, S5_W), lam_im.reshape(depth, S5_W)], axis=1)
    ldt = jnp.repeat(log_dt, S5_STATE, axis=1)[:, None]
    to_rows = lambda b: jnp.transpose(b, (0, 3, 1, 2)).reshape(depth, S5_CH, S5_W)
    bcat = jnp.concatenate([to_rows(b_re), to_rows(b_im)], axis=1)

    eye_g = jnp.eye(S5_GROUPS, dtype=F32)
    cb = lambda cm: jnp.einsum("lgcp,gh->lgphc", cm, eye_g).reshape(depth, S5_W, BW)
    cbd = jnp.concatenate([cb(c_re), -cb(c_im)], axis=1).astype(BF16)

    lamp, bbd = [], []
    for l in range(depth):
        lp, bb = _s5_prep(lam2[l], ldt[l], bcat[l])
        lamp.append(lp)
        bbd.append(bb)
    return dict(wpre=wpre, wpost=wpost, vec=vec, wpool=wpool, lamp=jnp.stack(lamp), bbd=jnp.stack(bbd),
                cbd=cbd, wglu=w_glu.astype(BF16), wbr=w_br.astype(BF16), wout=w_out.astype(BF16))


def kernel(x_prompt, x_sample, state_pool, state_ssd_conv, state_ssd, state_mlstm_C, state_mlstm_n, state_mlstm_m, state_s5_re, state_s5_im, w_in, w_pool, pool_scale, conv_w, conv_b, dt_bias, a_log, d_ssd, ig_bias, fg_bias, lam_re, lam_im, b_re, b_im, c_re, c_im, log_dt, d_s5, w_glu, w_br, w_out, ln_g, ln_b):
    p = _prepare_params(w_in, w_pool, pool_scale, conv_w, conv_b, dt_bias, a_log, d_ssd, ig_bias, fg_bias,
                        lam_re, lam_im, b_re, b_im, c_re, c_im, log_dt, d_s5, w_glu, w_br, w_out, ln_g, ln_b)
    depth = w_in.shape[0]
    bsz = x_prompt.shape[0]
    nb = x_sample.shape[0]
    yp = x_prompt
    ys = x_sample.reshape(nb, D_MODEL)
    outs_p = [[] for _ in range(8)]
    outs_s = [[] for _ in range(8)]
    mats_s = None
    pool_t = jnp.transpose(state_pool, (0, 2, 1, 3))
    conv_t = jnp.transpose(state_ssd_conv, (0, 2, 1, 3))
    ssd_t = jnp.transpose(state_ssd, (0, 2, 3, 4, 1))
    mc_t = jnp.transpose(state_mlstm_C, (0, 2, 3, 4, 1))
    n_t = jnp.transpose(state_mlstm_n, (0, 2, 3, 1)).reshape(depth, BW, nb)
    m_t = jnp.transpose(state_mlstm_m, (0, 2, 1))
    re_t = jnp.transpose(state_s5_re, (0, 2, 3, 1)).reshape(depth, S5_W, nb)
    im_t = jnp.transpose(state_s5_im, (0, 2, 3, 1)).reshape(depth, S5_W, nb)
    for l in range(depth):
        yp, pool_p, conv_p, ssd_p, mc_p, mn_p, mm_p, re_p, im_p = _prompt_layer(yp, l, p)
        new_p = (pool_p, conv_p, ssd_p, mc_p, mn_p.reshape(bsz, HEADS, HD), mm_p[:, 0, :HEADS],
                 re_p.reshape(bsz, S5_GROUPS, S5_STATE), im_p.reshape(bsz, S5_GROUPS, S5_STATE))

        rows_t, mid, pool_s, conv_s, n_s, m_s, re_s, im_s = _sample_pre(
            ys, l, p, pool_t, conv_t, n_t, m_t, re_t, im_t)
        yoff_t, qc_t, ssd_s, mc_s = _sample_state(rows_t, l, ssd_t, mc_t, mats_s)
        mats_s = (ssd_s, mc_s)
        ys = _sample_post(ys, l, p, mid, yoff_t, qc_t)
        new_s = (pool_s, conv_s, None, None, n_s, m_s, re_s, im_s)
        for i in range(8):
            outs_p[i].append(new_p[i])
            outs_s[i].append(new_s[i])
    sp = [jnp.stack(o, axis=0) for o in outs_p]
    st = [mats_s[i - 2] if i in (2, 3) else jnp.stack(outs_s[i], axis=0) for i in range(8)]
    ss = [jnp.transpose(st[0], (0, 2, 1, 3)), jnp.transpose(st[1], (0, 2, 1, 3)),
          jnp.transpose(st[2], (0, 4, 1, 2, 3)), jnp.transpose(st[3], (0, 4, 1, 2, 3)),
          jnp.transpose(st[4].reshape(depth, HEADS, HD, nb), (0, 3, 1, 2)),
          jnp.transpose(st[5], (0, 2, 1)),
          jnp.transpose(st[6].reshape(depth, S5_GROUPS, S5_STATE, nb), (0, 3, 1, 2)),
          jnp.transpose(st[7].reshape(depth, S5_GROUPS, S5_STATE, nb), (0, 3, 1, 2))]
    out = [yp, ys.reshape(nb, 1, D_MODEL)]
    for i in range(8):
        out.append(sp[i])
        out.append(ss[i])
    return tuple(out)
```

```python
import functools

import jax
import jax.numpy as jnp
from jax import lax
from jax.experimental import pallas as pl
from jax.experimental.pallas import tpu as pltpu

F32 = jnp.float32
BF16 = jnp.bfloat16

D_MODEL = 1024
DEPTH = 2
PAST_LEN = 16384
BW = 256
POOL_WINDOWS = (2, 4, 8, 16)
POOL_BUF = 15
POOL_PAD = 32
HEADS = 4
HD = 64
SSD_STATE = 64
CONV_K = 4
CONV_DIM = 512
S5_GROUPS = 16
S5_CH = 16
S5_STATE = 64
S5_W = S5_GROUPS * S5_STATE
CHUNK = 128
TILE = 512
ALPHA = (2.0 * DEPTH) ** 0.25
LN_EPS = 1e-5

SUBLANES = 8
LANES = 128
VMEM_LIMIT = 62 * 1024 * 1024

PRE_W = 1920
POST_W = 5376
O_PU, O_XBC, O_Q, O_K, O_V, O_SU, O_SM = 0, 256, 768, 1024, 1280, 1536, 1792
O_Z, O_OG, O_GATE = 0, 1024, 1280

R_LAM = 0
R_P = 8
R_M = 24
LAMP_ROWS = R_M + 3 * 16

V_LNG, V_LNB, V_MISC, V_CONVB, V_CONVW = 0, 1, 2, 3, 4


def _dot(a, b):
    return jnp.dot(a.astype(BF16), b.astype(BF16), preferred_element_type=F32)


def _dot_nt(a, b):
    return lax.dot_general(a.astype(BF16), b.astype(BF16), (((1,), (1,)), ((), ())),
                           preferred_element_type=F32)


def _softplus(x):
    return jnp.logaddexp(x, 0.0)


def _sigmoid(x):
    return 0.5 * jnp.tanh(0.5 * x) + 0.5


def _silu(x):
    return x * _sigmoid(x)


def _silu_of_half(xh):
    return xh * (jnp.tanh(xh) + 1.0)


def _expand_heads(cols, width=HD):
    rows = cols[0].shape[0]
    lane = lax.broadcasted_iota(jnp.int32, (rows, HEADS * width), 1)
    out = jnp.broadcast_to(cols[HEADS - 1], (rows, HEADS * width))
    for h in range(HEADS - 2, -1, -1):
        out = jnp.where(lane < (h + 1) * width, jnp.broadcast_to(cols[h], (rows, HEADS * width)), out)
    return out


def _pool_select(w2, w4, w8, w16):
    lane = lax.broadcasted_iota(jnp.int32, w2.shape, 1)
    return jnp.where(lane < 64, w2, jnp.where(lane < 128, w4, jnp.where(lane < 192, w8, w16)))


def _pool_window_row():
    lane = lax.broadcasted_iota(jnp.int32, (1, BW), 1)
    return jnp.where(lane < 64, 2, jnp.where(lane < 128, 4, jnp.where(lane < 192, 8, 16)))


def _small_block(sm, vec_ref):
    bias = vec_ref[V_MISC:V_MISC + 1, 768:896]
    alog = vec_ref[V_MISC:V_MISC + 1, 896:1024]
    v = sm + bias
    sp = _softplus(v)
    lsg = -_softplus(-v)
    a_row = -jnp.exp(alog)
    return v, sp, lsg, a_row


def _merge_tail(x, z_acts, ys, gate_tanh, wbr_ref, wout_ref, vec_ref):
    merged = None
    for b in range(4):
        br = ys[b] * z_acts[b]
        pb = _dot(br, wbr_ref[b])
        gb = gate_tanh[b][...] + 1.0
        merged = gb * pb if merged is None else merged + gb * pb
    out = _dot(merged, wout_ref[...])
    r = ALPHA * x + out
    mu = jnp.mean(r, axis=-1, keepdims=True)
    var = jnp.mean(jnp.square(r - mu), axis=-1, keepdims=True)
    g = vec_ref[V_LNG:V_LNG + 1, :]
    b = vec_ref[V_LNB:V_LNB + 1, :]
    return (r - mu) * lax.rsqrt(var + LN_EPS) * g + b


def _s5_glu(y_s5, wglu_ref):
    glu = _dot(jax.nn.gelu(y_s5), wglu_ref[...])
    return glu[:, :BW] * _sigmoid(glu[:, BW:])


def _s5_prep_kernel(lam_ref, ldt_ref, b_ref, lamp_ref, bbd_ref):
    lr = lam_ref[0:1, :]
    li = lam_ref[1:2, :]
    step = jnp.exp(ldt_ref[0:1, :])
    e = jnp.exp(lr * step)
    lbr = e * jnp.cos(li * step)
    lbi = e * jnp.sin(li * step)
    den = lr * lr + li * li
    nr = lbr - 1.0
    qr = (nr * lr + lbi * li) / den
    qi = (lbi * lr - nr * li) / den
    bre = b_ref[0:S5_CH, :]
    bim = b_ref[S5_CH:2 * S5_CH, :]
    bbr = qr * bre - qi * bim
    bbi = qr * bim + qi * bre
    for half, (vr, vi) in enumerate(((bbr, bbi), (lbr * bbr - lbi * bbi, lbr * bbi + lbi * bbr))):
        bb = jnp.concatenate([vr, vi], axis=1)
        tiled = jnp.concatenate([bb] * S5_GROUPS, axis=0)
        rowg = lax.broadcasted_iota(jnp.int32, tiled.shape, 0) // S5_CH
        colg = (lax.broadcasted_iota(jnp.int32, tiled.shape, 1) % S5_W) // S5_STATE
        bbd_ref[half * BW:(half + 1) * BW, :] = jnp.where(rowg == colg, tiled, 0.0).astype(BF16)

    pows = [(lbr, lbi)]
    for _ in range(7):
        pr, pi = pows[-1]
        pows.append((pr * lbr - pi * lbi, pr * lbi + pi * lbr))
    row = lax.broadcasted_iota(jnp.int32, (SUBLANES, S5_W), 0)

    def bcast(v):
        return jnp.broadcast_to(v, (SUBLANES, S5_W))

    p_re = bcast(pows[7][0])
    p_im = bcast(pows[7][1])
    for j in range(6, -1, -1):
        p_re = jnp.where(row == j, bcast(pows[j][0]), p_re)
        p_im = jnp.where(row == j, bcast(pows[j][1]), p_im)
    lamp_ref[R_LAM:R_LAM + SUBLANES, :] = jnp.where(row == 0, bcast(lbr), jnp.where(row == 1, bcast(lbi), 0.0))
    lamp_ref[R_P:R_P + SUBLANES, :] = p_re
    lamp_ref[R_P + SUBLANES:R_P + 2 * SUBLANES, :] = p_im
    for i, d in enumerate((1, 2, 4)):
        base = R_M + 16 * i
        lamp_ref[base:base + SUBLANES, :] = jnp.where(row >= d, bcast(pows[d - 1][0]), 0.0)
        lamp_ref[base + SUBLANES:base + 2 * SUBLANES, :] = jnp.where(row >= d, bcast(pows[d - 1][1]), 0.0)


def _s5_prep(lam2, ldt, bcat):
    return pl.pallas_call(
        _s5_prep_kernel,
        out_shape=(jax.ShapeDtypeStruct((LAMP_ROWS, S5_W), F32),
                   jax.ShapeDtypeStruct((2 * BW, 2 * S5_W), BF16)),
        name="s5_prep",
    )(lam2, ldt, bcat)


def _prompt_kernel(x_ref, wpre_ref, wpost_ref, vec_ref, wpool_ref, lamp_ref, bbd_ref, cbd_ref,
                   wglu_ref, wbr_ref, wout_ref,
                   y_ref, pool_o, conv_o, ssd_o, mc_o, mn_o, mm_o, s5re_o, s5im_o,
                   poolh, poolw, convh, ssd_st, mc_st, nrow, mrow, s5cr, s5ci, gsc):
    T = x_ref.shape[1]
    Q = CHUNK
    c = pl.program_id(1)
    last = pl.num_programs(1) - 1

    @pl.when(c == 0)
    def _():
        poolh[0:POOL_PAD, :] = jnp.zeros((POOL_PAD, BW), F32)
        convh[0:SUBLANES, :] = jnp.zeros((SUBLANES, CONV_DIM), F32)
        ssd_st[...] = jnp.zeros(ssd_st.shape, F32)
        mc_st[...] = jnp.zeros(mc_st.shape, F32)
        nrow[...] = jnp.zeros(nrow.shape, F32)
        mrow[...] = jnp.zeros(mrow.shape, F32)
        s5cr[...] = jnp.zeros(s5cr.shape, F32)
        s5ci[...] = jnp.zeros(s5ci.shape, F32)

    x = x_ref[0]
    xb = x.astype(BF16)

    def pre(lo, width):
        return jnp.dot(xb, wpre_ref[:, lo:lo + width], preferred_element_type=F32)

    def post(lo, width):
        return jnp.dot(xb, wpost_ref[:, lo:lo + width], preferred_element_type=F32)

    rows_i = lax.broadcasted_iota(jnp.int32, (Q, Q), 0)
    cols_i = lax.broadcasted_iota(jnp.int32, (Q, Q), 1)
    causal = rows_i >= cols_i

    n_rows = POOL_PAD + T

    def pool_mix(pu):
        poolh[POOL_PAD:n_rows, :] = pu
        poolw[0, 8:n_rows, :] = poolh[8:n_rows, :] + poolh[7:n_rows - 1, :]
        poolw[1, 16:n_rows, :] = poolw[0, 16:n_rows, :] + poolw[0, 14:n_rows - 2, :]
        poolw[2, 24:n_rows, :] = poolw[1, 24:n_rows, :] + poolw[1, 20:n_rows - 4, :]
        w2 = poolw[0, POOL_PAD:n_rows, :]
        w4 = poolw[1, POOL_PAD:n_rows, :]
        w8 = poolw[2, POOL_PAD:n_rows, :]
        w16 = w8 + poolw[2, POOL_PAD - 8:n_rows - 8, :]
        pos = c * T + lax.broadcasted_iota(jnp.int32, (T, BW), 0)
        cnt = jnp.minimum(pos + 1, _pool_window_row()).astype(F32)
        dpool = _pool_select(w2, w4, w8, w16) / cnt - pu
        poolh[0:POOL_PAD, :] = poolh[T:n_rows, :]
        return _dot(dpool, wpool_ref[...]) * vec_ref[V_MISC:V_MISC + 1, 0:256]

    def conv_mix(xbc_raw):
        convh[SUBLANES:SUBLANES + T, :] = xbc_raw
        acc = vec_ref[V_CONVB:V_CONVB + 1, 0:CONV_DIM]
        for kk in range(CONV_K):
            acc = acc + convh[5 + kk:5 + kk + T, :] * vec_ref[V_CONVW + kk:V_CONVW + kk + 1, 0:CONV_DIM]
        convh[0:SUBLANES, :] = convh[T:T + SUBLANES, :]
        return _silu(acc)

    v_sm, sp_all, lsg, a_row = _small_block(pre(O_SM, LANES), vec_ref)
    lane_s = lax.broadcasted_iota(jnp.int32, (T, LANES), 1)
    g_all = jnp.where(lane_s < 4, sp_all * a_row, jnp.where(lane_s < 8, v_sm, jnp.where(lane_s < 12, lsg, 0.0)))
    tril = causal.astype(BF16)
    g_hi = g_all.astype(BF16)
    g_r1 = g_all - g_hi.astype(F32)
    g_mid = g_r1.astype(BF16)
    g_lo = (g_r1 - g_mid.astype(F32)).astype(BF16)
    cums = []
    for s in range(T // Q):
        rs = slice(s * Q, (s + 1) * Q)
        cum = (jnp.dot(tril, g_hi[rs], preferred_element_type=F32)
               + jnp.dot(tril, g_mid[rs], preferred_element_type=F32)
               + jnp.dot(tril, g_lo[rs], preferred_element_type=F32))
        cums.append((cum, cum.T, g_all[rs].T))

    proj = {}
    fillers = [
        lambda: proj.update(pu=pre(O_PU, BW), xbc=pre(O_XBC, CONV_DIM)),
        lambda: proj.update(q=pre(O_Q, BW), k=pre(O_K, BW) * (HD ** -0.5)),
        lambda: proj.update(v=pre(O_V, BW), zs=[post(O_Z + b * BW, BW) for b in range(4)]),
        lambda: proj.update(og=post(O_OG, BW)),
    ]
    y_s5 = _s5_chunk_scan(pre(O_SU, BW), fillers, lamp_ref, bbd_ref, cbd_ref, vec_ref, s5cr, s5ci)
    zs = proj["zs"]

    dyn0 = jnp.minimum(c, 0)
    gate_raw = post(O_GATE, D_MODEL)
    y_pool = pool_mix(proj["pu"])
    gsc[dyn0] = jnp.tanh(gate_raw)
    gate_raw = post(O_GATE + D_MODEL, D_MODEL)
    xbc = conv_mix(proj["xbc"])
    gsc[dyn0 + 1] = jnp.tanh(gate_raw)
    gate_raw = post(O_GATE + 2 * D_MODEL, D_MODEL)
    gsc[dyn0 + 2] = jnp.tanh(gate_raw)
    gate_raw = post(O_GATE + 3 * D_MODEL, D_MODEL)
    gsc[dyn0 + 3] = jnp.tanh(gate_raw)
    gates = [gsc.at[dyn0 + b] for b in range(4)]

    y_ssd_parts, h_m_parts = [], []
    for s in range(T // Q):
        y_c, h_c = _ssd_mlstm_chunk(
            xbc[s * Q:(s + 1) * Q], proj["q"][s * Q:(s + 1) * Q], proj["k"][s * Q:(s + 1) * Q],
            proj["v"][s * Q:(s + 1) * Q], sp_all[s * Q:(s + 1) * Q], g_all[s * Q:(s + 1) * Q],
            cums[s], causal, vec_ref, ssd_st, mc_st, nrow, mrow)
        y_ssd_parts.append(y_c)
        h_m_parts.append(h_c)
    y_ssd = jnp.concatenate(y_ssd_parts, axis=0)
    y_m = _sigmoid(proj["og"]) * jnp.concatenate(h_m_parts, axis=0)
    y_s5 = _s5_glu(y_s5, wglu_ref)

    z_acts = [_silu_of_half(z) for z in zs]
    y_ref[0] = _merge_tail(x, z_acts, [y_pool, y_ssd, y_m, y_s5], gates, wbr_ref, wout_ref, vec_ref)

    @pl.when(c == last)
    def _():
        pool_o[0] = poolh[POOL_PAD - POOL_BUF:POOL_PAD, :]
        conv_o[0] = convh[5:8, :]
        ssd_o[0] = ssd_st[...]
        mc_o[0] = mc_st[...]
        mn_o[0] = nrow[0:1, :]
        mm_o[0] = mrow[0:1, :]
        s5re_o[0] = s5cr[0:1, :]
        s5im_o[0] = s5ci[0:1, :]


def _ssd_mlstm_chunk(xbc, q, k, v, sp, g_blk, cums, causal, vec_ref, ssd_st, mc_st, nrow, mrow):
    T = xbc.shape[0]
    cum, cum_t, g_t = cums
    xs = xbc[:, 0:BW]
    dt_cols = [sp[:, h:h + 1] for h in range(HEADS)]
    acs_cols = [cum[:, h:h + 1] for h in range(HEADS)]
    acs_last = [cum[T - 1:T, h:h + 1] for h in range(HEADS)]
    xdt = xs * _expand_heads(dt_cols)
    xw_t = (xdt * _expand_heads([jnp.exp(acs_last[h] - acs_cols[h]) for h in range(HEADS)])).T
    d_ssd = vec_ref[V_MISC:V_MISC + 1, 256:512]
    m_row = mrow[0:1, :]
    n_row = nrow[0:1, :]
    hsl = [slice(h * HD, (h + 1) * HD) for h in range(HEADS)]
    b_gs = [xbc[:, BW + g * SSD_STATE:BW + (g + 1) * SSD_STATE] for g in range(2)]
    c_gs = [xbc[:, BW + 128 + g * SSD_STATE:BW + 128 + (g + 1) * SSD_STATE] for g in range(2)]

    cbs = [_dot_nt(c_gs[g], b_gs[g]) for g in range(2)]
    offs = [_dot_nt(c_gs[g], ssd_st[2 * g:2 * g + 2].reshape(2 * HD, SSD_STATE)) for g in range(2)]
    qks = [_dot_nt(q[:, hsl[h]], k[:, hsl[h]]) for h in range(HEADS)]
    qcs = [_dot(q[:, hsl[h]], mc_st[h]) for h in range(HEADS)]
    gms = [cbs[h // 2] * jnp.exp(jnp.where(causal, acs_cols[h] - cum_t[h:h + 1, :], -jnp.inf))
           for h in range(HEADS)]
    b_cols = [cum[:, 8 + h:9 + h] for h in range(HEADS)]
    m_prevs = [m_row[:, h:h + 1] for h in range(HEADS)]
    dlogs = [jnp.where(causal, b_cols[h] - cum_t[8 + h:9 + h, :] + g_t[4 + h:5 + h, :], -jnp.inf)
             for h in range(HEADS)]
    inters = [b_cols[h] + m_prevs[h] for h in range(HEADS)]
    m_ts = [jnp.maximum(inters[h], jnp.max(dlogs[h], axis=1, keepdims=True)) for h in range(HEADS)]
    ss = [qks[h] * jnp.exp(dlogs[h] - m_ts[h]) for h in range(HEADS)]
    wis = [jnp.exp(inters[h] - m_ts[h]) for h in range(HEADS)]
    y_diags = [_dot(gms[h], xdt[:, hsl[h]]) for h in range(HEADS)]
    svs = [_dot(ss[h], v[:, hsl[h]]) for h in range(HEADS)]
    y_off = jnp.concatenate(offs, axis=1) * _expand_heads([jnp.exp(acs_cols[h]) for h in range(HEADS)])
    y_ssd = jnp.concatenate(y_diags, axis=1) + y_off + d_ssd * xs
    h_heads, wk_parts, decays, m_news = [], [], [], []
    for h in range(HEADS):
        num = svs[h] + wis[h] * qcs[h]
        den = (jnp.sum(ss[h], axis=1, keepdims=True)
               + wis[h] * jnp.sum(q[:, hsl[h]] * n_row[:, hsl[h]], axis=1, keepdims=True))
        h_heads.append(num / jnp.maximum(jnp.abs(den), jnp.exp(-m_ts[h])))
        m_new = m_ts[h][T - 1:T, :]
        b_last = b_cols[h][T - 1:T, :]
        wk_parts.append(jnp.exp(b_last - b_cols[h] + g_blk[:, 4 + h:5 + h] - m_new))
        decays.append(jnp.exp(b_last + m_prevs[h] - m_new))
        m_news.append(m_new)
    for h in range(HEADS):
        ssd_st[h] = jnp.exp(acs_last[h]) * ssd_st[h] + _dot(xw_t[hsl[h], :], b_gs[h // 2])
    lane_m = lax.broadcasted_iota(jnp.int32, (1, LANES), 1)
    m_row_new = jnp.zeros((1, LANES), F32)
    for h in range(HEADS):
        m_row_new = jnp.where(lane_m == h, jnp.broadcast_to(m_news[h], (1, LANES)), m_row_new)
    mrow[0:1, :] = m_row_new
    wk = k * _expand_heads(wk_parts)
    wk_t = wk.T
    for h in range(HEADS):
        mc_st[h] = decays[h] * mc_st[h] + _dot(wk_t[h * HD:(h + 1) * HD, :], v[:, h * HD:(h + 1) * HD])
    nrow[0:1, :] = _expand_heads(decays) * n_row + jnp.sum(wk, axis=0, keepdims=True)
    return y_ssd, jnp.concatenate(h_heads, axis=1)


def _s5_chunk_scan(su, fillers, lamp_ref, bbd_ref, cbd_ref, vec_ref, s5cr, s5ci):
    T = su.shape[0]
    nblk = T // SUBLANES
    su3 = su.reshape(nblk, SUBLANES, BW)
    in_block = lax.broadcasted_iota(jnp.int32, (SUBLANES, BW), 0) >= 1
    su_prev = jnp.where(in_block, pltpu.roll(su3, 1, 1), 0.0).reshape(T, BW)
    bu = _dot(jnp.concatenate([su, su_prev], axis=1), bbd_ref[...])
    fillers[0]()
    h_re = bu[:, 0:S5_W].reshape(nblk, SUBLANES, S5_W)
    h_im = bu[:, S5_W:2 * S5_W].reshape(nblk, SUBLANES, S5_W)
    for i, d in ((1, 2), (2, 4)):
        base = R_M + 16 * i
        m_re = lamp_ref[base:base + SUBLANES, :]
        m_im = lamp_ref[base + SUBLANES:base + 2 * SUBLANES, :]
        r_re = pltpu.roll(h_re, d, 1)
        r_im = pltpu.roll(h_im, d, 1)
        h_re, h_im = h_re + (m_re * r_re - m_im * r_im), h_im + (m_re * r_im + m_im * r_re)
        fillers[i]()
    p_re = lamp_ref[R_P:R_P + SUBLANES, :]
    p_im = lamp_ref[R_P + SUBLANES:R_P + 2 * SUBLANES, :]
    cr = s5cr[0:1, :]
    ci = s5ci[0:1, :]
    blk_re = []
    blk_im = []
    for j in range(nblk):
        br = h_re[j] + (p_re * cr - p_im * ci)
        bi = h_im[j] + (p_re * ci + p_im * cr)
        blk_re.append(br)
        blk_im.append(bi)
        cr = br[SUBLANES - 1:SUBLANES, :]
        ci = bi[SUBLANES - 1:SUBLANES, :]
    s5cr[0:1, :] = cr
    s5ci[0:1, :] = ci
    fillers[3]()
    h_all = jnp.concatenate([jnp.concatenate(blk_re, axis=0), jnp.concatenate(blk_im, axis=0)], axis=1)
    return _dot(h_all, cbd_ref[...]) + vec_ref[V_MISC:V_MISC + 1, 512:768] * su


def _const_spec(shape, layer):
    nd = len(shape)
    return pl.BlockSpec((None,) + tuple(shape), lambda *_: (layer,) + (0,) * nd,
                        pipeline_mode=pl.Buffered(1))


def _prompt_layer(x, layer, p):
    bsz, seq, _ = x.shape
    T = TILE if seq % TILE == 0 else CHUNK
    nc = seq // T

    def bmap(nd):
        return lambda b, c: (b,) + (0,) * nd

    out_shape = (
        jax.ShapeDtypeStruct((bsz, seq, D_MODEL), F32),
        jax.ShapeDtypeStruct((bsz, POOL_BUF, BW), F32),
        jax.ShapeDtypeStruct((bsz, CONV_K - 1, CONV_DIM), F32),
        jax.ShapeDtypeStruct((bsz, HEADS, HD, SSD_STATE), F32),
        jax.ShapeDtypeStruct((bsz, HEADS, HD, HD), F32),
        jax.ShapeDtypeStruct((bsz, 1, BW), F32),
        jax.ShapeDtypeStruct((bsz, 1, LANES), F32),
        jax.ShapeDtypeStruct((bsz, 1, S5_W), F32),
        jax.ShapeDtypeStruct((bsz, 1, S5_W), F32),
    )
    out_specs = (
        pl.BlockSpec((1, T, D_MODEL), lambda b, c: (b, c, 0)),
        pl.BlockSpec((1, POOL_BUF, BW), bmap(2)),
        pl.BlockSpec((1, CONV_K - 1, CONV_DIM), bmap(2)),
        pl.BlockSpec((1, HEADS, HD, SSD_STATE), bmap(3)),
        pl.BlockSpec((1, HEADS, HD, HD), bmap(3)),
        pl.BlockSpec((1, 1, BW), bmap(2)),
        pl.BlockSpec((1, 1, LANES), bmap(2)),
        pl.BlockSpec((1, 1, S5_W), bmap(2)),
        pl.BlockSpec((1, 1, S5_W), bmap(2)),
    )
    in_specs = [
        pl.BlockSpec((1, T, D_MODEL), lambda b, c: (b, c, 0)),
        _const_spec((D_MODEL, PRE_W), layer),
        _const_spec((D_MODEL, POST_W), layer),
        _const_spec((16, D_MODEL), layer),
        _const_spec((BW, BW), layer),
        _const_spec((LAMP_ROWS, S5_W), layer),
        _const_spec((2 * BW, 2 * S5_W), layer),
        _const_spec((2 * S5_W, BW), layer),
        _const_spec((BW, 2 * BW), layer),
        _const_spec((4, BW, D_MODEL), layer),
        _const_spec((D_MODEL, D_MODEL), layer),
    ]
    scratch = [
        pltpu.VMEM((T + POOL_PAD, BW), F32),
        pltpu.VMEM((3, T + POOL_PAD, BW), F32),
        pltpu.VMEM((T + SUBLANES, CONV_DIM), F32),
        pltpu.VMEM((HEADS, HD, SSD_STATE), F32),
        pltpu.VMEM((HEADS, HD, HD), F32),
        pltpu.VMEM((SUBLANES, BW), F32),
        pltpu.VMEM((SUBLANES, LANES), F32),
        pltpu.VMEM((SUBLANES, S5_W), F32),
        pltpu.VMEM((SUBLANES, S5_W), F32),
        pltpu.VMEM((4, T, D_MODEL), F32),
    ]
    return pl.pallas_call(
        _prompt_kernel,
        grid=(bsz, nc),
        in_specs=in_specs,
        out_specs=out_specs,
        out_shape=out_shape,
        scratch_shapes=scratch,
        compiler_params=pltpu.CompilerParams(dimension_semantics=("arbitrary", "arbitrary"),
                                             vmem_limit_bytes=VMEM_LIMIT),
        name="prompt_layer",
    )(x, p["wpre"], p["wpost"], p["vec"], p["wpool"], p["lamp"], p["bbd"], p["cbd"],
      p["wglu"], p["wbr"], p["wout"])


RW_BC, RW_XDT, RW_Q, RW_WK, RW_V, RW_SDEC, RW_MDEC, RW_W = 0, 256, 512, 768, 1024, 1280, 1408, 1536
MID_W = 7 * BW


def _sample_pre_kernel(x_ref, wpre_ref, vec_ref, wpool_ref, lamp_ref, bbd_ref, cbd_ref,
                       pool_ref, conv_ref, n_ref, m_ref, s5re_ref, s5im_ref,
                       rows_o, mid_o, pool_o, conv_o, n_o, m_o, s5re_o, s5im_o):
    x = x_ref[...]
    nb = x.shape[0]
    pp = _dot(x, wpre_ref[...])
    pu = pp[:, O_PU:O_PU + BW]
    xbc_raw = pp[:, O_XBC:O_XBC + CONV_DIM]
    q = pp[:, O_Q:O_Q + BW]
    k = pp[:, O_K:O_K + BW] * (HD ** -0.5)
    v = pp[:, O_V:O_V + BW]
    su = pp[:, O_SU:O_SU + BW]

    def ext(j):
        return pool_ref[j]

    w2 = pu + ext(14)
    w4 = w2 + ext(13) + ext(12)
    w8 = w4
    for j in range(11, 7, -1):
        w8 = w8 + ext(j)
    w16 = w8
    for j in range(7, -1, -1):
        w16 = w16 + ext(j)
    cnt = jnp.minimum(PAST_LEN + 1, _pool_window_row()).astype(F32)
    dpool = _pool_select(w2, w4, w8, w16) / cnt - pu
    y_pool = _dot(dpool, wpool_ref[...]) * vec_ref[V_MISC:V_MISC + 1, 0:256]
    for j in range(POOL_BUF - 1):
        pool_o[j] = pool_ref[j + 1]
    pool_o[POOL_BUF - 1] = pu

    acc = vec_ref[V_CONVB:V_CONVB + 1, 0:CONV_DIM]
    for kk in range(CONV_K - 1):
        acc = acc + conv_ref[kk] * vec_ref[V_CONVW + kk:V_CONVW + kk + 1, 0:CONV_DIM]
    acc = acc + xbc_raw * vec_ref[V_CONVW + CONV_K - 1:V_CONVW + CONV_K, 0:CONV_DIM]
    for kk in range(CONV_K - 2):
        conv_o[kk] = conv_ref[kk + 1]
    conv_o[CONV_K - 2] = xbc_raw
    xbc = _silu(acc)
    xs = xbc[:, 0:BW]
    bc = xbc[:, BW:2 * BW]

    v_sm, sp, lsg, a_row = _small_block(pp[:, O_SM:O_SM + LANES], vec_ref)
    sdec_blk = jnp.exp(sp * a_row)
    dt_cols = [sp[:, h:h + 1] for h in range(HEADS)]
    xdt = xs * _expand_heads(dt_cols)
    cb_cols = []
    for g in range(2):
        b_g = bc[:, g * SSD_STATE:(g + 1) * SSD_STATE]
        c_g = bc[:, 128 + g * SSD_STATE:128 + (g + 1) * SSD_STATE]
        cb_cols.append(jnp.sum(c_g * b_g, axis=1, keepdims=True))
    yssd_part = (_expand_heads([cb_cols[h // 2] for h in range(HEADS)]) * xdt
                 + vec_ref[V_MISC:V_MISC + 1, 256:512] * xs)
    sdec_exp = _expand_heads([sdec_blk[:, h:h + 1] for h in range(HEADS)])

    n0 = n_ref[...].T
    m0 = jnp.concatenate([m_ref[...], jnp.zeros((SUBLANES - HEADS, nb), F32)], axis=0).T
    s_cols, wi_cols, dw_cols, dmax_cols, m_cols = [], [], [], [], []
    for h in range(HEADS):
        sl = slice(h * HD, (h + 1) * HD)
        ig = v_sm[:, 4 + h:5 + h]
        lf = lsg[:, 8 + h:9 + h]
        inter = lf + m0[:, h:h + 1]
        m_t = jnp.maximum(inter, (lf - lf) + ig)
        dw = jnp.exp((lf - lf) + ig - m_t)
        wi = jnp.exp(inter - m_t)
        s = jnp.sum(q[:, sl] * k[:, sl], axis=1, keepdims=True) * dw
        den = s + wi * jnp.sum(q[:, sl] * n0[:, sl], axis=1, keepdims=True)
        s_cols.append(s)
        wi_cols.append(wi)
        dw_cols.append(dw)
        dmax_cols.append(jnp.maximum(jnp.abs(den), jnp.exp(-m_t)))
        m_cols.append(m_t)
    wi_exp = _expand_heads(wi_cols)
    wk = _expand_heads(dw_cols) * k
    n_o[...] = (wi_exp * n0 + wk).T
    lane = lax.broadcasted_iota(jnp.int32, (nb, LANES), 1)
    m_blk = jnp.zeros((nb, LANES), F32)
    wi_blk = jnp.zeros((nb, LANES), F32)
    for h in range(HEADS):
        m_blk = jnp.where(lane == h, jnp.broadcast_to(m_cols[h], (nb, LANES)), m_blk)
        wi_blk = jnp.where(lane == h, jnp.broadcast_to(wi_cols[h], (nb, LANES)), wi_blk)
    m_o[...] = m_blk.T[0:HEADS, :]

    bu = _dot(su, bbd_ref[0:BW, :])
    lbr = lamp_ref[R_LAM:R_LAM + 1, :]
    lbi = lamp_ref[R_LAM + 1:R_LAM + 2, :]
    h0r = s5re_ref[...].T
    h0i = s5im_ref[...].T
    h_re = bu[:, 0:S5_W] + (lbr * h0r - lbi * h0i)
    h_im = bu[:, S5_W:2 * S5_W] + (lbr * h0i + lbi * h0r)
    s5re_o[...] = h_re.T
    s5im_o[...] = h_im.T
    y_s5 = _dot(jnp.concatenate([h_re, h_im], axis=1), cbd_ref[...]) + vec_ref[V_MISC:V_MISC + 1, 512:768] * su

    rows_o[...] = jnp.concatenate([bc, xdt, q, wk, v, sdec_blk, wi_blk], axis=1).T
    mid_o[...] = jnp.concatenate([y_pool, yssd_part, sdec_exp, _expand_heads(s_cols) * v, wi_exp,
                                  _expand_heads(dmax_cols), y_s5], axis=1)


def _sample_pre(x, layer, p, pool2, conv2, n2, m2, s5re2, s5im2):
    nb = x.shape[0]
    full = lambda a: pl.BlockSpec(a.shape, lambda *_: (0,) * a.ndim)
    of_layer = lambda a: pl.BlockSpec((None,) + a.shape[1:], lambda *_: (layer,) + (0,) * (a.ndim - 1))
    out_shape = (
        jax.ShapeDtypeStruct((RW_W, nb), F32),
        jax.ShapeDtypeStruct((nb, MID_W), F32),
        jax.ShapeDtypeStruct(pool2.shape[1:], F32),
        jax.ShapeDtypeStruct(conv2.shape[1:], F32),
        jax.ShapeDtypeStruct((BW, nb), F32),
        jax.ShapeDtypeStruct((HEADS, nb), F32),
        jax.ShapeDtypeStruct((S5_W, nb), F32),
        jax.ShapeDtypeStruct((S5_W, nb), F32),
    )
    in_specs = [
        full(x),
        _const_spec((D_MODEL, PRE_W), layer),
        _const_spec((16, D_MODEL), layer),
        _const_spec((BW, BW), layer),
        _const_spec((LAMP_ROWS, S5_W), layer),
        _const_spec((2 * BW, 2 * S5_W), layer),
        _const_spec((2 * S5_W, BW), layer),
        of_layer(pool2), of_layer(conv2), of_layer(n2), of_layer(m2), of_layer(s5re2), of_layer(s5im2),
    ]
    return pl.pallas_call(
        _sample_pre_kernel,
        grid=(1,),
        in_specs=in_specs,
        out_specs=tuple(pl.BlockSpec(s.shape, lambda *_, nd=len(s.shape): (0,) * nd) for s in out_shape),
        out_shape=out_shape,
        compiler_params=pltpu.CompilerParams(dimension_semantics=("arbitrary",),
                                             vmem_limit_bytes=VMEM_LIMIT),
        name="sample_pre",
    )(x, p["wpre"], p["vec"], p["wpool"], p["lamp"], p["bbd"], p["cbd"],
      pool2, conv2, n2, m2, s5re2, s5im2)


def _sample_state_kernel(first_layer, rows_ref, ssd_ref, mc_ref, *rest):
    h = pl.program_id(0)
    part = pl.program_id(1)
    if first_layer:
        yoff_ref, qc_ref, ssd_o, mc_o = rest
        ssd_o[1] = ssd_ref[1]
        mc_o[1] = mc_ref[1]
        ssd_in, mc_in, ssd_out, mc_out = ssd_ref.at[0], mc_ref.at[0], ssd_o.at[0], mc_o.at[0]
    else:
        _, _, yoff_ref, qc_ref, ssd_out, mc_out = rest
        ssd_in, mc_in = ssd_ref, mc_ref
    g64 = pl.multiple_of((h // 2) * SSD_STATE, SSD_STATE)
    h64 = pl.multiple_of(h * HD, HD)
    b_t = rows_ref[pl.ds(RW_BC + g64, SSD_STATE), :]
    c_t = rows_ref[pl.ds(RW_BC + 128 + g64, SSD_STATE), :]
    v_t = rows_ref[pl.ds(RW_V + h64, HD), :]
    sdec = rows_ref[pl.ds(RW_SDEC + h, 1), :]
    mdec = rows_ref[pl.ds(RW_MDEC + h, 1), :]
    base = h64 + part * STATE_ROWS

    def body(i, acc):
        hs = ssd_in[i]
        yoff_ref[pl.ds(i, 1), :] = jnp.sum(hs * c_t, axis=0, keepdims=True)
        ssd_out[i] = sdec * hs + rows_ref[pl.ds(RW_XDT + base + i, 1), :] * b_t
        cs = mc_in[i]
        mc_out[i] = mdec * cs + rows_ref[pl.ds(RW_WK + base + i, 1), :] * v_t
        return acc + rows_ref[pl.ds(RW_Q + base + i, 1), :] * cs

    acc = lax.fori_loop(0, STATE_ROWS, body, jnp.zeros((HD, rows_ref.shape[1]), F32), unroll=4)

    @pl.when(part == 0)
    def _():
        qc_ref[...] = acc

    @pl.when(part != 0)
    def _():
        qc_ref[...] = qc_ref[...] + acc


STATE_ROWS = 32


def _sample_state(rows_t, layer, ssd_all, mc_all, prev=None):
    nb = rows_t.shape[1]
    depth = ssd_all.shape[0]
    parts = HD // STATE_ROWS
    tile = (STATE_ROWS, HD, nb)
    rows_spec = pl.BlockSpec(rows_t.shape, lambda h, s: (0, 0))
    yoff_spec = pl.BlockSpec((STATE_ROWS, nb), lambda h, s: (h * parts + s, 0))
    qc_spec = pl.BlockSpec((HD, nb), lambda h, s: (h, 0))
    out_shape = (jax.ShapeDtypeStruct((BW, nb), F32),
                 jax.ShapeDtypeStruct((BW, nb), F32),
                 jax.ShapeDtypeStruct(ssd_all.shape, F32),
                 jax.ShapeDtypeStruct(mc_all.shape, F32))
    params = pltpu.CompilerParams(dimension_semantics=("arbitrary", "arbitrary"),
                                  vmem_limit_bytes=VMEM_LIMIT)
    if prev is None:
        assert layer == 0 and depth == 2
        both = pl.BlockSpec((depth, None) + tile, lambda h, s: (0, h, s, 0, 0))
        return pl.pallas_call(
            functools.partial(_sample_state_kernel, True),
            grid=(HEADS, parts),
            in_specs=[rows_spec, both, both],
            out_specs=(yoff_spec, qc_spec, both, both),
            out_shape=out_shape,
            compiler_params=params,
            name="sample_state_first",
        )(rows_t, ssd_all, mc_all)
    one = pl.BlockSpec((None, None) + tile, lambda h, s: (layer, h, s, 0, 0))
    anywhere = pl.BlockSpec(memory_space=pl.ANY)
    return pl.pallas_call(
        functools.partial(_sample_state_kernel, False),
        grid=(HEADS, parts),
        in_specs=[rows_spec, one, one, anywhere, anywhere],
        out_specs=(yoff_spec, qc_spec, one, one),
        out_shape=out_shape,
        input_output_aliases={3: 2, 4: 3},
        compiler_params=params,
        name="sample_state_next",
    )(rows_t, ssd_all, mc_all, prev[0], prev[1])


def _sample_post_kernel(x_ref, wpost_ref, vec_ref, mid_ref, yoff_ref, qc_ref, wglu_ref, wbr_ref, wout_ref, y_ref):
    x = x_ref[...]
    xb = x.astype(BF16)

    def post(lo, width):
        return jnp.dot(xb, wpost_ref[:, lo:lo + width], preferred_element_type=F32)

    def mid(i):
        return mid_ref[:, i * BW:(i + 1) * BW]

    y_pool, yssd_part, sdec_exp, sv, wi_exp, dmax_exp, y_s5 = [mid(i) for i in range(7)]
    y_ssd = yssd_part + yoff_ref[...].T * sdec_exp
    hh = (sv + wi_exp * qc_ref[...].T) / dmax_exp
    y_m = _sigmoid(post(O_OG, BW)) * hh
    y_s5 = _s5_glu(y_s5, wglu_ref)
    z_acts = [_silu_of_half(post(O_Z + b * BW, BW)) for b in range(4)]
    gates = [jnp.tanh(post(O_GATE + b * D_MODEL, D_MODEL)) for b in range(4)]
    y_ref[...] = _merge_tail(x, z_acts, [y_pool, y_ssd, y_m, y_s5], gates, wbr_ref, wout_ref, vec_ref)


def _sample_post(x, layer, p, mid, yoff_t, qc_t):
    nb = x.shape[0]
    full = lambda a: pl.BlockSpec(a.shape, lambda *_: (0,) * a.ndim)
    return pl.pallas_call(
        _sample_post_kernel,
        grid=(1,),
        in_specs=[full(x), _const_spec((D_MODEL, POST_W), layer), _const_spec((16, D_MODEL), layer),
                  full(mid), full(yoff_t), full(qc_t), _const_spec((BW, 2 * BW), layer),
                  _const_spec((4, BW, D_MODEL), layer), _const_spec((D_MODEL, D_MODEL), layer)],
        out_specs=pl.BlockSpec((nb, D_MODEL), lambda *_: (0, 0)),
        out_shape=jax.ShapeDtypeStruct((nb, D_MODEL), F32),
        compiler_params=pltpu.CompilerParams(dimension_semantics=("arbitrary",),
                                             vmem_limit_bytes=VMEM_LIMIT),
        name="sample_post",
    )(x, p["wpost"], p["vec"], mid, yoff_t, qc_t, p["wglu"], p["wbr"], p["wout"])


def _prepare_params(w_in, w_pool, pool_scale, conv_w, conv_b, dt_bias, a_log, d_ssd, ig_bias, fg_bias,
                    lam_re, lam_im, b_re, b_im, c_re, c_im, log_dt, d_s5, w_glu, w_br, w_out, ln_g, ln_b):
    depth = w_in.shape[0]
    sizes = (256, 256, 512, 4, 256, 256, 256, 256, 4, 4, 256, 256, 256, 256, 4096)
    offs = [0]
    for s in sizes:
        offs.append(offs[-1] + s)
    col = lambda i: w_in[:, :, offs[i]:offs[i + 1]]
    small = jnp.concatenate([col(3), col(8), col(9), jnp.zeros((depth, D_MODEL, LANES - 12), w_in.dtype)], axis=2)
    wpre = jnp.concatenate([col(0), col(2), col(5), col(6), col(7), col(12), small], axis=2).astype(BF16)
    wpost = jnp.concatenate([0.5 * col(1), 0.5 * col(4), 0.5 * col(11), 0.5 * col(13), col(10), 0.5 * col(14)],
                            axis=2).astype(BF16)

    zeros = lambda n: jnp.zeros((depth, n), F32)
    misc = jnp.concatenate([pool_scale, jnp.repeat(d_ssd, HD, axis=1), d_s5,
                            dt_bias, ig_bias, fg_bias, zeros(LANES - 12),
                            a_log, zeros(LANES - HEADS)], axis=1)
    pad512 = lambda a: jnp.concatenate([a, jnp.zeros(a.shape[:-1] + (D_MODEL - CONV_DIM,), F32)], axis=-1)
    vec = jnp.concatenate([ln_g[:, None], ln_b[:, None], misc[:, None], pad512(conv_b)[:, None],
                           pad512(conv_w), jnp.zeros((depth, 16 - 4 - CONV_K, D_MODEL), F32)], axis=1)

    eye4 = jnp.eye(len(POOL_WINDOWS), dtype=F32)
    wpool = jnp.einsum("lgce,gh->lgche", w_pool, eye4).reshape(depth, BW, BW).astype(BF16)

    lam2 = jnp.stack([lam_re.reshape(depth, S5_W), lam_im.reshape(depth, S5_W)], axis=1)
    ldt = jnp.repeat(log_dt, S5_STATE, axis=1)[:, None]
    to_rows = lambda b: jnp.transpose(b, (0, 3, 1, 2)).reshape(depth, S5_CH, S5_W)
    bcat = jnp.concatenate([to_rows(b_re), to_rows(b_im)], axis=1)

    eye_g = jnp.eye(S5_GROUPS, dtype=F32)
    cb = lambda cm: jnp.einsum("lgcp,gh->lgphc", cm, eye_g).reshape(depth, S5_W, BW)
    cbd = jnp.concatenate([cb(c_re), -cb(c_im)], axis=1).astype(BF16)

    lamp, bbd = [], []
    for l in range(depth):
        lp, bb = _s5_prep(lam2[l], ldt[l], bcat[l])
        lamp.append(lp)
        bbd.append(bb)
    return dict(wpre=wpre, wpost=wpost, vec=vec, wpool=wpool, lamp=jnp.stack(lamp), bbd=jnp.stack(bbd),
                cbd=cbd, wglu=w_glu.astype(BF16), wbr=w_br.astype(BF16), wout=(0.5 * w_out).astype(BF16))


def kernel(x_prompt, x_sample, state_pool, state_ssd_conv, state_ssd, state_mlstm_C, state_mlstm_n, state_mlstm_m, state_s5_re, state_s5_im, w_in, w_pool, pool_scale, conv_w, conv_b, dt_bias, a_log, d_ssd, ig_bias, fg_bias, lam_re, lam_im, b_re, b_im, c_re, c_im, log_dt, d_s5, w_glu, w_br, w_out, ln_g, ln_b):
    p = _prepare_params(w_in, w_pool, pool_scale, conv_w, conv_b, dt_bias, a_log, d_ssd, ig_bias, fg_bias,
                        lam_re, lam_im, b_re, b_im, c_re, c_im, log_dt, d_s5, w_glu, w_br, w_out, ln_g, ln_b)
    depth = w_in.shape[0]
    bsz = x_prompt.shape[0]
    nb = x_sample.shape[0]
    yp = x_prompt
    ys = x_sample.reshape(nb, D_MODEL)
    outs_p = [[] for _ in range(8)]
    outs_s = [[] for _ in range(8)]
    mats_s = None
    pool_t = jnp.transpose(state_pool, (0, 2, 1, 3))
    conv_t = jnp.transpose(state_ssd_conv, (0, 2, 1, 3))
    ssd_t = jnp.transpose(state_ssd, (0, 2, 3, 4, 1))
    mc_t = jnp.transpose(state_mlstm_C, (0, 2, 3, 4, 1))
    n_t = jnp.transpose(state_mlstm_n, (0, 2, 3, 1)).reshape(depth, BW, nb)
    m_t = jnp.transpose(state_mlstm_m, (0, 2, 1))
    re_t = jnp.transpose(state_s5_re, (0, 2, 3, 1)).reshape(depth, S5_W, nb)
    im_t = jnp.transpose(state_s5_im, (0, 2, 3, 1)).reshape(depth, S5_W, nb)
    for l in range(depth):
        yp, pool_p, conv_p, ssd_p, mc_p, mn_p, mm_p, re_p, im_p = _prompt_layer(yp, l, p)
        new_p = (pool_p, conv_p, ssd_p, mc_p, mn_p.reshape(bsz, HEADS, HD), mm_p[:, 0, :HEADS],
                 re_p.reshape(bsz, S5_GROUPS, S5_STATE), im_p.reshape(bsz, S5_GROUPS, S5_STATE))

        rows_t, mid, pool_s, conv_s, n_s, m_s, re_s, im_s = _sample_pre(
            ys, l, p, pool_t, conv_t, n_t, m_t, re_t, im_t)
        yoff_t, qc_t, ssd_s, mc_s = _sample_state(rows_t, l, ssd_t, mc_t, mats_s)
        mats_s = (ssd_s, mc_s)
        ys = _sample_post(ys, l, p, mid, yoff_t, qc_t)
        new_s = (pool_s, conv_s, None, None, n_s, m_s, re_s, im_s)
        for i in range(8):
            outs_p[i].append(new_p[i])
            outs_s[i].append(new_s[i])
    sp = [jnp.stack(o, axis=0) for o in outs_p]
    st = [mats_s[i - 2] if i in (2, 3) else jnp.stack(outs_s[i], axis=0) for i in range(8)]
    ss = [jnp.transpose(st[0], (0, 2, 1, 3)), jnp.transpose(st[1], (0, 2, 1, 3)),
          jnp.transpose(st[2], (0, 4, 1, 2, 3)), jnp.transpose(st[3], (0, 4, 1, 2, 3)),
          jnp.transpose(st[4].reshape(depth, HEADS, HD, nb), (0, 3, 1, 2)),
          jnp.transpose(st[5], (0, 2, 1)),
          jnp.transpose(st[6].reshape(depth, S5_GROUPS, S5_STATE, nb), (0, 3, 1, 2)),
          jnp.transpose(st[7].reshape(depth, S5_GROUPS, S5_STATE, nb), (0, 3, 1, 2))]
    out = [yp, ys.reshape(nb, 1, D_MODEL)]
    for i in range(8):
        out.append(sp[i])
        out.append(ss[i])
    return tuple(out)
```

```python
import functools

import jax
import jax.numpy as jnp
from jax import lax
from jax.experimental import pallas as pl
from jax.experimental.pallas import tpu as pltpu

F32 = jnp.float32
BF16 = jnp.bfloat16

D_MODEL = 1024
DEPTH = 2
PAST_LEN = 16384
BW = 256
POOL_WINDOWS = (2, 4, 8, 16)
POOL_BUF = 15
POOL_PAD = 32
HEADS = 4
HD = 64
SSD_STATE = 64
CONV_K = 4
CONV_DIM = 512
S5_GROUPS = 16
S5_CH = 16
S5_STATE = 64
S5_W = S5_GROUPS * S5_STATE
CHUNK = 128
TILE = 512
ALPHA = (2.0 * DEPTH) ** 0.25
LN_EPS = 1e-5

SUBLANES = 8
LANES = 128
VMEM_LIMIT = 62 * 1024 * 1024

PRE_W = 1792
POST_W = 5376
O_PU, O_XBC, O_Q, O_K, O_V, O_SU = 0, 256, 768, 1024, 1280, 1536
O_Z, O_OG, O_GATE = 0, 1024, 1280
PRE_BLOCK = POST_W // PRE_W
assert POST_W == PRE_BLOCK * PRE_W
WCOLS = 256
_SRC_POST = [256, 1028, 2316, 2828, 2060] + [3084 + WCOLS * i for i in range(16)]
_SRC_PRE = [0, 512, 768, 1284, 1540, 1796, 2572]
W_SRC = tuple(_SRC_POST + _SRC_PRE)
W_HALVED = tuple([1, 1, 1, 1, 0] + [1] * 16 + [0] * 7)
SMALL_SRC = ((1024, 4), (2052, 4), (2056, 4))

R_LAM = 0
R_P = 8
R_M = 24
LAMP_ROWS = R_M + 3 * 16

V_LNG, V_LNB, V_MISC, V_CONVB, V_CONVW = 0, 1, 2, 3, 4


def _dot(a, b):
    return jnp.dot(a.astype(BF16), b.astype(BF16), preferred_element_type=F32)


def _dot_nt(a, b):
    return lax.dot_general(a.astype(BF16), b.astype(BF16), (((1,), (1,)), ((), ())),
                           preferred_element_type=F32)


def _softplus(x):
    return jnp.logaddexp(x, 0.0)


def _sigmoid(x):
    return 0.5 * jnp.tanh(0.5 * x) + 0.5


def _silu(x):
    return x * _sigmoid(x)


def _silu_of_half(xh):
    return xh * (jnp.tanh(xh) + 1.0)


def _expand_heads(cols, width=HD):
    rows = cols[0].shape[0]
    lane = lax.broadcasted_iota(jnp.int32, (rows, HEADS * width), 1)
    out = jnp.broadcast_to(cols[HEADS - 1], (rows, HEADS * width))
    for h in range(HEADS - 2, -1, -1):
        out = jnp.where(lane < (h + 1) * width, jnp.broadcast_to(cols[h], (rows, HEADS * width)), out)
    return out


def _pool_select(w2, w4, w8, w16):
    lane = lax.broadcasted_iota(jnp.int32, w2.shape, 1)
    return jnp.where(lane < 64, w2, jnp.where(lane < 128, w4, jnp.where(lane < 192, w8, w16)))


def _pool_window_row():
    lane = lax.broadcasted_iota(jnp.int32, (1, BW), 1)
    return jnp.where(lane < 64, 2, jnp.where(lane < 128, 4, jnp.where(lane < 192, 8, 16)))


def _small_block(sm, vec_ref):
    bias = vec_ref[V_MISC:V_MISC + 1, 768:896]
    alog = vec_ref[V_MISC:V_MISC + 1, 896:1024]
    v = sm + bias
    sp = _softplus(v)
    lsg = -_softplus(-v)
    a_row = -jnp.exp(alog)
    return v, sp, lsg, a_row


def _merge_tail(x, z_acts, ys, gate_tanh, wbr_ref, wout_ref, vec_ref):
    merged = None
    for b in range(4):
        br = ys[b] * z_acts[b]
        pb = _dot(br, wbr_ref[b])
        gb = gate_tanh[b][...] + 1.0
        merged = gb * pb if merged is None else merged + gb * pb
    out = _dot(merged, wout_ref[...])
    r = ALPHA * x + out
    mu = jnp.mean(r, axis=-1, keepdims=True)
    var = jnp.mean(jnp.square(r - mu), axis=-1, keepdims=True)
    g = vec_ref[V_LNG:V_LNG + 1, :]
    b = vec_ref[V_LNB:V_LNB + 1, :]
    return (r - mu) * lax.rsqrt(var + LN_EPS) * g + b


def _s5_glu(y_s5, wglu_ref):
    glu = _dot(jax.nn.gelu(y_s5), wglu_ref[...])
    return glu[:, :BW] * _sigmoid(glu[:, BW:])


def _s5_prep_kernel(lam_ref, ldt_ref, b_ref, lamp_ref, bbd_ref):
    lr = lam_ref[0:1, :]
    li = lam_ref[1:2, :]
    step = jnp.exp(ldt_ref[0:1, :])
    e = jnp.exp(lr * step)
    lbr = e * jnp.cos(li * step)
    lbi = e * jnp.sin(li * step)
    den = lr * lr + li * li
    nr = lbr - 1.0
    qr = (nr * lr + lbi * li) / den
    qi = (lbi * lr - nr * li) / den
    bre = b_ref[0:S5_CH, :]
    bim = b_ref[S5_CH:2 * S5_CH, :]
    bb = jnp.concatenate([qr * bre - qi * bim, qr * bim + qi * bre], axis=1)
    tiled = jnp.concatenate([bb] * S5_GROUPS, axis=0)
    rowg = lax.broadcasted_iota(jnp.int32, tiled.shape, 0) // S5_CH
    colg = (lax.broadcasted_iota(jnp.int32, tiled.shape, 1) % S5_W) // S5_STATE
    bbd_ref[...] = jnp.where(rowg == colg, tiled, 0.0).astype(BF16)

    pows = [(lbr, lbi)]
    for _ in range(7):
        pr, pi = pows[-1]
        pows.append((pr * lbr - pi * lbi, pr * lbi + pi * lbr))
    row = lax.broadcasted_iota(jnp.int32, (SUBLANES, S5_W), 0)

    def bcast(v):
        return jnp.broadcast_to(v, (SUBLANES, S5_W))

    p_re = bcast(pows[7][0])
    p_im = bcast(pows[7][1])
    for j in range(6, -1, -1):
        p_re = jnp.where(row == j, bcast(pows[j][0]), p_re)
        p_im = jnp.where(row == j, bcast(pows[j][1]), p_im)
    lamp_ref[R_LAM:R_LAM + SUBLANES, :] = jnp.where(row == 0, bcast(lbr), jnp.where(row == 1, bcast(lbi), 0.0))
    lamp_ref[R_P:R_P + SUBLANES, :] = p_re
    lamp_ref[R_P + SUBLANES:R_P + 2 * SUBLANES, :] = p_im
    for i, d in enumerate((1, 2, 4)):
        base = R_M + 16 * i
        lamp_ref[base:base + SUBLANES, :] = jnp.where(row >= d, bcast(pows[d - 1][0]), 0.0)
        lamp_ref[base + SUBLANES:base + 2 * SUBLANES, :] = jnp.where(row >= d, bcast(pows[d - 1][1]), 0.0)


def _s5_prep(lam2, ldt, bcat):
    return pl.pallas_call(
        _s5_prep_kernel,
        out_shape=(jax.ShapeDtypeStruct((LAMP_ROWS, S5_W), F32),
                   jax.ShapeDtypeStruct((BW, 2 * S5_W), BF16)),
        name="s5_prep",
    )(lam2, ldt, bcat)


def _prompt_kernel(x_ref, wpre_ref, wpost_ref, wsm_ref, vec_ref, wpool_ref, lamp_ref, bbd_ref, cbd_ref,
                   wglu_ref, wbr_ref, wout_ref,
                   y_ref, pool_o, conv_o, ssd_o, mc_o, mn_o, mm_o, s5re_o, s5im_o,
                   poolh, poolw, convh, ssd_st, mc_st, nrow, mrow, s5cr, s5ci, gsc):
    T = x_ref.shape[1]
    Q = CHUNK
    c = pl.program_id(1)
    last = pl.num_programs(1) - 1

    @pl.when(c == 0)
    def _():
        poolh[0:POOL_PAD, :] = jnp.zeros((POOL_PAD, BW), F32)
        convh[0:SUBLANES, :] = jnp.zeros((SUBLANES, CONV_DIM), F32)
        ssd_st[...] = jnp.zeros(ssd_st.shape, F32)
        mc_st[...] = jnp.zeros(mc_st.shape, F32)
        nrow[...] = jnp.zeros(nrow.shape, F32)
        mrow[...] = jnp.zeros(mrow.shape, F32)
        s5cr[...] = jnp.zeros(s5cr.shape, F32)
        s5ci[...] = jnp.zeros(s5ci.shape, F32)

    x = x_ref[0]
    xb = x.astype(BF16)

    def pre(lo, width):
        return jnp.dot(xb, wpre_ref[:, lo:lo + width], preferred_element_type=F32)

    def post(lo, width):
        return jnp.dot(xb, wpost_ref[:, lo:lo + width], preferred_element_type=F32)

    rows_i = lax.broadcasted_iota(jnp.int32, (Q, Q), 0)
    cols_i = lax.broadcasted_iota(jnp.int32, (Q, Q), 1)
    causal = rows_i >= cols_i

    n_rows = POOL_PAD + T

    def pool_mix(pu):
        poolh[POOL_PAD:n_rows, :] = pu
        poolw[0, 8:n_rows, :] = poolh[8:n_rows, :] + poolh[7:n_rows - 1, :]
        poolw[1, 16:n_rows, :] = poolw[0, 16:n_rows, :] + poolw[0, 14:n_rows - 2, :]
        poolw[2, 24:n_rows, :] = poolw[1, 24:n_rows, :] + poolw[1, 20:n_rows - 4, :]
        w2 = poolw[0, POOL_PAD:n_rows, :]
        w4 = poolw[1, POOL_PAD:n_rows, :]
        w8 = poolw[2, POOL_PAD:n_rows, :]
        w16 = w8 + poolw[2, POOL_PAD - 8:n_rows - 8, :]
        pos = c * T + lax.broadcasted_iota(jnp.int32, (T, BW), 0)
        cnt = jnp.minimum(pos + 1, _pool_window_row()).astype(F32)
        dpool = _pool_select(w2, w4, w8, w16) / cnt - pu
        poolh[0:POOL_PAD, :] = poolh[T:n_rows, :]
        return _dot(dpool, wpool_ref[...]) * vec_ref[V_MISC:V_MISC + 1, 0:256]

    def conv_mix(xbc_raw):
        convh[SUBLANES:SUBLANES + T, :] = xbc_raw
        acc = vec_ref[V_CONVB:V_CONVB + 1, 0:CONV_DIM]
        for kk in range(CONV_K):
            acc = acc + convh[5 + kk:5 + kk + T, :] * vec_ref[V_CONVW + kk:V_CONVW + kk + 1, 0:CONV_DIM]
        convh[0:SUBLANES, :] = convh[T:T + SUBLANES, :]
        return _silu(acc)

    v_sm, sp_all, lsg, a_row = _small_block(jnp.dot(xb, wsm_ref[...], preferred_element_type=F32), vec_ref)
    lane_s = lax.broadcasted_iota(jnp.int32, (T, LANES), 1)
    g_all = jnp.where(lane_s < 4, sp_all * a_row, jnp.where(lane_s < 8, v_sm, jnp.where(lane_s < 12, lsg, 0.0)))
    tril = causal.astype(BF16)
    g_hi = g_all.astype(BF16)
    g_r1 = g_all - g_hi.astype(F32)
    g_mid = g_r1.astype(BF16)
    g_lo = (g_r1 - g_mid.astype(F32)).astype(BF16)
    cums = []
    for s in range(T // Q):
        rs = slice(s * Q, (s + 1) * Q)
        cum = (jnp.dot(tril, g_hi[rs], preferred_element_type=F32)
               + jnp.dot(tril, g_mid[rs], preferred_element_type=F32)
               + jnp.dot(tril, g_lo[rs], preferred_element_type=F32))
        cums.append((cum, cum.T, g_all[rs].T))

    proj = {}
    fillers = [
        lambda: proj.update(pu=pre(O_PU, BW), xbc=pre(O_XBC, CONV_DIM)),
        lambda: proj.update(q=pre(O_Q, BW), k=pre(O_K, BW) * (HD ** -0.5)),
        lambda: proj.update(v=pre(O_V, BW), zs=[post(O_Z + b * BW, BW) for b in range(4)]),
        lambda: proj.update(og=post(O_OG, BW)),
    ]
    y_s5 = _s5_chunk_scan(pre(O_SU, BW), fillers, lamp_ref, bbd_ref, cbd_ref, vec_ref, s5cr, s5ci)
    zs = proj["zs"]

    dyn0 = jnp.minimum(c, 0)
    gate_raw = post(O_GATE, D_MODEL)
    y_pool = pool_mix(proj["pu"])
    gsc[dyn0] = jnp.tanh(gate_raw)
    gate_raw = post(O_GATE + D_MODEL, D_MODEL)
    xbc = conv_mix(proj["xbc"])
    gsc[dyn0 + 1] = jnp.tanh(gate_raw)
    gate_raw = post(O_GATE + 2 * D_MODEL, D_MODEL)
    gsc[dyn0 + 2] = jnp.tanh(gate_raw)
    gate_raw = post(O_GATE + 3 * D_MODEL, D_MODEL)
    gsc[dyn0 + 3] = jnp.tanh(gate_raw)
    gates = [gsc.at[dyn0 + b] for b in range(4)]

    y_ssd_parts, h_m_parts = [], []
    for s in range(T // Q):
        y_c, h_c = _ssd_mlstm_chunk(
            xbc[s * Q:(s + 1) * Q], proj["q"][s * Q:(s + 1) * Q], proj["k"][s * Q:(s + 1) * Q],
            proj["v"][s * Q:(s + 1) * Q], sp_all[s * Q:(s + 1) * Q], g_all[s * Q:(s + 1) * Q],
            cums[s], causal, vec_ref, ssd_st, mc_st, nrow, mrow)
        y_ssd_parts.append(y_c)
        h_m_parts.append(h_c)
    y_ssd = jnp.concatenate(y_ssd_parts, axis=0)
    y_m = _sigmoid(proj["og"]) * jnp.concatenate(h_m_parts, axis=0)
    y_s5 = _s5_glu(y_s5, wglu_ref)

    z_acts = [_silu_of_half(z) for z in zs]
    y_ref[0] = _merge_tail(x, z_acts, [y_pool, y_ssd, y_m, y_s5], gates, wbr_ref, wout_ref, vec_ref)

    @pl.when(c == last)
    def _():
        pool_o[0] = poolh[POOL_PAD - POOL_BUF:POOL_PAD, :]
        conv_o[0] = convh[5:8, :]
        ssd_o[0] = ssd_st[...]
        mc_o[0] = mc_st[...]
        mn_o[0] = nrow[0:1, :]
        mm_o[0] = mrow[0:1, :]
        s5re_o[0] = s5cr[0:1, :]
        s5im_o[0] = s5ci[0:1, :]


def _ssd_mlstm_chunk(xbc, q, k, v, sp, g_blk, cums, causal, vec_ref, ssd_st, mc_st, nrow, mrow):
    T = xbc.shape[0]
    cum, cum_t, g_t = cums
    xs = xbc[:, 0:BW]
    dt_cols = [sp[:, h:h + 1] for h in range(HEADS)]
    acs_cols = [cum[:, h:h + 1] for h in range(HEADS)]
    acs_last = [cum[T - 1:T, h:h + 1] for h in range(HEADS)]
    xdt = xs * _expand_heads(dt_cols)
    xw_t = (xdt * _expand_heads([jnp.exp(acs_last[h] - acs_cols[h]) for h in range(HEADS)])).T
    d_ssd = vec_ref[V_MISC:V_MISC + 1, 256:512]
    m_row = mrow[0:1, :]
    n_row = nrow[0:1, :]
    hsl = [slice(h * HD, (h + 1) * HD) for h in range(HEADS)]
    b_gs = [xbc[:, BW + g * SSD_STATE:BW + (g + 1) * SSD_STATE] for g in range(2)]
    c_gs = [xbc[:, BW + 128 + g * SSD_STATE:BW + 128 + (g + 1) * SSD_STATE] for g in range(2)]

    cbs = [_dot_nt(c_gs[g], b_gs[g]) for g in range(2)]
    offs = [_dot_nt(c_gs[g], ssd_st[2 * g:2 * g + 2].reshape(2 * HD, SSD_STATE)) for g in range(2)]
    qks = [_dot_nt(q[:, hsl[h]], k[:, hsl[h]]) for h in range(HEADS)]
    qcs = [_dot(q[:, hsl[h]], mc_st[h]) for h in range(HEADS)]
    gms = [cbs[h // 2] * jnp.exp(jnp.where(causal, acs_cols[h] - cum_t[h:h + 1, :], -jnp.inf))
           for h in range(HEADS)]
    b_cols = [cum[:, 8 + h:9 + h] for h in range(HEADS)]
    m_prevs = [m_row[:, h:h + 1] for h in range(HEADS)]
    dlogs = [jnp.where(causal, b_cols[h] - cum_t[8 + h:9 + h, :] + g_t[4 + h:5 + h, :], -jnp.inf)
             for h in range(HEADS)]
    inters = [b_cols[h] + m_prevs[h] for h in range(HEADS)]
    m_ts = [jnp.maximum(inters[h], jnp.max(dlogs[h], axis=1, keepdims=True)) for h in range(HEADS)]
    ss = [qks[h] * jnp.exp(dlogs[h] - m_ts[h]) for h in range(HEADS)]
    wis = [jnp.exp(inters[h] - m_ts[h]) for h in range(HEADS)]
    y_diags = [_dot(gms[h], xdt[:, hsl[h]]) for h in range(HEADS)]
    svs = [_dot(ss[h], v[:, hsl[h]]) for h in range(HEADS)]
    y_off = jnp.concatenate(offs, axis=1) * _expand_heads([jnp.exp(acs_cols[h]) for h in range(HEADS)])
    y_ssd = jnp.concatenate(y_diags, axis=1) + y_off + d_ssd * xs
    h_heads, wk_parts, decays, m_news = [], [], [], []
    for h in range(HEADS):
        num = svs[h] + wis[h] * qcs[h]
        den = (jnp.sum(ss[h], axis=1, keepdims=True)
               + wis[h] * jnp.sum(q[:, hsl[h]] * n_row[:, hsl[h]], axis=1, keepdims=True))
        h_heads.append(num / jnp.maximum(jnp.abs(den), jnp.exp(-m_ts[h])))
        m_new = m_ts[h][T - 1:T, :]
        b_last = b_cols[h][T - 1:T, :]
        wk_parts.append(jnp.exp(b_last - b_cols[h] + g_blk[:, 4 + h:5 + h] - m_new))
        decays.append(jnp.exp(b_last + m_prevs[h] - m_new))
        m_news.append(m_new)
    for h in range(HEADS):
        ssd_st[h] = jnp.exp(acs_last[h]) * ssd_st[h] + _dot(xw_t[hsl[h], :], b_gs[h // 2])
    lane_m = lax.broadcasted_iota(jnp.int32, (1, LANES), 1)
    m_row_new = jnp.zeros((1, LANES), F32)
    for h in range(HEADS):
        m_row_new = jnp.where(lane_m == h, jnp.broadcast_to(m_news[h], (1, LANES)), m_row_new)
    mrow[0:1, :] = m_row_new
    wk = k * _expand_heads(wk_parts)
    wk_t = wk.T
    for h in range(HEADS):
        mc_st[h] = decays[h] * mc_st[h] + _dot(wk_t[h * HD:(h + 1) * HD, :], v[:, h * HD:(h + 1) * HD])
    nrow[0:1, :] = _expand_heads(decays) * n_row + jnp.sum(wk, axis=0, keepdims=True)
    return y_ssd, jnp.concatenate(h_heads, axis=1)


def _s5_chunk_scan(su, fillers, lamp_ref, bbd_ref, cbd_ref, vec_ref, s5cr, s5ci):
    T = su.shape[0]
    nblk = T // SUBLANES
    bu = _dot(su, bbd_ref[...])
    h_re = bu[:, 0:S5_W].reshape(nblk, SUBLANES, S5_W)
    h_im = bu[:, S5_W:2 * S5_W].reshape(nblk, SUBLANES, S5_W)
    for i, d in enumerate((1, 2, 4)):
        base = R_M + 16 * i
        m_re = lamp_ref[base:base + SUBLANES, :]
        m_im = lamp_ref[base + SUBLANES:base + 2 * SUBLANES, :]
        r_re = pltpu.roll(h_re, d, 1)
        r_im = pltpu.roll(h_im, d, 1)
        h_re, h_im = h_re + (m_re * r_re - m_im * r_im), h_im + (m_re * r_im + m_im * r_re)
        fillers[i]()
    p_re = lamp_ref[R_P:R_P + SUBLANES, :]
    p_im = lamp_ref[R_P + SUBLANES:R_P + 2 * SUBLANES, :]
    cr = s5cr[0:1, :]
    ci = s5ci[0:1, :]
    blk_re = []
    blk_im = []
    for j in range(nblk):
        br = h_re[j] + (p_re * cr - p_im * ci)
        bi = h_im[j] + (p_re * ci + p_im * cr)
        blk_re.append(br)
        blk_im.append(bi)
        cr = br[SUBLANES - 1:SUBLANES, :]
        ci = bi[SUBLANES - 1:SUBLANES, :]
    s5cr[0:1, :] = cr
    s5ci[0:1, :] = ci
    fillers[3]()
    h_all = jnp.concatenate([jnp.concatenate(blk_re, axis=0), jnp.concatenate(blk_im, axis=0)], axis=1)
    return _dot(h_all, cbd_ref[...]) + vec_ref[V_MISC:V_MISC + 1, 512:768] * su


def _const_spec(shape, layer):
    nd = len(shape)
    return pl.BlockSpec((None,) + tuple(shape), lambda *_: (layer,) + (0,) * nd,
                        pipeline_mode=pl.Buffered(1))


def _w_spec(width, block, layer):
    return pl.BlockSpec((None, D_MODEL, width), lambda *_: (layer, 0, block), pipeline_mode=pl.Buffered(1))


def _prompt_layer(x, layer, p):
    bsz, seq, _ = x.shape
    T = TILE if seq % TILE == 0 else CHUNK
    nc = seq // T

    def bmap(nd):
        return lambda b, c: (b,) + (0,) * nd

    out_shape = (
        jax.ShapeDtypeStruct((bsz, seq, D_MODEL), F32),
        jax.ShapeDtypeStruct((bsz, POOL_BUF, BW), F32),
        jax.ShapeDtypeStruct((bsz, CONV_K - 1, CONV_DIM), F32),
        jax.ShapeDtypeStruct((bsz, HEADS, HD, SSD_STATE), F32),
        jax.ShapeDtypeStruct((bsz, HEADS, HD, HD), F32),
        jax.ShapeDtypeStruct((bsz, 1, BW), F32),
        jax.ShapeDtypeStruct((bsz, 1, LANES), F32),
        jax.ShapeDtypeStruct((bsz, 1, S5_W), F32),
        jax.ShapeDtypeStruct((bsz, 1, S5_W), F32),
    )
    out_specs = (
        pl.BlockSpec((1, T, D_MODEL), lambda b, c: (b, c, 0)),
        pl.BlockSpec((1, POOL_BUF, BW), bmap(2)),
        pl.BlockSpec((1, CONV_K - 1, CONV_DIM), bmap(2)),
        pl.BlockSpec((1, HEADS, HD, SSD_STATE), bmap(3)),
        pl.BlockSpec((1, HEADS, HD, HD), bmap(3)),
        pl.BlockSpec((1, 1, BW), bmap(2)),
        pl.BlockSpec((1, 1, LANES), bmap(2)),
        pl.BlockSpec((1, 1, S5_W), bmap(2)),
        pl.BlockSpec((1, 1, S5_W), bmap(2)),
    )
    in_specs = [
        pl.BlockSpec((1, T, D_MODEL), lambda b, c: (b, c, 0)),
        _w_spec(PRE_W, PRE_BLOCK, layer),
        _w_spec(POST_W, 0, layer),
        _const_spec((D_MODEL, LANES), layer),
        _const_spec((16, D_MODEL), layer),
        _const_spec((BW, BW), layer),
        _const_spec((LAMP_ROWS, S5_W), layer),
        _const_spec((BW, 2 * S5_W), layer),
        _const_spec((2 * S5_W, BW), layer),
        _const_spec((BW, 2 * BW), layer),
        _const_spec((4, BW, D_MODEL), layer),
        _const_spec((D_MODEL, D_MODEL), layer),
    ]
    scratch = [
        pltpu.VMEM((T + POOL_PAD, BW), F32),
        pltpu.VMEM((3, T + POOL_PAD, BW), F32),
        pltpu.VMEM((T + SUBLANES, CONV_DIM), F32),
        pltpu.VMEM((HEADS, HD, SSD_STATE), F32),
        pltpu.VMEM((HEADS, HD, HD), F32),
        pltpu.VMEM((SUBLANES, BW), F32),
        pltpu.VMEM((SUBLANES, LANES), F32),
        pltpu.VMEM((SUBLANES, S5_W), F32),
        pltpu.VMEM((SUBLANES, S5_W), F32),
        pltpu.VMEM((4, T, D_MODEL), F32),
    ]
    return pl.pallas_call(
        _prompt_kernel,
        grid=(bsz, nc),
        in_specs=in_specs,
        out_specs=out_specs,
        out_shape=out_shape,
        scratch_shapes=scratch,
        compiler_params=pltpu.CompilerParams(dimension_semantics=("arbitrary", "arbitrary"),
                                             vmem_limit_bytes=VMEM_LIMIT),
        name="prompt_layer",
    )(x, p["wall"], p["wall"], p["wsm"], p["vec"], p["wpool"], p["lamp"], p["bbd"], p["cbd"],
      p["wglu"], p["wbr"], p["wout"])


RW_BC, RW_XDT, RW_Q, RW_WK, RW_V, RW_SDEC, RW_MDEC, RW_W = 0, 256, 512, 768, 1024, 1280, 1408, 1536
MID_W = 7 * BW


def _sample_pre_kernel(x_ref, wpre_ref, wsm_ref, vec_ref, wpool_ref, lamp_ref, bbd_ref, cbd_ref,
                       pool_ref, conv_ref, n_ref, m_ref, s5re_ref, s5im_ref,
                       rows_o, mid_o, pool_o, conv_o, n_o, m_o, s5re_o, s5im_o):
    x = x_ref[...]
    nb = x.shape[0]
    pp = _dot(x, wpre_ref[...])
    pu = pp[:, O_PU:O_PU + BW]
    xbc_raw = pp[:, O_XBC:O_XBC + CONV_DIM]
    q = pp[:, O_Q:O_Q + BW]
    k = pp[:, O_K:O_K + BW] * (HD ** -0.5)
    v = pp[:, O_V:O_V + BW]
    su = pp[:, O_SU:O_SU + BW]

    def ext(j):
        return pool_ref[j]

    w2 = pu + ext(14)
    w4 = w2 + ext(13) + ext(12)
    w8 = w4
    for j in range(11, 7, -1):
        w8 = w8 + ext(j)
    w16 = w8
    for j in range(7, -1, -1):
        w16 = w16 + ext(j)
    cnt = jnp.minimum(PAST_LEN + 1, _pool_window_row()).astype(F32)
    dpool = _pool_select(w2, w4, w8, w16) / cnt - pu
    y_pool = _dot(dpool, wpool_ref[...]) * vec_ref[V_MISC:V_MISC + 1, 0:256]
    for j in range(POOL_BUF - 1):
        pool_o[j] = pool_ref[j + 1]
    pool_o[POOL_BUF - 1] = pu

    acc = vec_ref[V_CONVB:V_CONVB + 1, 0:CONV_DIM]
    for kk in range(CONV_K - 1):
        acc = acc + conv_ref[kk] * vec_ref[V_CONVW + kk:V_CONVW + kk + 1, 0:CONV_DIM]
    acc = acc + xbc_raw * vec_ref[V_CONVW + CONV_K - 1:V_CONVW + CONV_K, 0:CONV_DIM]
    for kk in range(CONV_K - 2):
        conv_o[kk] = conv_ref[kk + 1]
    conv_o[CONV_K - 2] = xbc_raw
    xbc = _silu(acc)
    xs = xbc[:, 0:BW]
    bc = xbc[:, BW:2 * BW]

    v_sm, sp, lsg, a_row = _small_block(_dot(x, wsm_ref[...]), vec_ref)
    sdec_blk = jnp.exp(sp * a_row)
    dt_cols = [sp[:, h:h + 1] for h in range(HEADS)]
    xdt = xs * _expand_heads(dt_cols)
    cb_cols = []
    for g in range(2):
        b_g = bc[:, g * SSD_STATE:(g + 1) * SSD_STATE]
        c_g = bc[:, 128 + g * SSD_STATE:128 + (g + 1) * SSD_STATE]
        cb_cols.append(jnp.sum(c_g * b_g, axis=1, keepdims=True))
    yssd_part = (_expand_heads([cb_cols[h // 2] for h in range(HEADS)]) * xdt
                 + vec_ref[V_MISC:V_MISC + 1, 256:512] * xs)
    sdec_exp = _expand_heads([sdec_blk[:, h:h + 1] for h in range(HEADS)])

    n0 = n_ref[...].T
    m0 = jnp.concatenate([m_ref[...], jnp.zeros((SUBLANES - HEADS, nb), F32)], axis=0).T
    s_cols, wi_cols, dw_cols, dmax_cols, m_cols = [], [], [], [], []
    for h in range(HEADS):
        sl = slice(h * HD, (h + 1) * HD)
        ig = v_sm[:, 4 + h:5 + h]
        lf = lsg[:, 8 + h:9 + h]
        inter = lf + m0[:, h:h + 1]
        m_t = jnp.maximum(inter, (lf - lf) + ig)
        dw = jnp.exp((lf - lf) + ig - m_t)
        wi = jnp.exp(inter - m_t)
        s = jnp.sum(q[:, sl] * k[:, sl], axis=1, keepdims=True) * dw
        den = s + wi * jnp.sum(q[:, sl] * n0[:, sl], axis=1, keepdims=True)
        s_cols.append(s)
        wi_cols.append(wi)
        dw_cols.append(dw)
        dmax_cols.append(jnp.maximum(jnp.abs(den), jnp.exp(-m_t)))
        m_cols.append(m_t)
    wi_exp = _expand_heads(wi_cols)
    wk = _expand_heads(dw_cols) * k
    n_o[...] = (wi_exp * n0 + wk).T
    lane = lax.broadcasted_iota(jnp.int32, (nb, LANES), 1)
    m_blk = jnp.zeros((nb, LANES), F32)
    wi_blk = jnp.zeros((nb, LANES), F32)
    for h in range(HEADS):
        m_blk = jnp.where(lane == h, jnp.broadcast_to(m_cols[h], (nb, LANES)), m_blk)
        wi_blk = jnp.where(lane == h, jnp.broadcast_to(wi_cols[h], (nb, LANES)), wi_blk)
    m_o[...] = m_blk.T[0:HEADS, :]

    bu = _dot(su, bbd_ref[...])
    lbr = lamp_ref[R_LAM:R_LAM + 1, :]
    lbi = lamp_ref[R_LAM + 1:R_LAM + 2, :]
    h0r = s5re_ref[...].T
    h0i = s5im_ref[...].T
    h_re = bu[:, 0:S5_W] + (lbr * h0r - lbi * h0i)
    h_im = bu[:, S5_W:2 * S5_W] + (lbr * h0i + lbi * h0r)
    s5re_o[...] = h_re.T
    s5im_o[...] = h_im.T
    y_s5 = _dot(jnp.concatenate([h_re, h_im], axis=1), cbd_ref[...]) + vec_ref[V_MISC:V_MISC + 1, 512:768] * su

    rows_o[...] = jnp.concatenate([bc, xdt, q, wk, v, sdec_blk, wi_blk], axis=1).T
    mid_o[...] = jnp.concatenate([y_pool, yssd_part, sdec_exp, _expand_heads(s_cols) * v, wi_exp,
                                  _expand_heads(dmax_cols), y_s5], axis=1)


def _sample_pre(x, layer, p, pool2, conv2, n2, m2, s5re2, s5im2):
    nb = x.shape[0]
    full = lambda a: pl.BlockSpec(a.shape, lambda *_: (0,) * a.ndim)
    of_layer = lambda a: pl.BlockSpec((None,) + a.shape[1:], lambda *_: (layer,) + (0,) * (a.ndim - 1))
    out_shape = (
        jax.ShapeDtypeStruct((RW_W, nb), F32),
        jax.ShapeDtypeStruct((nb, MID_W), F32),
        jax.ShapeDtypeStruct(pool2.shape[1:], F32),
        jax.ShapeDtypeStruct(conv2.shape[1:], F32),
        jax.ShapeDtypeStruct((BW, nb), F32),
        jax.ShapeDtypeStruct((HEADS, nb), F32),
        jax.ShapeDtypeStruct((S5_W, nb), F32),
        jax.ShapeDtypeStruct((S5_W, nb), F32),
    )
    in_specs = [
        full(x),
        _w_spec(PRE_W, PRE_BLOCK, layer),
        _const_spec((D_MODEL, LANES), layer),
        _const_spec((16, D_MODEL), layer),
        _const_spec((BW, BW), layer),
        _const_spec((LAMP_ROWS, S5_W), layer),
        _const_spec((BW, 2 * S5_W), layer),
        _const_spec((2 * S5_W, BW), layer),
        of_layer(pool2), of_layer(conv2), of_layer(n2), of_layer(m2), of_layer(s5re2), of_layer(s5im2),
    ]
    return pl.pallas_call(
        _sample_pre_kernel,
        grid=(1,),
        in_specs=in_specs,
        out_specs=tuple(pl.BlockSpec(s.shape, lambda *_, nd=len(s.shape): (0,) * nd) for s in out_shape),
        out_shape=out_shape,
        compiler_params=pltpu.CompilerParams(dimension_semantics=("arbitrary",),
                                             vmem_limit_bytes=VMEM_LIMIT),
        name="sample_pre",
    )(x, p["wall"], p["wsm"], p["vec"], p["wpool"], p["lamp"], p["bbd"], p["cbd"],
      pool2, conv2, n2, m2, s5re2, s5im2)


def _sample_state_kernel(first_layer, rows_ref, ssd_ref, mc_ref, *rest):
    h = pl.program_id(0)
    part = pl.program_id(1)
    if first_layer:
        yoff_ref, qc_ref, ssd_o, mc_o = rest
        ssd_o[1] = ssd_ref[1]
        mc_o[1] = mc_ref[1]
        ssd_in, mc_in, ssd_out, mc_out = ssd_ref.at[0], mc_ref.at[0], ssd_o.at[0], mc_o.at[0]
    else:
        _, _, yoff_ref, qc_ref, ssd_out, mc_out = rest
        ssd_in, mc_in = ssd_ref, mc_ref
    g64 = pl.multiple_of((h // 2) * SSD_STATE, SSD_STATE)
    h64 = pl.multiple_of(h * HD, HD)
    b_t = rows_ref[pl.ds(RW_BC + g64, SSD_STATE), :]
    c_t = rows_ref[pl.ds(RW_BC + 128 + g64, SSD_STATE), :]
    v_t = rows_ref[pl.ds(RW_V + h64, HD), :]
    sdec = rows_ref[pl.ds(RW_SDEC + h, 1), :]
    mdec = rows_ref[pl.ds(RW_MDEC + h, 1), :]
    base = h64 + part * STATE_ROWS

    def body(i, acc):
        hs = ssd_in[i]
        yoff_ref[pl.ds(i, 1), :] = jnp.sum(hs * c_t, axis=0, keepdims=True)
        ssd_out[i] = sdec * hs + rows_ref[pl.ds(RW_XDT + base + i, 1), :] * b_t
        cs = mc_in[i]
        mc_out[i] = mdec * cs + rows_ref[pl.ds(RW_WK + base + i, 1), :] * v_t
        return acc + rows_ref[pl.ds(RW_Q + base + i, 1), :] * cs

    acc = lax.fori_loop(0, STATE_ROWS, body, jnp.zeros((HD, rows_ref.shape[1]), F32), unroll=4)

    @pl.when(part == 0)
    def _():
        qc_ref[...] = acc

    @pl.when(part != 0)
    def _():
        qc_ref[...] = qc_ref[...] + acc


STATE_ROWS = 32


def _sample_state(rows_t, layer, ssd_all, mc_all, prev=None):
    nb = rows_t.shape[1]
    depth = ssd_all.shape[0]
    parts = HD // STATE_ROWS
    tile = (STATE_ROWS, HD, nb)
    rows_spec = pl.BlockSpec(rows_t.shape, lambda h, s: (0, 0))
    yoff_spec = pl.BlockSpec((STATE_ROWS, nb), lambda h, s: (h * parts + s, 0))
    qc_spec = pl.BlockSpec((HD, nb), lambda h, s: (h, 0))
    out_shape = (jax.ShapeDtypeStruct((BW, nb), F32),
                 jax.ShapeDtypeStruct((BW, nb), F32),
                 jax.ShapeDtypeStruct(ssd_all.shape, F32),
                 jax.ShapeDtypeStruct(mc_all.shape, F32))
    params = pltpu.CompilerParams(dimension_semantics=("arbitrary", "arbitrary"),
                                  vmem_limit_bytes=VMEM_LIMIT)
    if prev is None:
        assert layer == 0 and depth == 2
        both = pl.BlockSpec((depth, None) + tile, lambda h, s: (0, h, s, 0, 0))
        return pl.pallas_call(
            functools.partial(_sample_state_kernel, True),
            grid=(HEADS, parts),
            in_specs=[rows_spec, both, both],
            out_specs=(yoff_spec, qc_spec, both, both),
            out_shape=out_shape,
            compiler_params=params,
            name="sample_state_first",
        )(rows_t, ssd_all, mc_all)
    one = pl.BlockSpec((None, None) + tile, lambda h, s: (layer, h, s, 0, 0))
    anywhere = pl.BlockSpec(memory_space=pl.ANY)
    return pl.pallas_call(
        functools.partial(_sample_state_kernel, False),
        grid=(HEADS, parts),
        in_specs=[rows_spec, one, one, anywhere, anywhere],
        out_specs=(yoff_spec, qc_spec, one, one),
        out_shape=out_shape,
        input_output_aliases={3: 2, 4: 3},
        compiler_params=params,
        name="sample_state_next",
    )(rows_t, ssd_all, mc_all, prev[0], prev[1])


def _sample_post_kernel(x_ref, wpost_ref, vec_ref, mid_ref, yoff_ref, qc_ref, wglu_ref, wbr_ref, wout_ref, y_ref):
    x = x_ref[...]
    xb = x.astype(BF16)

    def post(lo, width):
        return jnp.dot(xb, wpost_ref[:, lo:lo + width], preferred_element_type=F32)

    def mid(i):
        return mid_ref[:, i * BW:(i + 1) * BW]

    y_pool, yssd_part, sdec_exp, sv, wi_exp, dmax_exp, y_s5 = [mid(i) for i in range(7)]
    y_ssd = yssd_part + yoff_ref[...].T * sdec_exp
    hh = (sv + wi_exp * qc_ref[...].T) / dmax_exp
    y_m = _sigmoid(post(O_OG, BW)) * hh
    y_s5 = _s5_glu(y_s5, wglu_ref)
    z_acts = [_silu_of_half(post(O_Z + b * BW, BW)) for b in range(4)]
    gates = [jnp.tanh(post(O_GATE + b * D_MODEL, D_MODEL)) for b in range(4)]
    y_ref[...] = _merge_tail(x, z_acts, [y_pool, y_ssd, y_m, y_s5], gates, wbr_ref, wout_ref, vec_ref)


def _sample_post(x, layer, p, mid, yoff_t, qc_t):
    nb = x.shape[0]
    full = lambda a: pl.BlockSpec(a.shape, lambda *_: (0,) * a.ndim)
    return pl.pallas_call(
        _sample_post_kernel,
        grid=(1,),
        in_specs=[full(x), _w_spec(POST_W, 0, layer), _const_spec((16, D_MODEL), layer),
                  full(mid), full(yoff_t), full(qc_t), _const_spec((BW, 2 * BW), layer),
                  _const_spec((4, BW, D_MODEL), layer), _const_spec((D_MODEL, D_MODEL), layer)],
        out_specs=pl.BlockSpec((nb, D_MODEL), lambda *_: (0, 0)),
        out_shape=jax.ShapeDtypeStruct((nb, D_MODEL), F32),
        compiler_params=pltpu.CompilerParams(dimension_semantics=("arbitrary",),
                                             vmem_limit_bytes=VMEM_LIMIT),
        name="sample_post",
    )(x, p["wall"], p["vec"], mid, yoff_t, qc_t, p["wglu"], p["wbr"], p["wout"])


def _w_permute_kernel(src_ref, halved_ref, w_hbm, out_ref, buf, sem):
    i = pl.program_id(0)
    depth = out_ref.shape[0]

    def copies(step, slot):
        start = src_ref[step]
        return [pltpu.make_async_copy(w_hbm.at[pl.ds(start, WCOLS), l, :], buf.at[slot, l], sem.at[slot, l])
                for l in range(depth)]

    @pl.when(i == 0)
    def _():
        for cp in copies(0, 0):
            cp.start()

    @pl.when(i + 1 < pl.num_programs(0))
    def _():
        for cp in copies(i + 1, (i + 1) % 2):
            cp.start()

    for cp in copies(i, i % 2):
        cp.wait()
    scale = jnp.where(halved_ref[i] == 1, 0.5, 1.0).astype(F32)
    for l in range(depth):
        out_ref[l] = (buf[i % 2, l].T * scale).astype(BF16)


def _w_permute(w_in):
    depth, rows, _ = w_in.shape
    w_t = jnp.transpose(w_in, (2, 0, 1))
    nblk = len(W_SRC)
    return pl.pallas_call(
        _w_permute_kernel,
        grid_spec=pltpu.PrefetchScalarGridSpec(
            num_scalar_prefetch=2, grid=(nblk,),
            in_specs=[pl.BlockSpec(memory_space=pl.ANY)],
            out_specs=pl.BlockSpec((depth, rows, WCOLS), lambda i, src, halved: (0, 0, i)),
            scratch_shapes=[pltpu.VMEM((2, depth, WCOLS, rows), F32), pltpu.SemaphoreType.DMA((2, depth))]),
        out_shape=jax.ShapeDtypeStruct((depth, rows, nblk * WCOLS), BF16),
        compiler_params=pltpu.CompilerParams(dimension_semantics=("arbitrary",)),
        name="w_permute",
    )(jnp.asarray(W_SRC, jnp.int32), jnp.asarray(W_HALVED, jnp.int32), w_t)


def _prepare_params(w_in, w_pool, pool_scale, conv_w, conv_b, dt_bias, a_log, d_ssd, ig_bias, fg_bias,
                    lam_re, lam_im, b_re, b_im, c_re, c_im, log_dt, d_s5, w_glu, w_br, w_out, ln_g, ln_b):
    depth = w_in.shape[0]
    wall = _w_permute(w_in)
    small_cols = [w_in[:, :, lo:lo + n] for lo, n in SMALL_SRC]
    wsm = jnp.concatenate(small_cols + [jnp.zeros((depth, D_MODEL, LANES - 12), w_in.dtype)], axis=2).astype(BF16)

    zeros = lambda n: jnp.zeros((depth, n), F32)
    misc = jnp.concatenate([pool_scale, jnp.repeat(d_ssd, HD, axis=1), d_s5,
                            dt_bias, ig_bias, fg_bias, zeros(LANES - 12),
                            a_log, zeros(LANES - HEADS)], axis=1)
    pad512 = lambda a: jnp.concatenate([a, jnp.zeros(a.shape[:-1] + (D_MODEL - CONV_DIM,), F32)], axis=-1)
    vec = jnp.concatenate([ln_g[:, None], ln_b[:, None], misc[:, None], pad512(conv_b)[:, None],
                           pad512(conv_w), jnp.zeros((depth, 16 - 4 - CONV_K, D_MODEL), F32)], axis=1)

    eye4 = jnp.eye(len(POOL_WINDOWS), dtype=F32)
    wpool = jnp.einsum("lgce,gh->lgche", w_pool, eye4).reshape(depth, BW, BW).astype(BF16)

    lam2 = jnp.stack([lam_re.reshape(depth, S5_W), lam_im.reshape(depth, S5_W)], axis=1)
    ldt = jnp.repeat(log_dt, S5_STATE, axis=1)[:, None]
    to_rows = lambda b: jnp.transpose(b, (0, 3, 1, 2)).reshape(depth, S5_CH, S5_W)
    bcat = jnp.concatenate([to_rows(b_re), to_rows(b_im)], axis=1)

    eye_g = jnp.eye(S5_GROUPS, dtype=F32)
    cb = lambda cm: jnp.einsum("lgcp,gh->lgphc", cm, eye_g).reshape(depth, S5_W, BW)
    cbd = jnp.concatenate([cb(c_re), -cb(c_im)], axis=1).astype(BF16)

    lamp, bbd = [], []
    for l in range(depth):
        lp, bb = _s5_prep(lam2[l], ldt[l], bcat[l])
        lamp.append(lp)
        bbd.append(bb)
    return dict(wall=wall, wsm=wsm, vec=vec, wpool=wpool, lamp=jnp.stack(lamp), bbd=jnp.stack(bbd),
                cbd=cbd, wglu=w_glu.astype(BF16), wbr=w_br.astype(BF16), wout=(0.5 * w_out).astype(BF16))


def kernel(x_prompt, x_sample, state_pool, state_ssd_conv, state_ssd, state_mlstm_C, state_mlstm_n, state_mlstm_m, state_s5_re, state_s5_im, w_in, w_pool, pool_scale, conv_w, conv_b, dt_bias, a_log, d_ssd, ig_bias, fg_bias, lam_re, lam_im, b_re, b_im, c_re, c_im, log_dt, d_s5, w_glu, w_br, w_out, ln_g, ln_b):
    p = _prepare_params(w_in, w_pool, pool_scale, conv_w, conv_b, dt_bias, a_log, d_ssd, ig_bias, fg_bias,
                        lam_re, lam_im, b_re, b_im, c_re, c_im, log_dt, d_s5, w_glu, w_br, w_out, ln_g, ln_b)
    depth = w_in.shape[0]
    bsz = x_prompt.shape[0]
    nb = x_sample.shape[0]
    yp = x_prompt
    ys = x_sample.reshape(nb, D_MODEL)
    outs_p = [[] for _ in range(8)]
    outs_s = [[] for _ in range(8)]
    mats_s = None
    pool_t = jnp.transpose(state_pool, (0, 2, 1, 3))
    conv_t = jnp.transpose(state_ssd_conv, (0, 2, 1, 3))
    ssd_t = jnp.transpose(state_ssd, (0, 2, 3, 4, 1))
    mc_t = jnp.transpose(state_mlstm_C, (0, 2, 3, 4, 1))
    n_t = jnp.transpose(state_mlstm_n, (0, 2, 3, 1)).reshape(depth, BW, nb)
    m_t = jnp.transpose(state_mlstm_m, (0, 2, 1))
    re_t = jnp.transpose(state_s5_re, (0, 2, 3, 1)).reshape(depth, S5_W, nb)
    im_t = jnp.transpose(state_s5_im, (0, 2, 3, 1)).reshape(depth, S5_W, nb)
    for l in range(depth):
        yp, pool_p, conv_p, ssd_p, mc_p, mn_p, mm_p, re_p, im_p = _prompt_layer(yp, l, p)
        new_p = (pool_p, conv_p, ssd_p, mc_p, mn_p.reshape(bsz, HEADS, HD), mm_p[:, 0, :HEADS],
                 re_p.reshape(bsz, S5_GROUPS, S5_STATE), im_p.reshape(bsz, S5_GROUPS, S5_STATE))

        rows_t, mid, pool_s, conv_s, n_s, m_s, re_s, im_s = _sample_pre(
            ys, l, p, pool_t, conv_t, n_t, m_t, re_t, im_t)
        yoff_t, qc_t, ssd_s, mc_s = _sample_state(rows_t, l, ssd_t, mc_t, mats_s)
        mats_s = (ssd_s, mc_s)
        ys = _sample_post(ys, l, p, mid, yoff_t, qc_t)
        new_s = (pool_s, conv_s, None, None, n_s, m_s, re_s, im_s)
        for i in range(8):
            outs_p[i].append(new_p[i])
            outs_s[i].append(new_s[i])
    sp = [jnp.stack(o, axis=0) for o in outs_p]
    st = [mats_s[i - 2] if i in (2, 3) else jnp.stack(outs_s[i], axis=0) for i in range(8)]
    ss = [jnp.transpose(st[0], (0, 2, 1, 3)), jnp.transpose(st[1], (0, 2, 1, 3)),
          jnp.transpose(st[2], (0, 4, 1, 2, 3)), jnp.transpose(st[3], (0, 4, 1, 2, 3)),
          jnp.transpose(st[4].reshape(depth, HEADS, HD, nb), (0, 3, 1, 2)),
          jnp.transpose(st[5], (0, 2, 1)),
          jnp.transpose(st[6].reshape(depth, S5_GROUPS, S5_STATE, nb), (0, 3, 1, 2)),
          jnp.transpose(st[7].reshape(depth, S5_GROUPS, S5_STATE, nb), (0, 3, 1, 2))]
    out = [yp, ys.reshape(nb, 1, D_MODEL)]
    for i in range(8):
        out.append(sp[i])
        out.append(ss[i])
    return tuple(out)
```

```python
import functools

import jax
import jax.numpy as jnp
from jax import lax
from jax.experimental import pallas as pl
from jax.experimental.pallas import tpu as pltpu

F32 = jnp.float32
BF16 = jnp.bfloat16

D_MODEL = 1024
DEPTH = 2
PAST_LEN = 16384
BW = 256
POOL_WINDOWS = (2, 4, 8, 16)
POOL_BUF = 15
POOL_PAD = 32
HEADS = 4
HD = 64
SSD_STATE = 64
CONV_K = 4
CONV_DIM = 512
S5_GROUPS = 16
S5_CH = 16
S5_STATE = 64
S5_W = S5_GROUPS * S5_STATE
CHUNK = 128
TILE = 512
ALPHA = (2.0 * DEPTH) ** 0.25
LN_EPS = 1e-5

SUBLANES = 8
LANES = 128
VMEM_LIMIT = 62 * 1024 * 1024

PRE_W = 1792
POST_W = 5376
O_PU, O_XBC, O_Q, O_K, O_V, O_SU = 0, 256, 768, 1024, 1280, 1536
O_Z, O_OG, O_GATE = 0, 1024, 1280
PRE_BLOCK = POST_W // PRE_W
assert POST_W == PRE_BLOCK * PRE_W
WCOLS = 256
_SRC_POST = [256, 1028, 2316, 2828, 2060] + [3084 + WCOLS * i for i in range(16)]
_SRC_PRE = [0, 512, 768, 1284, 1540, 1796, 2572]
W_SRC = tuple(_SRC_POST + _SRC_PRE)
W_HALVED = tuple([1, 1, 1, 1, 0] + [1] * 16 + [0] * 7)
SMALL_SRC = ((1024, 4), (2052, 4), (2056, 4))

R_LAM = 0
R_P = 8
R_M = 24
LAMP_ROWS = R_M + 3 * 16

V_LNG, V_LNB, V_MISC, V_CONVB, V_CONVW = 0, 1, 2, 3, 4


def _dot(a, b):
    return jnp.dot(a.astype(BF16), b.astype(BF16), preferred_element_type=F32)


def _dot_nt(a, b):
    return lax.dot_general(a.astype(BF16), b.astype(BF16), (((1,), (1,)), ((), ())),
                           preferred_element_type=F32)


def _softplus(x):
    return jnp.logaddexp(x, 0.0)


def _sigmoid(x):
    return 0.5 * jnp.tanh(0.5 * x) + 0.5


def _silu(x):
    return x * _sigmoid(x)


def _silu_of_half(xh):
    return xh * (jnp.tanh(xh) + 1.0)


def _expand_heads(cols, width=HD):
    rows = cols[0].shape[0]
    lane = lax.broadcasted_iota(jnp.int32, (rows, HEADS * width), 1)
    out = jnp.broadcast_to(cols[HEADS - 1], (rows, HEADS * width))
    for h in range(HEADS - 2, -1, -1):
        out = jnp.where(lane < (h + 1) * width, jnp.broadcast_to(cols[h], (rows, HEADS * width)), out)
    return out


def _pool_select(w2, w4, w8, w16):
    lane = lax.broadcasted_iota(jnp.int32, w2.shape, 1)
    return jnp.where(lane < 64, w2, jnp.where(lane < 128, w4, jnp.where(lane < 192, w8, w16)))


def _pool_window_row():
    lane = lax.broadcasted_iota(jnp.int32, (1, BW), 1)
    return jnp.where(lane < 64, 2, jnp.where(lane < 128, 4, jnp.where(lane < 192, 8, 16)))


def _small_block(sm, vec_ref):
    bias = vec_ref[V_MISC:V_MISC + 1, 768:896]
    alog = vec_ref[V_MISC:V_MISC + 1, 896:1024]
    v = sm + bias
    sp = _softplus(v)
    lsg = -_softplus(-v)
    a_row = -jnp.exp(alog)
    return v, sp, lsg, a_row


def _merge_tail(x, z_acts, ys, gate_tanh, wbr_ref, wout_ref, vec_ref):
    merged = None
    for b in range(4):
        br = ys[b] * z_acts[b]
        pb = _dot(br, wbr_ref[b])
        gb = gate_tanh[b][...] + 1.0
        merged = gb * pb if merged is None else merged + gb * pb
    out = _dot(merged, wout_ref[...])
    r = ALPHA * x + out
    mu = jnp.mean(r, axis=-1, keepdims=True)
    var = jnp.mean(jnp.square(r - mu), axis=-1, keepdims=True)
    g = vec_ref[V_LNG:V_LNG + 1, :]
    b = vec_ref[V_LNB:V_LNB + 1, :]
    return (r - mu) * lax.rsqrt(var + LN_EPS) * g + b


def _s5_glu(y_s5, wglu_ref):
    glu = _dot(jax.nn.gelu(y_s5), wglu_ref[...])
    return glu[:, :BW] * _sigmoid(glu[:, BW:])


def _s5_prep_kernel(lam_ref, ldt_ref, b_ref, lamp_ref, bbd_ref):
    lr = lam_ref[0:1, :]
    li = lam_ref[1:2, :]
    step = jnp.exp(ldt_ref[0:1, :])
    e = jnp.exp(lr * step)
    lbr = e * jnp.cos(li * step)
    lbi = e * jnp.sin(li * step)
    den = lr * lr + li * li
    nr = lbr - 1.0
    qr = (nr * lr + lbi * li) / den
    qi = (lbi * lr - nr * li) / den
    bre = b_ref[0:S5_CH, :]
    bim = b_ref[S5_CH:2 * S5_CH, :]
    bb = jnp.concatenate([qr * bre - qi * bim, qr * bim + qi * bre], axis=1)
    tiled = jnp.concatenate([bb] * S5_GROUPS, axis=0)
    rowg = lax.broadcasted_iota(jnp.int32, tiled.shape, 0) // S5_CH
    colg = (lax.broadcasted_iota(jnp.int32, tiled.shape, 1) % S5_W) // S5_STATE
    bbd_ref[...] = jnp.where(rowg == colg, tiled, 0.0).astype(BF16)

    pows = [(lbr, lbi)]
    for _ in range(7):
        pr, pi = pows[-1]
        pows.append((pr * lbr - pi * lbi, pr * lbi + pi * lbr))
    row = lax.broadcasted_iota(jnp.int32, (SUBLANES, S5_W), 0)

    def bcast(v):
        return jnp.broadcast_to(v, (SUBLANES, S5_W))

    p_re = bcast(pows[7][0])
    p_im = bcast(pows[7][1])
    for j in range(6, -1, -1):
        p_re = jnp.where(row == j, bcast(pows[j][0]), p_re)
        p_im = jnp.where(row == j, bcast(pows[j][1]), p_im)
    lamp_ref[R_LAM:R_LAM + SUBLANES, :] = jnp.where(row == 0, bcast(lbr), jnp.where(row == 1, bcast(lbi), 0.0))
    lamp_ref[R_P:R_P + SUBLANES, :] = p_re
    lamp_ref[R_P + SUBLANES:R_P + 2 * SUBLANES, :] = p_im
    for i, d in enumerate((1, 2, 4)):
        base = R_M + 16 * i
        lamp_ref[base:base + SUBLANES, :] = jnp.where(row >= d, bcast(pows[d - 1][0]), 0.0)
        lamp_ref[base + SUBLANES:base + 2 * SUBLANES, :] = jnp.where(row >= d, bcast(pows[d - 1][1]), 0.0)


def _s5_prep(lam2, ldt, bcat):
    return pl.pallas_call(
        _s5_prep_kernel,
        out_shape=(jax.ShapeDtypeStruct((LAMP_ROWS, S5_W), F32),
                   jax.ShapeDtypeStruct((BW, 2 * S5_W), BF16)),
        name="s5_prep",
    )(lam2, ldt, bcat)


def _prompt_kernel(x_ref, wpre_ref, wpost_ref, wsm_ref, vec_ref, wpool_ref, lamp_ref, bbd_ref, cbd_ref,
                   wglu_ref, wbr_ref, wout_ref,
                   y_ref, pool_o, conv_o, ssd_o, mc_o, mn_o, mm_o, s5re_o, s5im_o,
                   poolh, poolw, convh, ssd_st, mc_st, nrow, mrow, s5cr, s5ci, gsc):
    T = x_ref.shape[1]
    Q = CHUNK
    c = pl.program_id(1)
    last = pl.num_programs(1) - 1

    @pl.when(c == 0)
    def _():
        poolh[0:POOL_PAD, :] = jnp.zeros((POOL_PAD, BW), F32)
        convh[0:SUBLANES, :] = jnp.zeros((SUBLANES, CONV_DIM), F32)
        ssd_st[...] = jnp.zeros(ssd_st.shape, F32)
        mc_st[...] = jnp.zeros(mc_st.shape, F32)
        nrow[...] = jnp.zeros(nrow.shape, F32)
        mrow[...] = jnp.zeros(mrow.shape, F32)
        s5cr[...] = jnp.zeros(s5cr.shape, F32)
        s5ci[...] = jnp.zeros(s5ci.shape, F32)

    x = x_ref[0]
    xb = x.astype(BF16)

    def pre(lo, width):
        return jnp.dot(xb, wpre_ref[:, lo:lo + width], preferred_element_type=F32)

    def post(lo, width):
        return jnp.dot(xb, wpost_ref[:, lo:lo + width], preferred_element_type=F32)

    rows_i = lax.broadcasted_iota(jnp.int32, (Q, Q), 0)
    cols_i = lax.broadcasted_iota(jnp.int32, (Q, Q), 1)
    causal = rows_i >= cols_i

    n_rows = POOL_PAD + T

    def pool_mix(pu):
        poolh[POOL_PAD:n_rows, :] = pu
        poolw[0, 8:n_rows, :] = poolh[8:n_rows, :] + poolh[7:n_rows - 1, :]
        poolw[1, 16:n_rows, :] = poolw[0, 16:n_rows, :] + poolw[0, 14:n_rows - 2, :]
        poolw[2, 24:n_rows, :] = poolw[1, 24:n_rows, :] + poolw[1, 20:n_rows - 4, :]
        w2 = poolw[0, POOL_PAD:n_rows, :]
        w4 = poolw[1, POOL_PAD:n_rows, :]
        w8 = poolw[2, POOL_PAD:n_rows, :]
        w16 = w8 + poolw[2, POOL_PAD - 8:n_rows - 8, :]
        pos = c * T + lax.broadcasted_iota(jnp.int32, (T, BW), 0)
        cnt = jnp.minimum(pos + 1, _pool_window_row()).astype(F32)
        dpool = _pool_select(w2, w4, w8, w16) / cnt - pu
        poolh[0:POOL_PAD, :] = poolh[T:n_rows, :]
        return _dot(dpool, wpool_ref[...]) * vec_ref[V_MISC:V_MISC + 1, 0:256]

    def conv_mix(xbc_raw):
        convh[SUBLANES:SUBLANES + T, :] = xbc_raw
        acc = vec_ref[V_CONVB:V_CONVB + 1, 0:CONV_DIM]
        for kk in range(CONV_K):
            acc = acc + convh[5 + kk:5 + kk + T, :] * vec_ref[V_CONVW + kk:V_CONVW + kk + 1, 0:CONV_DIM]
        convh[0:SUBLANES, :] = convh[T:T + SUBLANES, :]
        return _silu(acc)

    v_sm, sp_all, lsg, a_row = _small_block(jnp.dot(xb, wsm_ref[...], preferred_element_type=F32), vec_ref)
    lane_s = lax.broadcasted_iota(jnp.int32, (T, LANES), 1)
    g_all = jnp.where(lane_s < 4, sp_all * a_row, jnp.where(lane_s < 8, v_sm, jnp.where(lane_s < 12, lsg, 0.0)))
    tril = causal.astype(BF16)
    g_hi = g_all.astype(BF16)
    g_r1 = g_all - g_hi.astype(F32)
    g_mid = g_r1.astype(BF16)
    g_lo = (g_r1 - g_mid.astype(F32)).astype(BF16)
    cums = []
    for s in range(T // Q):
        rs = slice(s * Q, (s + 1) * Q)
        cum = (jnp.dot(tril, g_hi[rs], preferred_element_type=F32)
               + jnp.dot(tril, g_mid[rs], preferred_element_type=F32)
               + jnp.dot(tril, g_lo[rs], preferred_element_type=F32))
        cums.append((cum, cum.T, g_all[rs].T))

    proj = {}
    dyn0 = jnp.minimum(c, 0)

    half = D_MODEL // 2

    def gate(b, part):
        lo = part * half
        gsc[dyn0 + b, :, lo:lo + half] = jnp.tanh(post(O_GATE + b * D_MODEL + lo, half))

    fillers = [
        lambda: proj.update(pu=pre(O_PU, BW)),
        lambda: proj.update(xbc=pre(O_XBC, CONV_DIM)),
        lambda: proj.update(q=pre(O_Q, BW)),
        lambda: proj.update(k=pre(O_K, BW) * (HD ** -0.5)),
        lambda: proj.update(v=pre(O_V, BW)),
        lambda: proj.update(zs=[post(O_Z + b * BW, BW) for b in range(4)]),
        lambda: proj.update(og=post(O_OG, BW)),
    ] + [functools.partial(gate, b, part) for b in range(4) for part in range(2)]
    y_s5 = _s5_chunk_scan(pre(O_SU, BW), fillers, lamp_ref, bbd_ref, cbd_ref, vec_ref, s5cr, s5ci)
    zs = proj["zs"]
    y_pool = pool_mix(proj["pu"])
    xbc = conv_mix(proj["xbc"])
    gates = [gsc.at[dyn0 + b] for b in range(4)]

    y_ssd_parts, h_m_parts = [], []
    for s in range(T // Q):
        y_c, h_c = _ssd_mlstm_chunk(
            xbc[s * Q:(s + 1) * Q], proj["q"][s * Q:(s + 1) * Q], proj["k"][s * Q:(s + 1) * Q],
            proj["v"][s * Q:(s + 1) * Q], sp_all[s * Q:(s + 1) * Q], g_all[s * Q:(s + 1) * Q],
            cums[s], causal, vec_ref, ssd_st, mc_st, nrow, mrow)
        y_ssd_parts.append(y_c)
        h_m_parts.append(h_c)
    y_ssd = jnp.concatenate(y_ssd_parts, axis=0)
    y_m = _sigmoid(proj["og"]) * jnp.concatenate(h_m_parts, axis=0)
    y_s5 = _s5_glu(y_s5, wglu_ref)

    z_acts = [_silu_of_half(z) for z in zs]
    y_ref[0] = _merge_tail(x, z_acts, [y_pool, y_ssd, y_m, y_s5], gates, wbr_ref, wout_ref, vec_ref)

    @pl.when(c == last)
    def _():
        pool_o[0] = poolh[POOL_PAD - POOL_BUF:POOL_PAD, :]
        conv_o[0] = convh[5:8, :]
        ssd_o[0] = ssd_st[...]
        mc_o[0] = mc_st[...]
        mn_o[0] = nrow[0:1, :]
        mm_o[0] = mrow[0:1, :]
        s5re_o[0] = s5cr[0:1, :]
        s5im_o[0] = s5ci[0:1, :]


def _ssd_mlstm_chunk(xbc, q, k, v, sp, g_blk, cums, causal, vec_ref, ssd_st, mc_st, nrow, mrow):
    T = xbc.shape[0]
    cum, cum_t, g_t = cums
    xs = xbc[:, 0:BW]
    dt_cols = [sp[:, h:h + 1] for h in range(HEADS)]
    acs_cols = [cum[:, h:h + 1] for h in range(HEADS)]
    acs_last = [cum[T - 1:T, h:h + 1] for h in range(HEADS)]
    xdt = xs * _expand_heads(dt_cols)
    xw_t = (xdt * _expand_heads([jnp.exp(acs_last[h] - acs_cols[h]) for h in range(HEADS)])).T
    d_ssd = vec_ref[V_MISC:V_MISC + 1, 256:512]
    m_row = mrow[0:1, :]
    n_row = nrow[0:1, :]
    hsl = [slice(h * HD, (h + 1) * HD) for h in range(HEADS)]
    b_gs = [xbc[:, BW + g * SSD_STATE:BW + (g + 1) * SSD_STATE] for g in range(2)]
    c_gs = [xbc[:, BW + 128 + g * SSD_STATE:BW + 128 + (g + 1) * SSD_STATE] for g in range(2)]

    cbs = [_dot_nt(c_gs[g], b_gs[g]) for g in range(2)]
    off_all = jnp.concatenate(
        [_dot_nt(c_gs[g], ssd_st[2 * g:2 * g + 2].reshape(2 * HD, SSD_STATE)) for g in range(2)], axis=1)
    qks = [_dot_nt(q[:, hsl[h]], k[:, hsl[h]]) for h in range(HEADS)]
    qcs = [_dot(q[:, hsl[h]], mc_st[h]) for h in range(HEADS)]
    gms = [cbs[h // 2] * jnp.exp(jnp.where(causal, acs_cols[h] - cum_t[h:h + 1, :], -jnp.inf))
           for h in range(HEADS)]
    b_cols = [cum[:, 8 + h:9 + h] for h in range(HEADS)]
    m_prevs = [m_row[:, h:h + 1] for h in range(HEADS)]
    dlogs = [jnp.where(causal, b_cols[h] - cum_t[8 + h:9 + h, :] + g_t[4 + h:5 + h, :], -jnp.inf)
             for h in range(HEADS)]
    inters = [b_cols[h] + m_prevs[h] for h in range(HEADS)]
    m_ts = [jnp.maximum(inters[h], jnp.max(dlogs[h], axis=1, keepdims=True)) for h in range(HEADS)]
    ss = [qks[h] * jnp.exp(dlogs[h] - m_ts[h]) for h in range(HEADS)]
    wis = [jnp.exp(inters[h] - m_ts[h]) for h in range(HEADS)]
    y_diag_all = jnp.concatenate([_dot(gms[h], xdt[:, hsl[h]]) for h in range(HEADS)], axis=1)
    svs = [_dot(ss[h], v[:, hsl[h]]) for h in range(HEADS)]
    y_off = off_all * _expand_heads([jnp.exp(acs_cols[h]) for h in range(HEADS)])
    y_ssd = y_diag_all + y_off + d_ssd * xs
    h_heads, wk_parts, decays, m_news = [], [], [], []
    for h in range(HEADS):
        num = svs[h] + wis[h] * qcs[h]
        den = (jnp.sum(ss[h], axis=1, keepdims=True)
               + wis[h] * jnp.sum(q[:, hsl[h]] * n_row[:, hsl[h]], axis=1, keepdims=True))
        h_heads.append(num / jnp.maximum(jnp.abs(den), jnp.exp(-m_ts[h])))
        m_new = m_ts[h][T - 1:T, :]
        b_last = b_cols[h][T - 1:T, :]
        wk_parts.append(jnp.exp(b_last - b_cols[h] + g_blk[:, 4 + h:5 + h] - m_new))
        decays.append(jnp.exp(b_last + m_prevs[h] - m_new))
        m_news.append(m_new)
    for h in range(HEADS):
        ssd_st[h] = jnp.exp(acs_last[h]) * ssd_st[h] + _dot(xw_t[hsl[h], :], b_gs[h // 2])
    lane_m = lax.broadcasted_iota(jnp.int32, (1, LANES), 1)
    m_row_new = jnp.zeros((1, LANES), F32)
    for h in range(HEADS):
        m_row_new = jnp.where(lane_m == h, jnp.broadcast_to(m_news[h], (1, LANES)), m_row_new)
    mrow[0:1, :] = m_row_new
    wk = k * _expand_heads(wk_parts)
    wk_t = wk.T
    for h in range(HEADS):
        mc_st[h] = decays[h] * mc_st[h] + _dot(wk_t[h * HD:(h + 1) * HD, :], v[:, h * HD:(h + 1) * HD])
    nrow[0:1, :] = _expand_heads(decays) * n_row + jnp.sum(wk, axis=0, keepdims=True)
    return y_ssd, jnp.concatenate(h_heads, axis=1)


def _s5_chunk_scan(su, fillers, lamp_ref, bbd_ref, cbd_ref, vec_ref, s5cr, s5ci):
    T = su.shape[0]
    bu = _dot(su, bbd_ref[...])
    p_re = lamp_ref[R_P:R_P + SUBLANES, :]
    p_im = lamp_ref[R_P + SUBLANES:R_P + 2 * SUBLANES, :]
    cr = s5cr[0:1, :]
    ci = s5ci[0:1, :]
    blk_re = []
    blk_im = []
    pieces = T // CHUNK
    nblk = CHUNK // SUBLANES
    slots = 4 * pieces
    filler_at = {}
    for i, f in enumerate(fillers):
        filler_at.setdefault((i * slots) // len(fillers), []).append(f)
    for piece in range(pieces):
        rows = slice(piece * CHUNK, (piece + 1) * CHUNK)
        h_re = bu[rows, 0:S5_W].reshape(nblk, SUBLANES, S5_W)
        h_im = bu[rows, S5_W:2 * S5_W].reshape(nblk, SUBLANES, S5_W)
        for i, d in enumerate((1, 2, 4)):
            base = R_M + 16 * i
            m_re = lamp_ref[base:base + SUBLANES, :]
            m_im = lamp_ref[base + SUBLANES:base + 2 * SUBLANES, :]
            r_re = pltpu.roll(h_re, d, 1)
            r_im = pltpu.roll(h_im, d, 1)
            h_re, h_im = h_re + (m_re * r_re - m_im * r_im), h_im + (m_re * r_im + m_im * r_re)
            for f in filler_at.get(4 * piece + i, ()):
                f()
        for j in range(nblk):
            br = h_re[j] + (p_re * cr - p_im * ci)
            bi = h_im[j] + (p_re * ci + p_im * cr)
            blk_re.append(br)
            blk_im.append(bi)
            cr = br[SUBLANES - 1:SUBLANES, :]
            ci = bi[SUBLANES - 1:SUBLANES, :]
        for f in filler_at.get(4 * piece + 3, ()):
            f()
    s5cr[0:1, :] = cr
    s5ci[0:1, :] = ci
    h_all = jnp.concatenate([jnp.concatenate(blk_re, axis=0), jnp.concatenate(blk_im, axis=0)], axis=1)
    return _dot(h_all, cbd_ref[...]) + vec_ref[V_MISC:V_MISC + 1, 512:768] * su


def _const_spec(shape, layer):
    nd = len(shape)
    return pl.BlockSpec((None,) + tuple(shape), lambda *_: (layer,) + (0,) * nd,
                        pipeline_mode=pl.Buffered(1))


def _w_spec(width, block, layer):
    return pl.BlockSpec((None, D_MODEL, width), lambda *_: (layer, 0, block), pipeline_mode=pl.Buffered(1))


def _prompt_layer(x, layer, p):
    bsz, seq, _ = x.shape
    T = TILE if seq % TILE == 0 else CHUNK
    nc = seq // T

    def bmap(nd):
        return lambda b, c: (b,) + (0,) * nd

    out_shape = (
        jax.ShapeDtypeStruct((bsz, seq, D_MODEL), F32),
        jax.ShapeDtypeStruct((bsz, POOL_BUF, BW), F32),
        jax.ShapeDtypeStruct((bsz, CONV_K - 1, CONV_DIM), F32),
        jax.ShapeDtypeStruct((bsz, HEADS, HD, SSD_STATE), F32),
        jax.ShapeDtypeStruct((bsz, HEADS, HD, HD), F32),
        jax.ShapeDtypeStruct((bsz, 1, BW), F32),
        jax.ShapeDtypeStruct((bsz, 1, LANES), F32),
        jax.ShapeDtypeStruct((bsz, 1, S5_W), F32),
        jax.ShapeDtypeStruct((bsz, 1, S5_W), F32),
    )
    out_specs = (
        pl.BlockSpec((1, T, D_MODEL), lambda b, c: (b, c, 0)),
        pl.BlockSpec((1, POOL_BUF, BW), bmap(2)),
        pl.BlockSpec((1, CONV_K - 1, CONV_DIM), bmap(2)),
        pl.BlockSpec((1, HEADS, HD, SSD_STATE), bmap(3)),
        pl.BlockSpec((1, HEADS, HD, HD), bmap(3)),
        pl.BlockSpec((1, 1, BW), bmap(2)),
        pl.BlockSpec((1, 1, LANES), bmap(2)),
        pl.BlockSpec((1, 1, S5_W), bmap(2)),
        pl.BlockSpec((1, 1, S5_W), bmap(2)),
    )
    in_specs = [
        pl.BlockSpec((1, T, D_MODEL), lambda b, c: (b, c, 0)),
        _w_spec(PRE_W, PRE_BLOCK, layer),
        _w_spec(POST_W, 0, layer),
        _const_spec((D_MODEL, LANES), layer),
        _const_spec((16, D_MODEL), layer),
        _const_spec((BW, BW), layer),
        _const_spec((LAMP_ROWS, S5_W), layer),
        _const_spec((BW, 2 * S5_W), layer),
        _const_spec((2 * S5_W, BW), layer),
        _const_spec((BW, 2 * BW), layer),
        _const_spec((4, BW, D_MODEL), layer),
        _const_spec((D_MODEL, D_MODEL), layer),
    ]
    scratch = [
        pltpu.VMEM((T + POOL_PAD, BW), F32),
        pltpu.VMEM((3, T + POOL_PAD, BW), F32),
        pltpu.VMEM((T + SUBLANES, CONV_DIM), F32),
        pltpu.VMEM((HEADS, HD, SSD_STATE), F32),
        pltpu.VMEM((HEADS, HD, HD), F32),
        pltpu.VMEM((SUBLANES, BW), F32),
        pltpu.VMEM((SUBLANES, LANES), F32),
        pltpu.VMEM((SUBLANES, S5_W), F32),
        pltpu.VMEM((SUBLANES, S5_W), F32),
        pltpu.VMEM((4, T, D_MODEL), F32),
    ]
    return pl.pallas_call(
        _prompt_kernel,
        grid=(bsz, nc),
        in_specs=in_specs,
        out_specs=out_specs,
        out_shape=out_shape,
        scratch_shapes=scratch,
        compiler_params=pltpu.CompilerParams(dimension_semantics=("arbitrary", "arbitrary"),
                                             vmem_limit_bytes=VMEM_LIMIT),
        name="prompt_layer",
    )(x, p["wall"], p["wall"], p["wsm"], p["vec"], p["wpool"], p["lamp"], p["bbd"], p["cbd"],
      p["wglu"], p["wbr"], p["wout"])


RW_BC, RW_XDT, RW_Q, RW_WK, RW_V, RW_SDEC, RW_MDEC, RW_W = 0, 256, 512, 768, 1024, 1280, 1408, 1536
MID_W = 7 * BW


def _sample_pre_kernel(x_ref, wpre_ref, wsm_ref, vec_ref, wpool_ref, lamp_ref, bbd_ref, cbd_ref,
                       pool_ref, conv_ref, n_ref, m_ref, s5re_ref, s5im_ref,
                       rows_o, mid_o, pool_o, conv_o, n_o, m_o, s5re_o, s5im_o):
    x = x_ref[...]
    nb = x.shape[0]
    pp = _dot(x, wpre_ref[...])
    pu = pp[:, O_PU:O_PU + BW]
    xbc_raw = pp[:, O_XBC:O_XBC + CONV_DIM]
    q = pp[:, O_Q:O_Q + BW]
    k = pp[:, O_K:O_K + BW] * (HD ** -0.5)
    v = pp[:, O_V:O_V + BW]
    su = pp[:, O_SU:O_SU + BW]

    def ext(j):
        return pool_ref[j]

    w2 = pu + ext(14)
    w4 = w2 + ext(13) + ext(12)
    w8 = w4
    for j in range(11, 7, -1):
        w8 = w8 + ext(j)
    w16 = w8
    for j in range(7, -1, -1):
        w16 = w16 + ext(j)
    cnt = jnp.minimum(PAST_LEN + 1, _pool_window_row()).astype(F32)
    dpool = _pool_select(w2, w4, w8, w16) / cnt - pu
    y_pool = _dot(dpool, wpool_ref[...]) * vec_ref[V_MISC:V_MISC + 1, 0:256]
    for j in range(POOL_BUF - 1):
        pool_o[j] = pool_ref[j + 1]
    pool_o[POOL_BUF - 1] = pu

    acc = vec_ref[V_CONVB:V_CONVB + 1, 0:CONV_DIM]
    for kk in range(CONV_K - 1):
        acc = acc + conv_ref[kk] * vec_ref[V_CONVW + kk:V_CONVW + kk + 1, 0:CONV_DIM]
    acc = acc + xbc_raw * vec_ref[V_CONVW + CONV_K - 1:V_CONVW + CONV_K, 0:CONV_DIM]
    for kk in range(CONV_K - 2):
        conv_o[kk] = conv_ref[kk + 1]
    conv_o[CONV_K - 2] = xbc_raw
    xbc = _silu(acc)
    xs = xbc[:, 0:BW]
    bc = xbc[:, BW:2 * BW]

    v_sm, sp, lsg, a_row = _small_block(_dot(x, wsm_ref[...]), vec_ref)
    sdec_blk = jnp.exp(sp * a_row)
    dt_cols = [sp[:, h:h + 1] for h in range(HEADS)]
    xdt = xs * _expand_heads(dt_cols)
    cb_cols = []
    for g in range(2):
        b_g = bc[:, g * SSD_STATE:(g + 1) * SSD_STATE]
        c_g = bc[:, 128 + g * SSD_STATE:128 + (g + 1) * SSD_STATE]
        cb_cols.append(jnp.sum(c_g * b_g, axis=1, keepdims=True))
    yssd_part = (_expand_heads([cb_cols[h // 2] for h in range(HEADS)]) * xdt
                 + vec_ref[V_MISC:V_MISC + 1, 256:512] * xs)
    sdec_exp = _expand_heads([sdec_blk[:, h:h + 1] for h in range(HEADS)])

    n0 = n_ref[...].T
    m0 = jnp.concatenate([m_ref[...], jnp.zeros((SUBLANES - HEADS, nb), F32)], axis=0).T
    s_cols, wi_cols, dw_cols, dmax_cols, m_cols = [], [], [], [], []
    for h in range(HEADS):
        sl = slice(h * HD, (h + 1) * HD)
        ig = v_sm[:, 4 + h:5 + h]
        lf = lsg[:, 8 + h:9 + h]
        inter = lf + m0[:, h:h + 1]
        m_t = jnp.maximum(inter, (lf - lf) + ig)
        dw = jnp.exp((lf - lf) + ig - m_t)
        wi = jnp.exp(inter - m_t)
        s = jnp.sum(q[:, sl] * k[:, sl], axis=1, keepdims=True) * dw
        den = s + wi * jnp.sum(q[:, sl] * n0[:, sl], axis=1, keepdims=True)
        s_cols.append(s)
        wi_cols.append(wi)
        dw_cols.append(dw)
        dmax_cols.append(jnp.maximum(jnp.abs(den), jnp.exp(-m_t)))
        m_cols.append(m_t)
    wi_exp = _expand_heads(wi_cols)
    wk = _expand_heads(dw_cols) * k
    n_o[...] = (wi_exp * n0 + wk).T
    lane = lax.broadcasted_iota(jnp.int32, (nb, LANES), 1)
    m_blk = jnp.zeros((nb, LANES), F32)
    wi_blk = jnp.zeros((nb, LANES), F32)
    for h in range(HEADS):
        m_blk = jnp.where(lane == h, jnp.broadcast_to(m_cols[h], (nb, LANES)), m_blk)
        wi_blk = jnp.where(lane == h, jnp.broadcast_to(wi_cols[h], (nb, LANES)), wi_blk)
    m_o[...] = m_blk.T[0:HEADS, :]

    bu = _dot(su, bbd_ref[...])
    lbr = lamp_ref[R_LAM:R_LAM + 1, :]
    lbi = lamp_ref[R_LAM + 1:R_LAM + 2, :]
    h0r = s5re_ref[...].T
    h0i = s5im_ref[...].T
    h_re = bu[:, 0:S5_W] + (lbr * h0r - lbi * h0i)
    h_im = bu[:, S5_W:2 * S5_W] + (lbr * h0i + lbi * h0r)
    s5re_o[...] = h_re.T
    s5im_o[...] = h_im.T
    y_s5 = _dot(jnp.concatenate([h_re, h_im], axis=1), cbd_ref[...]) + vec_ref[V_MISC:V_MISC + 1, 512:768] * su

    rows_o[...] = jnp.concatenate([bc, xdt, q, wk, v, sdec_blk, wi_blk], axis=1).T
    mid_o[...] = jnp.concatenate([y_pool, yssd_part, sdec_exp, _expand_heads(s_cols) * v, wi_exp,
                                  _expand_heads(dmax_cols), y_s5], axis=1)


def _sample_pre(x, layer, p, pool2, conv2, n2, m2, s5re2, s5im2):
    nb = x.shape[0]
    full = lambda a: pl.BlockSpec(a.shape, lambda *_: (0,) * a.ndim)
    of_layer = lambda a: pl.BlockSpec((None,) + a.shape[1:], lambda *_: (layer,) + (0,) * (a.ndim - 1))
    out_shape = (
        jax.ShapeDtypeStruct((RW_W, nb), F32),
        jax.ShapeDtypeStruct((nb, MID_W), F32),
        jax.ShapeDtypeStruct(pool2.shape[1:], F32),
        jax.ShapeDtypeStruct(conv2.shape[1:], F32),
        jax.ShapeDtypeStruct((BW, nb), F32),
        jax.ShapeDtypeStruct((HEADS, nb), F32),
        jax.ShapeDtypeStruct((S5_W, nb), F32),
        jax.ShapeDtypeStruct((S5_W, nb), F32),
    )
    in_specs = [
        full(x),
        _w_spec(PRE_W, PRE_BLOCK, layer),
        _const_spec((D_MODEL, LANES), layer),
        _const_spec((16, D_MODEL), layer),
        _const_spec((BW, BW), layer),
        _const_spec((LAMP_ROWS, S5_W), layer),
        _const_spec((BW, 2 * S5_W), layer),
        _const_spec((2 * S5_W, BW), layer),
        of_layer(pool2), of_layer(conv2), of_layer(n2), of_layer(m2), of_layer(s5re2), of_layer(s5im2),
    ]
    return pl.pallas_call(
        _sample_pre_kernel,
        grid=(1,),
        in_specs=in_specs,
        out_specs=tuple(pl.BlockSpec(s.shape, lambda *_, nd=len(s.shape): (0,) * nd) for s in out_shape),
        out_shape=out_shape,
        compiler_params=pltpu.CompilerParams(dimension_semantics=("arbitrary",),
                                             vmem_limit_bytes=VMEM_LIMIT),
        name="sample_pre",
    )(x, p["wall"], p["wsm"], p["vec"], p["wpool"], p["lamp"], p["bbd"], p["cbd"],
      pool2, conv2, n2, m2, s5re2, s5im2)


def _sample_state_kernel(first_layer, rows_ref, ssd_ref, mc_ref, *rest):
    h = pl.program_id(0)
    part = pl.program_id(1)
    if first_layer:
        yoff_ref, qc_ref, ssd_o, mc_o = rest
        ssd_o[1] = ssd_ref[1]
        mc_o[1] = mc_ref[1]
        ssd_in, mc_in, ssd_out, mc_out = ssd_ref.at[0], mc_ref.at[0], ssd_o.at[0], mc_o.at[0]
    else:
        _, _, yoff_ref, qc_ref, ssd_out, mc_out = rest
        ssd_in, mc_in = ssd_ref, mc_ref
    g64 = pl.multiple_of((h // 2) * SSD_STATE, SSD_STATE)
    h64 = pl.multiple_of(h * HD, HD)
    b_t = rows_ref[pl.ds(RW_BC + g64, SSD_STATE), :]
    c_t = rows_ref[pl.ds(RW_BC + 128 + g64, SSD_STATE), :]
    v_t = rows_ref[pl.ds(RW_V + h64, HD), :]
    sdec = rows_ref[pl.ds(RW_SDEC + h, 1), :]
    mdec = rows_ref[pl.ds(RW_MDEC + h, 1), :]
    base = h64 + part * STATE_ROWS

    def body(i, acc):
        hs = ssd_in[i]
        yoff_ref[pl.ds(i, 1), :] = jnp.sum(hs * c_t, axis=0, keepdims=True)
        ssd_out[i] = sdec * hs + rows_ref[pl.ds(RW_XDT + base + i, 1), :] * b_t
        cs = mc_in[i]
        mc_out[i] = mdec * cs + rows_ref[pl.ds(RW_WK + base + i, 1), :] * v_t
        return acc + rows_ref[pl.ds(RW_Q + base + i, 1), :] * cs

    acc = lax.fori_loop(0, STATE_ROWS, body, jnp.zeros((HD, rows_ref.shape[1]), F32), unroll=4)

    @pl.when(part == 0)
    def _():
        qc_ref[...] = acc

    @pl.when(part != 0)
    def _():
        qc_ref[...] = qc_ref[...] + acc


STATE_ROWS = 32


def _sample_state(rows_t, layer, ssd_all, mc_all, prev=None):
    nb = rows_t.shape[1]
    depth = ssd_all.shape[0]
    parts = HD // STATE_ROWS
    tile = (STATE_ROWS, HD, nb)
    rows_spec = pl.BlockSpec(rows_t.shape, lambda h, s: (0, 0))
    yoff_spec = pl.BlockSpec((STATE_ROWS, nb), lambda h, s: (h * parts + s, 0))
    qc_spec = pl.BlockSpec((HD, nb), lambda h, s: (h, 0))
    out_shape = (jax.ShapeDtypeStruct((BW, nb), F32),
                 jax.ShapeDtypeStruct((BW, nb), F32),
                 jax.ShapeDtypeStruct(ssd_all.shape, F32),
                 jax.ShapeDtypeStruct(mc_all.shape, F32))
    params = pltpu.CompilerParams(dimension_semantics=("arbitrary", "arbitrary"),
                                  vmem_limit_bytes=VMEM_LIMIT)
    if prev is None:
        assert layer == 0 and depth == 2
        both = pl.BlockSpec((depth, None) + tile, lambda h, s: (0, h, s, 0, 0))
        return pl.pallas_call(
            functools.partial(_sample_state_kernel, True),
            grid=(HEADS, parts),
            in_specs=[rows_spec, both, both],
            out_specs=(yoff_spec, qc_spec, both, both),
            out_shape=out_shape,
            compiler_params=params,
            name="sample_state_first",
        )(rows_t, ssd_all, mc_all)
    one = pl.BlockSpec((None, None) + tile, lambda h, s: (layer, h, s, 0, 0))
    anywhere = pl.BlockSpec(memory_space=pl.ANY)
    return pl.pallas_call(
        functools.partial(_sample_state_kernel, False),
        grid=(HEADS, parts),
        in_specs=[rows_spec, one, one, anywhere, anywhere],
        out_specs=(yoff_spec, qc_spec, one, one),
        out_shape=out_shape,
        input_output_aliases={3: 2, 4: 3},
        compiler_params=params,
        name="sample_state_next",
    )(rows_t, ssd_all, mc_all, prev[0], prev[1])


def _sample_post_kernel(x_ref, wpost_ref, vec_ref, mid_ref, yoff_ref, qc_ref, wglu_ref, wbr_ref, wout_ref, y_ref):
    x = x_ref[...]
    xb = x.astype(BF16)

    def post(lo, width):
        return jnp.dot(xb, wpost_ref[:, lo:lo + width], preferred_element_type=F32)

    def mid(i):
        return mid_ref[:, i * BW:(i + 1) * BW]

    y_pool, yssd_part, sdec_exp, sv, wi_exp, dmax_exp, y_s5 = [mid(i) for i in range(7)]
    y_ssd = yssd_part + yoff_ref[...].T * sdec_exp
    hh = (sv + wi_exp * qc_ref[...].T) / dmax_exp
    y_m = _sigmoid(post(O_OG, BW)) * hh
    y_s5 = _s5_glu(y_s5, wglu_ref)
    z_acts = [_silu_of_half(post(O_Z + b * BW, BW)) for b in range(4)]
    gates = [jnp.tanh(post(O_GATE + b * D_MODEL, D_MODEL)) for b in range(4)]
    y_ref[...] = _merge_tail(x, z_acts, [y_pool, y_ssd, y_m, y_s5], gates, wbr_ref, wout_ref, vec_ref)


def _sample_post(x, layer, p, mid, yoff_t, qc_t):
    nb = x.shape[0]
    full = lambda a: pl.BlockSpec(a.shape, lambda *_: (0,) * a.ndim)
    return pl.pallas_call(
        _sample_post_kernel,
        grid=(1,),
        in_specs=[full(x), _w_spec(POST_W, 0, layer), _const_spec((16, D_MODEL), layer),
                  full(mid), full(yoff_t), full(qc_t), _const_spec((BW, 2 * BW), layer),
                  _const_spec((4, BW, D_MODEL), layer), _const_spec((D_MODEL, D_MODEL), layer)],
        out_specs=pl.BlockSpec((nb, D_MODEL), lambda *_: (0, 0)),
        out_shape=jax.ShapeDtypeStruct((nb, D_MODEL), F32),
        compiler_params=pltpu.CompilerParams(dimension_semantics=("arbitrary",),
                                             vmem_limit_bytes=VMEM_LIMIT),
        name="sample_post",
    )(x, p["wall"], p["vec"], mid, yoff_t, qc_t, p["wglu"], p["wbr"], p["wout"])


def _w_permute_kernel(src_ref, halved_ref, w_hbm, out_ref, buf, sem):
    i = pl.program_id(0)
    depth = out_ref.shape[0]

    def copies(step, slot):
        start = src_ref[step]
        return [pltpu.make_async_copy(w_hbm.at[pl.ds(start, WCOLS), l, :], buf.at[slot, l], sem.at[slot, l])
                for l in range(depth)]

    @pl.when(i == 0)
    def _():
        for cp in copies(0, 0):
            cp.start()

    @pl.when(i + 1 < pl.num_programs(0))
    def _():
        for cp in copies(i + 1, (i + 1) % 2):
            cp.start()

    for cp in copies(i, i % 2):
        cp.wait()
    scale = jnp.where(halved_ref[i] == 1, 0.5, 1.0).astype(F32)
    for l in range(depth):
        out_ref[l] = (buf[i % 2, l].T * scale).astype(BF16)


def _w_permute(w_in):
    depth, rows, _ = w_in.shape
    w_t = jnp.transpose(w_in, (2, 0, 1))
    nblk = len(W_SRC)
    return pl.pallas_call(
        _w_permute_kernel,
        grid_spec=pltpu.PrefetchScalarGridSpec(
            num_scalar_prefetch=2, grid=(nblk,),
            in_specs=[pl.BlockSpec(memory_space=pl.ANY)],
            out_specs=pl.BlockSpec((depth, rows, WCOLS), lambda i, src, halved: (0, 0, i)),
            scratch_shapes=[pltpu.VMEM((2, depth, WCOLS, rows), F32), pltpu.SemaphoreType.DMA((2, depth))]),
        out_shape=jax.ShapeDtypeStruct((depth, rows, nblk * WCOLS), BF16),
        compiler_params=pltpu.CompilerParams(dimension_semantics=("arbitrary",)),
        name="w_permute",
    )(jnp.asarray(W_SRC, jnp.int32), jnp.asarray(W_HALVED, jnp.int32), w_t)


def _prepare_params(w_in, w_pool, pool_scale, conv_w, conv_b, dt_bias, a_log, d_ssd, ig_bias, fg_bias,
                    lam_re, lam_im, b_re, b_im, c_re, c_im, log_dt, d_s5, w_glu, w_br, w_out, ln_g, ln_b):
    depth = w_in.shape[0]
    wall = _w_permute(w_in)
    small_cols = [w_in[:, :, lo:lo + n] for lo, n in SMALL_SRC]
    wsm = jnp.concatenate(small_cols + [jnp.zeros((depth, D_MODEL, LANES - 12), w_in.dtype)], axis=2).astype(BF16)

    zeros = lambda n: jnp.zeros((depth, n), F32)
    misc = jnp.concatenate([pool_scale, jnp.repeat(d_ssd, HD, axis=1), d_s5,
                            dt_bias, ig_bias, fg_bias, zeros(LANES - 12),
                            a_log, zeros(LANES - HEADS)], axis=1)
    pad512 = lambda a: jnp.concatenate([a, jnp.zeros(a.shape[:-1] + (D_MODEL - CONV_DIM,), F32)], axis=-1)
    vec = jnp.concatenate([ln_g[:, None], ln_b[:, None], misc[:, None], pad512(conv_b)[:, None],
                           pad512(conv_w), jnp.zeros((depth, 16 - 4 - CONV_K, D_MODEL), F32)], axis=1)

    eye4 = jnp.eye(len(POOL_WINDOWS), dtype=F32)
    wpool = jnp.einsum("lgce,gh->lgche", w_pool, eye4).reshape(depth, BW, BW).astype(BF16)

    lam2 = jnp.stack([lam_re.reshape(depth, S5_W), lam_im.reshape(depth, S5_W)], axis=1)
    ldt = jnp.repeat(log_dt, S5_STATE, axis=1)[:, None]
    to_rows = lambda b: jnp.transpose(b, (0, 3, 1, 2)).reshape(depth, S5_CH, S5_W)
    bcat = jnp.concatenate([to_rows(b_re), to_rows(b_im)], axis=1)

    eye_g = jnp.eye(S5_GROUPS, dtype=F32)
    cb = lambda cm: jnp.einsum("lgcp,gh->lgphc", cm, eye_g).reshape(depth, S5_W, BW)
    cbd = jnp.concatenate([cb(c_re), -cb(c_im)], axis=1).astype(BF16)

    lamp, bbd = [], []
    for l in range(depth):
        lp, bb = _s5_prep(lam2[l], ldt[l], bcat[l])
        lamp.append(lp)
        bbd.append(bb)
    return dict(wall=wall, wsm=wsm, vec=vec, wpool=wpool, lamp=jnp.stack(lamp), bbd=jnp.stack(bbd),
                cbd=cbd, wglu=w_glu.astype(BF16), wbr=w_br.astype(BF16), wout=(0.5 * w_out).astype(BF16))


def kernel(x_prompt, x_sample, state_pool, state_ssd_conv, state_ssd, state_mlstm_C, state_mlstm_n, state_mlstm_m, state_s5_re, state_s5_im, w_in, w_pool, pool_scale, conv_w, conv_b, dt_bias, a_log, d_ssd, ig_bias, fg_bias, lam_re, lam_im, b_re, b_im, c_re, c_im, log_dt, d_s5, w_glu, w_br, w_out, ln_g, ln_b):
    p = _prepare_params(w_in, w_pool, pool_scale, conv_w, conv_b, dt_bias, a_log, d_ssd, ig_bias, fg_bias,
                        lam_re, lam_im, b_re, b_im, c_re, c_im, log_dt, d_s5, w_glu, w_br, w_out, ln_g, ln_b)
    depth = w_in.shape[0]
    bsz = x_prompt.shape[0]
    nb = x_sample.shape[0]
    yp = x_prompt
    ys = x_sample.reshape(nb, D_MODEL)
    outs_p = [[] for _ in range(8)]
    outs_s = [[] for _ in range(8)]
    mats_s = None
    pool_t = jnp.transpose(state_pool, (0, 2, 1, 3))
    conv_t = jnp.transpose(state_ssd_conv, (0, 2, 1, 3))
    ssd_t = jnp.transpose(state_ssd, (0, 2, 3, 4, 1))
    mc_t = jnp.transpose(state_mlstm_C, (0, 2, 3, 4, 1))
    n_t = jnp.transpose(state_mlstm_n, (0, 2, 3, 1)).reshape(depth, BW, nb)
    m_t = jnp.transpose(state_mlstm_m, (0, 2, 1))
    re_t = jnp.transpose(state_s5_re, (0, 2, 3, 1)).reshape(depth, S5_W, nb)
    im_t = jnp.transpose(state_s5_im, (0, 2, 3, 1)).reshape(depth, S5_W, nb)
    for l in range(depth):
        yp, pool_p, conv_p, ssd_p, mc_p, mn_p, mm_p, re_p, im_p = _prompt_layer(yp, l, p)
        new_p = (pool_p, conv_p, ssd_p, mc_p, mn_p.reshape(bsz, HEADS, HD), mm_p[:, 0, :HEADS],
                 re_p.reshape(bsz, S5_GROUPS, S5_STATE), im_p.reshape(bsz, S5_GROUPS, S5_STATE))

        rows_t, mid, pool_s, conv_s, n_s, m_s, re_s, im_s = _sample_pre(
            ys, l, p, pool_t, conv_t, n_t, m_t, re_t, im_t)
        yoff_t, qc_t, ssd_s, mc_s = _sample_state(rows_t, l, ssd_t, mc_t, mats_s)
        mats_s = (ssd_s, mc_s)
        ys = _sample_post(ys, l, p, mid, yoff_t, qc_t)
        new_s = (pool_s, conv_s, None, None, n_s, m_s, re_s, im_s)
        for i in range(8):
            outs_p[i].append(new_p[i])
            outs_s[i].append(new_s[i])
    sp = [jnp.stack(o, axis=0) for o in outs_p]
    st = [mats_s[i - 2] if i in (2, 3) else jnp.stack(outs_s[i], axis=0) for i in range(8)]
    ss = [jnp.transpose(st[0], (0, 2, 1, 3)), jnp.transpose(st[1], (0, 2, 1, 3)),
          jnp.transpose(st[2], (0, 4, 1, 2, 3)), jnp.transpose(st[3], (0, 4, 1, 2, 3)),
          jnp.transpose(st[4].reshape(depth, HEADS, HD, nb), (0, 3, 1, 2)),
          jnp.transpose(st[5], (0, 2, 1)),
          jnp.transpose(st[6].reshape(depth, S5_GROUPS, S5_STATE, nb), (0, 3, 1, 2)),
          jnp.transpose(st[7].reshape(depth, S5_GROUPS, S5_STATE, nb), (0, 3, 1, 2))]
    out = [yp, ys.reshape(nb, 1, D_MODEL)]
    for i in range(8):
        out.append(sp[i])
        out.append(ss[i])
    return tuple(out)
```

```python
import functools

import jax
import jax.numpy as jnp
from jax import lax
from jax.experimental import pallas as pl
from jax.experimental.pallas import tpu as pltpu

F32 = jnp.float32
BF16 = jnp.bfloat16

D_MODEL = 1024
DEPTH = 2
PAST_LEN = 16384
BW = 256
POOL_WINDOWS = (2, 4, 8, 16)
POOL_BUF = 15
POOL_PAD = 32
HEADS = 4
HD = 64
SSD_STATE = 64
CONV_K = 4
CONV_DIM = 512
S5_GROUPS = 16
S5_CH = 16
S5_STATE = 64
S5_W = S5_GROUPS * S5_STATE
CHUNK = 128
TILE = 512
ALPHA = (2.0 * DEPTH) ** 0.25
LN_EPS = 1e-5

SUBLANES = 8
LANES = 128
VMEM_LIMIT = 62 * 1024 * 1024

PRE_W = 1792
POST_W = 5376
O_PU, O_XBC, O_Q, O_K, O_V, O_SU = 0, 256, 768, 1024, 1280, 1536
O_Z, O_OG, O_GATE = 0, 1024, 1280
PRE_BLOCK = POST_W // PRE_W
assert POST_W == PRE_BLOCK * PRE_W
WCOLS = 256
_SRC_POST = [256, 1028, 2316, 2828, 2060] + [3084 + WCOLS * i for i in range(16)]
_SRC_PRE = [0, 512, 768, 1284, 1540, 1796, 2572]
W_SRC = tuple(_SRC_POST + _SRC_PRE)
W_HALVED = tuple([1, 1, 1, 1, 0] + [1] * 16 + [0] * 7)
SMALL_SRC = ((1024, 4), (2052, 4), (2056, 4))

R_LAM = 0
R_P = 8
R_M = 24
LAMP_ROWS = R_M + 3 * 16

V_LNG, V_LNB, V_MISC, V_CONVB, V_CONVW = 0, 1, 2, 3, 4


def _dot(a, b):
    return jnp.dot(a.astype(BF16), b.astype(BF16), preferred_element_type=F32)


def _dot_nt(a, b):
    return lax.dot_general(a.astype(BF16), b.astype(BF16), (((1,), (1,)), ((), ())),
                           preferred_element_type=F32)


def _softplus(x):
    return jnp.logaddexp(x, 0.0)


def _sigmoid(x):
    return 0.5 * jnp.tanh(0.5 * x) + 0.5


def _silu(x):
    return x * _sigmoid(x)


def _silu_of_half(xh):
    return xh * (jnp.tanh(xh) + 1.0)


def _expand_heads(cols, width=HD):
    rows = cols[0].shape[0]
    lane = lax.broadcasted_iota(jnp.int32, (rows, HEADS * width), 1)
    out = jnp.broadcast_to(cols[HEADS - 1], (rows, HEADS * width))
    for h in range(HEADS - 2, -1, -1):
        out = jnp.where(lane < (h + 1) * width, jnp.broadcast_to(cols[h], (rows, HEADS * width)), out)
    return out


def _pool_select(w2, w4, w8, w16):
    lane = lax.broadcasted_iota(jnp.int32, w2.shape, 1)
    return jnp.where(lane < 64, w2, jnp.where(lane < 128, w4, jnp.where(lane < 192, w8, w16)))


def _pool_window_row():
    lane = lax.broadcasted_iota(jnp.int32, (1, BW), 1)
    return jnp.where(lane < 64, 2, jnp.where(lane < 128, 4, jnp.where(lane < 192, 8, 16)))


def _small_block(sm, vec_ref):
    bias = vec_ref[V_MISC:V_MISC + 1, 768:896]
    alog = vec_ref[V_MISC:V_MISC + 1, 896:1024]
    v = sm + bias
    sp = _softplus(v)
    lsg = -_softplus(-v)
    a_row = -jnp.exp(alog)
    return v, sp, lsg, a_row


def _merge_tail(x, z_acts, ys, gate_tanh, wbr_ref, wout_ref, vec_ref):
    merged = None
    for b in range(4):
        br = ys[b] * z_acts[b]
        pb = _dot(br, wbr_ref[b])
        gb = gate_tanh[b][...] + 1.0
        merged = gb * pb if merged is None else merged + gb * pb
    out = _dot(merged, wout_ref[...])
    r = ALPHA * x + out
    mu = jnp.mean(r, axis=-1, keepdims=True)
    var = jnp.mean(jnp.square(r - mu), axis=-1, keepdims=True)
    g = vec_ref[V_LNG:V_LNG + 1, :]
    b = vec_ref[V_LNB:V_LNB + 1, :]
    return (r - mu) * lax.rsqrt(var + LN_EPS) * g + b


def _s5_glu(y_s5, wglu_ref):
    glu = _dot(jax.nn.gelu(y_s5), wglu_ref[...])
    return glu[:, :BW] * _sigmoid(glu[:, BW:])


def _s5_prep_kernel(lam_ref, ldt_ref, b_ref, lamp_ref, bbd_ref):
    lr = lam_ref[0:1, :]
    li = lam_ref[1:2, :]
    step = jnp.exp(ldt_ref[0:1, :])
    e = jnp.exp(lr * step)
    lbr = e * jnp.cos(li * step)
    lbi = e * jnp.sin(li * step)
    den = lr * lr + li * li
    nr = lbr - 1.0
    qr = (nr * lr + lbi * li) / den
    qi = (lbi * lr - nr * li) / den
    bre = b_ref[0:S5_CH, :]
    bim = b_ref[S5_CH:2 * S5_CH, :]
    bbr = qr * bre - qi * bim
    bbi = qr * bim + qi * bre
    for part, (vr, vi) in enumerate(((bbr, bbi), (lbr * bbr - lbi * bbi, lbr * bbi + lbi * bbr))):
        bb = jnp.concatenate([vr, vi], axis=1)
        tiled = jnp.concatenate([bb] * S5_GROUPS, axis=0)
        rowg = lax.broadcasted_iota(jnp.int32, tiled.shape, 0) // S5_CH
        colg = (lax.broadcasted_iota(jnp.int32, tiled.shape, 1) % S5_W) // S5_STATE
        bbd_ref[part * BW:(part + 1) * BW, :] = jnp.where(rowg == colg, tiled, 0.0).astype(BF16)

    pows = [(lbr, lbi)]
    for _ in range(7):
        pr, pi = pows[-1]
        pows.append((pr * lbr - pi * lbi, pr * lbi + pi * lbr))
    row = lax.broadcasted_iota(jnp.int32, (SUBLANES, S5_W), 0)

    def bcast(v):
        return jnp.broadcast_to(v, (SUBLANES, S5_W))

    p_re = bcast(pows[7][0])
    p_im = bcast(pows[7][1])
    for j in range(6, -1, -1):
        p_re = jnp.where(row == j, bcast(pows[j][0]), p_re)
        p_im = jnp.where(row == j, bcast(pows[j][1]), p_im)
    lamp_ref[R_LAM:R_LAM + SUBLANES, :] = jnp.where(row == 0, bcast(lbr), jnp.where(row == 1, bcast(lbi), 0.0))
    lamp_ref[R_P:R_P + SUBLANES, :] = p_re
    lamp_ref[R_P + SUBLANES:R_P + 2 * SUBLANES, :] = p_im
    for i, d in enumerate((1, 2, 4)):
        base = R_M + 16 * i
        lamp_ref[base:base + SUBLANES, :] = jnp.where(row >= d, bcast(pows[d - 1][0]), 0.0)
        lamp_ref[base + SUBLANES:base + 2 * SUBLANES, :] = jnp.where(row >= d, bcast(pows[d - 1][1]), 0.0)


def _s5_prep(lam2, ldt, bcat):
    return pl.pallas_call(
        _s5_prep_kernel,
        out_shape=(jax.ShapeDtypeStruct((LAMP_ROWS, S5_W), F32),
                   jax.ShapeDtypeStruct((2 * BW, 2 * S5_W), BF16)),
        name="s5_prep",
    )(lam2, ldt, bcat)


def _prompt_kernel(x_ref, wpre_ref, wpost_ref, wsm_ref, vec_ref, wpool_ref, lamp_ref, bbd_ref, cbd_ref,
                   wglu_ref, wbr_ref, wout_ref,
                   y_ref, pool_o, conv_o, ssd_o, mc_o, mn_o, mm_o, s5re_o, s5im_o,
                   poolh, poolw, convh, ssd_st, mc_st, nrow, mrow, s5cr, s5ci, gsc):
    T = x_ref.shape[1]
    Q = CHUNK
    c = pl.program_id(1)
    last = pl.num_programs(1) - 1

    @pl.when(c == 0)
    def _():
        poolh[0:POOL_PAD, :] = jnp.zeros((POOL_PAD, BW), F32)
        convh[0:SUBLANES, :] = jnp.zeros((SUBLANES, CONV_DIM), F32)
        ssd_st[...] = jnp.zeros(ssd_st.shape, F32)
        mc_st[...] = jnp.zeros(mc_st.shape, F32)
        nrow[...] = jnp.zeros(nrow.shape, F32)
        mrow[...] = jnp.zeros(mrow.shape, F32)
        s5cr[...] = jnp.zeros(s5cr.shape, F32)
        s5ci[...] = jnp.zeros(s5ci.shape, F32)

    x = x_ref[0]
    xb = x.astype(BF16)

    def pre(lo, width):
        return jnp.dot(xb, wpre_ref[:, lo:lo + width], preferred_element_type=F32)

    def post(lo, width):
        return jnp.dot(xb, wpost_ref[:, lo:lo + width], preferred_element_type=F32)

    rows_i = lax.broadcasted_iota(jnp.int32, (Q, Q), 0)
    cols_i = lax.broadcasted_iota(jnp.int32, (Q, Q), 1)
    causal = rows_i >= cols_i

    n_rows = POOL_PAD + T

    def pool_mix(pu):
        poolh[POOL_PAD:n_rows, :] = pu
        poolw[0, 8:n_rows, :] = poolh[8:n_rows, :] + poolh[7:n_rows - 1, :]
        poolw[1, 16:n_rows, :] = poolw[0, 16:n_rows, :] + poolw[0, 14:n_rows - 2, :]
        poolw[2, 24:n_rows, :] = poolw[1, 24:n_rows, :] + poolw[1, 20:n_rows - 4, :]
        w2 = poolw[0, POOL_PAD:n_rows, :]
        w4 = poolw[1, POOL_PAD:n_rows, :]
        w8 = poolw[2, POOL_PAD:n_rows, :]
        w16 = w8 + poolw[2, POOL_PAD - 8:n_rows - 8, :]
        pos = c * T + lax.broadcasted_iota(jnp.int32, (T, BW), 0)
        cnt = jnp.minimum(pos + 1, _pool_window_row()).astype(F32)
        dpool = _pool_select(w2, w4, w8, w16) / cnt - pu
        poolh[0:POOL_PAD, :] = poolh[T:n_rows, :]
        return _dot(dpool, wpool_ref[...]) * vec_ref[V_MISC:V_MISC + 1, 0:256]

    def conv_mix(xbc_raw):
        convh[SUBLANES:SUBLANES + T, :] = xbc_raw
        acc = vec_ref[V_CONVB:V_CONVB + 1, 0:CONV_DIM]
        for kk in range(CONV_K):
            acc = acc + convh[5 + kk:5 + kk + T, :] * vec_ref[V_CONVW + kk:V_CONVW + kk + 1, 0:CONV_DIM]
        convh[0:SUBLANES, :] = convh[T:T + SUBLANES, :]
        return _silu(acc)

    v_sm, sp_all, lsg, a_row = _small_block(jnp.dot(xb, wsm_ref[...], preferred_element_type=F32), vec_ref)
    lane_s = lax.broadcasted_iota(jnp.int32, (T, LANES), 1)
    g_all = jnp.where(lane_s < 4, sp_all * a_row, jnp.where(lane_s < 8, v_sm, jnp.where(lane_s < 12, lsg, 0.0)))
    tril = causal.astype(BF16)
    g_hi = g_all.astype(BF16)
    g_r1 = g_all - g_hi.astype(F32)
    g_mid = g_r1.astype(BF16)
    g_lo = (g_r1 - g_mid.astype(F32)).astype(BF16)
    cums = []
    for s in range(T // Q):
        rs = slice(s * Q, (s + 1) * Q)
        cum = (jnp.dot(tril, g_hi[rs], preferred_element_type=F32)
               + jnp.dot(tril, g_mid[rs], preferred_element_type=F32)
               + jnp.dot(tril, g_lo[rs], preferred_element_type=F32))
        cums.append((cum, cum.T, g_all[rs].T))

    proj = {}
    dyn0 = jnp.minimum(c, 0)

    half = D_MODEL // 2

    def gate(b, part):
        lo = part * half
        gsc[dyn0 + b, :, lo:lo + half] = jnp.tanh(post(O_GATE + b * D_MODEL + lo, half))

    fillers = [
        lambda: proj.update(pu=pre(O_PU, BW)),
        lambda: proj.update(xbc=pre(O_XBC, CONV_DIM)),
        lambda: proj.update(q=pre(O_Q, BW)),
        lambda: proj.update(k=pre(O_K, BW) * (HD ** -0.5)),
        lambda: proj.update(v=pre(O_V, BW)),
        lambda: proj.update(zs=[post(O_Z + b * BW, BW) for b in range(4)]),
        lambda: proj.update(og=post(O_OG, BW)),
    ] + [functools.partial(gate, b, part) for b in range(4) for part in range(2)]
    y_s5 = _s5_chunk_scan(pre(O_SU, BW), fillers, lamp_ref, bbd_ref, cbd_ref, vec_ref, s5cr, s5ci)
    zs = proj["zs"]
    y_pool = pool_mix(proj["pu"])
    xbc = conv_mix(proj["xbc"])
    gates = [gsc.at[dyn0 + b] for b in range(4)]

    y_ssd_parts, h_m_parts = [], []
    for s in range(T // Q):
        y_c, h_c = _ssd_mlstm_chunk(
            xbc[s * Q:(s + 1) * Q], proj["q"][s * Q:(s + 1) * Q], proj["k"][s * Q:(s + 1) * Q],
            proj["v"][s * Q:(s + 1) * Q], sp_all[s * Q:(s + 1) * Q], g_all[s * Q:(s + 1) * Q],
            cums[s], causal, vec_ref, ssd_st, mc_st, nrow, mrow)
        y_ssd_parts.append(y_c)
        h_m_parts.append(h_c)
    y_ssd = jnp.concatenate(y_ssd_parts, axis=0)
    y_m = _sigmoid(proj["og"]) * jnp.concatenate(h_m_parts, axis=0)
    y_s5 = _s5_glu(y_s5, wglu_ref)

    z_acts = [_silu_of_half(z) for z in zs]
    y_ref[0] = _merge_tail(x, z_acts, [y_pool, y_ssd, y_m, y_s5], gates, wbr_ref, wout_ref, vec_ref)

    @pl.when(c == last)
    def _():
        pool_o[0] = poolh[POOL_PAD - POOL_BUF:POOL_PAD, :]
        conv_o[0] = convh[5:8, :]
        ssd_o[0] = ssd_st[...]
        mc_o[0] = mc_st[...]
        mn_o[0] = nrow[0:1, :]
        mm_o[0] = mrow[0:1, :]
        s5re_o[0] = s5cr[0:1, :]
        s5im_o[0] = s5ci[0:1, :]


def _ssd_mlstm_chunk(xbc, q, k, v, sp, g_blk, cums, causal, vec_ref, ssd_st, mc_st, nrow, mrow):
    T = xbc.shape[0]
    cum, cum_t, g_t = cums
    xs = xbc[:, 0:BW]
    dt_cols = [sp[:, h:h + 1] for h in range(HEADS)]
    acs_cols = [cum[:, h:h + 1] for h in range(HEADS)]
    acs_last = [cum[T - 1:T, h:h + 1] for h in range(HEADS)]
    xdt = xs * _expand_heads(dt_cols)
    xw_t = (xdt * _expand_heads([jnp.exp(acs_last[h] - acs_cols[h]) for h in range(HEADS)])).T
    d_ssd = vec_ref[V_MISC:V_MISC + 1, 256:512]
    m_row = mrow[0:1, :]
    n_row = nrow[0:1, :]
    hsl = [slice(h * HD, (h + 1) * HD) for h in range(HEADS)]
    b_gs = [xbc[:, BW + g * SSD_STATE:BW + (g + 1) * SSD_STATE] for g in range(2)]
    c_gs = [xbc[:, BW + 128 + g * SSD_STATE:BW + 128 + (g + 1) * SSD_STATE] for g in range(2)]

    cbs = [_dot_nt(c_gs[g], b_gs[g]) for g in range(2)]
    off_all = jnp.concatenate(
        [_dot_nt(c_gs[g], ssd_st[2 * g:2 * g + 2].reshape(2 * HD, SSD_STATE)) for g in range(2)], axis=1)
    qks = [_dot_nt(q[:, hsl[h]], k[:, hsl[h]]) for h in range(HEADS)]
    qcs = [_dot(q[:, hsl[h]], mc_st[h]) for h in range(HEADS)]
    gms = [cbs[h // 2] * jnp.exp(jnp.where(causal, acs_cols[h] - cum_t[h:h + 1, :], -jnp.inf))
           for h in range(HEADS)]
    b_cols = [cum[:, 8 + h:9 + h] for h in range(HEADS)]
    m_prevs = [m_row[:, h:h + 1] for h in range(HEADS)]
    dlogs = [jnp.where(causal, b_cols[h] - cum_t[8 + h:9 + h, :] + g_t[4 + h:5 + h, :], -jnp.inf)
             for h in range(HEADS)]
    inters = [b_cols[h] + m_prevs[h] for h in range(HEADS)]
    m_ts = [jnp.maximum(inters[h], jnp.max(dlogs[h], axis=1, keepdims=True)) for h in range(HEADS)]
    ss = [qks[h] * jnp.exp(dlogs[h] - m_ts[h]) for h in range(HEADS)]
    wis = [jnp.exp(inters[h] - m_ts[h]) for h in range(HEADS)]
    y_diag_all = jnp.concatenate([_dot(gms[h], xdt[:, hsl[h]]) for h in range(HEADS)], axis=1)
    svs = [_dot(ss[h], v[:, hsl[h]]) for h in range(HEADS)]
    y_off = off_all * _expand_heads([jnp.exp(acs_cols[h]) for h in range(HEADS)])
    y_ssd = y_diag_all + y_off + d_ssd * xs
    h_heads, wk_parts, decays, m_news = [], [], [], []
    for h in range(HEADS):
        num = svs[h] + wis[h] * qcs[h]
        den = (jnp.sum(ss[h], axis=1, keepdims=True)
               + wis[h] * jnp.sum(q[:, hsl[h]] * n_row[:, hsl[h]], axis=1, keepdims=True))
        h_heads.append(num / jnp.maximum(jnp.abs(den), jnp.exp(-m_ts[h])))
        m_new = m_ts[h][T - 1:T, :]
        b_last = b_cols[h][T - 1:T, :]
        wk_parts.append(jnp.exp(b_last - b_cols[h] + g_blk[:, 4 + h:5 + h] - m_new))
        decays.append(jnp.exp(b_last + m_prevs[h] - m_new))
        m_news.append(m_new)
    for h in range(HEADS):
        ssd_st[h] = jnp.exp(acs_last[h]) * ssd_st[h] + _dot(xw_t[hsl[h], :], b_gs[h // 2])
    lane_m = lax.broadcasted_iota(jnp.int32, (1, LANES), 1)
    m_row_new = jnp.zeros((1, LANES), F32)
    for h in range(HEADS):
        m_row_new = jnp.where(lane_m == h, jnp.broadcast_to(m_news[h], (1, LANES)), m_row_new)
    mrow[0:1, :] = m_row_new
    wk = k * _expand_heads(wk_parts)
    wk_t = wk.T
    for h in range(HEADS):
        mc_st[h] = decays[h] * mc_st[h] + _dot(wk_t[h * HD:(h + 1) * HD, :], v[:, h * HD:(h + 1) * HD])
    nrow[0:1, :] = _expand_heads(decays) * n_row + jnp.sum(wk, axis=0, keepdims=True)
    return y_ssd, jnp.concatenate(h_heads, axis=1)


def _s5_chunk_scan(su, fillers, lamp_ref, bbd_ref, cbd_ref, vec_ref, s5cr, s5ci):
    T = su.shape[0]
    p_re = lamp_ref[R_P:R_P + SUBLANES, :]
    p_im = lamp_ref[R_P + SUBLANES:R_P + 2 * SUBLANES, :]
    cr = s5cr[0:1, :]
    ci = s5ci[0:1, :]
    blk_re = []
    blk_im = []
    pieces = T // CHUNK
    nblk = CHUNK // SUBLANES
    slots = 3 * pieces
    filler_at = {}
    for i, f in enumerate(fillers):
        filler_at.setdefault((i * slots) // len(fillers), []).append(f)
    in_block = lax.broadcasted_iota(jnp.int32, (SUBLANES, BW), 0) >= 1
    for piece in range(pieces):
        su_p = su[piece * CHUNK:(piece + 1) * CHUNK]
        su_prev = jnp.where(in_block, pltpu.roll(su_p.reshape(nblk, SUBLANES, BW), 1, 1), 0.0).reshape(CHUNK, BW)
        bu = _dot(jnp.concatenate([su_p, su_prev], axis=1), bbd_ref[...])
        h_re = bu[:, 0:S5_W].reshape(nblk, SUBLANES, S5_W)
        h_im = bu[:, S5_W:2 * S5_W].reshape(nblk, SUBLANES, S5_W)
        for i, d in ((0, 2), (1, 4)):
            base = R_M + 16 * (i + 1)
            m_re = lamp_ref[base:base + SUBLANES, :]
            m_im = lamp_ref[base + SUBLANES:base + 2 * SUBLANES, :]
            r_re = pltpu.roll(h_re, d, 1)
            r_im = pltpu.roll(h_im, d, 1)
            h_re, h_im = h_re + (m_re * r_re - m_im * r_im), h_im + (m_re * r_im + m_im * r_re)
            for f in filler_at.get(3 * piece + i, ()):
                f()
        for j in range(nblk):
            br = h_re[j] + (p_re * cr - p_im * ci)
            bi = h_im[j] + (p_re * ci + p_im * cr)
            blk_re.append(br)
            blk_im.append(bi)
            cr = br[SUBLANES - 1:SUBLANES, :]
            ci = bi[SUBLANES - 1:SUBLANES, :]
        for f in filler_at.get(3 * piece + 2, ()):
            f()
    s5cr[0:1, :] = cr
    s5ci[0:1, :] = ci
    h_all = jnp.concatenate([jnp.concatenate(blk_re, axis=0), jnp.concatenate(blk_im, axis=0)], axis=1)
    return _dot(h_all, cbd_ref[...]) + vec_ref[V_MISC:V_MISC + 1, 512:768] * su


def _const_spec(shape, layer):
    nd = len(shape)
    return pl.BlockSpec((None,) + tuple(shape), lambda *_: (layer,) + (0,) * nd,
                        pipeline_mode=pl.Buffered(1))


def _w_spec(width, block, layer):
    return pl.BlockSpec((None, D_MODEL, width), lambda *_: (layer, 0, block), pipeline_mode=pl.Buffered(1))


def _prompt_layer(x, layer, p):
    bsz, seq, _ = x.shape
    T = TILE if seq % TILE == 0 else CHUNK
    nc = seq // T

    def bmap(nd):
        return lambda b, c: (b,) + (0,) * nd

    out_shape = (
        jax.ShapeDtypeStruct((bsz, seq, D_MODEL), F32),
        jax.ShapeDtypeStruct((bsz, POOL_BUF, BW), F32),
        jax.ShapeDtypeStruct((bsz, CONV_K - 1, CONV_DIM), F32),
        jax.ShapeDtypeStruct((bsz, HEADS, HD, SSD_STATE), F32),
        jax.ShapeDtypeStruct((bsz, HEADS, HD, HD), F32),
        jax.ShapeDtypeStruct((bsz, 1, BW), F32),
        jax.ShapeDtypeStruct((bsz, 1, LANES), F32),
        jax.ShapeDtypeStruct((bsz, 1, S5_W), F32),
        jax.ShapeDtypeStruct((bsz, 1, S5_W), F32),
    )
    out_specs = (
        pl.BlockSpec((1, T, D_MODEL), lambda b, c: (b, c, 0)),
        pl.BlockSpec((1, POOL_BUF, BW), bmap(2)),
        pl.BlockSpec((1, CONV_K - 1, CONV_DIM), bmap(2)),
        pl.BlockSpec((1, HEADS, HD, SSD_STATE), bmap(3)),
        pl.BlockSpec((1, HEADS, HD, HD), bmap(3)),
        pl.BlockSpec((1, 1, BW), bmap(2)),
        pl.BlockSpec((1, 1, LANES), bmap(2)),
        pl.BlockSpec((1, 1, S5_W), bmap(2)),
        pl.BlockSpec((1, 1, S5_W), bmap(2)),
    )
    in_specs = [
        pl.BlockSpec((1, T, D_MODEL), lambda b, c: (b, c, 0)),
        _w_spec(PRE_W, PRE_BLOCK, layer),
        _w_spec(POST_W, 0, layer),
        _const_spec((D_MODEL, LANES), layer),
        _const_spec((16, D_MODEL), layer),
        _const_spec((BW, BW), layer),
        _const_spec((LAMP_ROWS, S5_W), layer),
        _const_spec((2 * BW, 2 * S5_W), layer),
        _const_spec((2 * S5_W, BW), layer),
        _const_spec((BW, 2 * BW), layer),
        _const_spec((4, BW, D_MODEL), layer),
        _const_spec((D_MODEL, D_MODEL), layer),
    ]
    scratch = [
        pltpu.VMEM((T + POOL_PAD, BW), F32),
        pltpu.VMEM((3, T + POOL_PAD, BW), F32),
        pltpu.VMEM((T + SUBLANES, CONV_DIM), F32),
        pltpu.VMEM((HEADS, HD, SSD_STATE), F32),
        pltpu.VMEM((HEADS, HD, HD), F32),
        pltpu.VMEM((SUBLANES, BW), F32),
        pltpu.VMEM((SUBLANES, LANES), F32),
        pltpu.VMEM((SUBLANES, S5_W), F32),
        pltpu.VMEM((SUBLANES, S5_W), F32),
        pltpu.VMEM((4, T, D_MODEL), F32),
    ]
    return pl.pallas_call(
        _prompt_kernel,
        grid=(bsz, nc),
        in_specs=in_specs,
        out_specs=out_specs,
        out_shape=out_shape,
        scratch_shapes=scratch,
        compiler_params=pltpu.CompilerParams(dimension_semantics=("arbitrary", "arbitrary"),
                                             vmem_limit_bytes=VMEM_LIMIT),
        name="prompt_layer",
    )(x, p["wall"], p["wall"], p["wsm"], p["vec"], p["wpool"], p["lamp"], p["bbd"], p["cbd"],
      p["wglu"], p["wbr"], p["wout"])


RW_BC, RW_XDT, RW_Q, RW_WK, RW_V, RW_SDEC, RW_MDEC, RW_W = 0, 256, 512, 768, 1024, 1280, 1408, 1536
MID_W = 7 * BW


def _sample_pre_kernel(x_ref, wpre_ref, wsm_ref, vec_ref, wpool_ref, lamp_ref, bbd_ref, cbd_ref,
                       pool_ref, conv_ref, n_ref, m_ref, s5re_ref, s5im_ref,
                       rows_o, mid_o, pool_o, conv_o, n_o, m_o, s5re_o, s5im_o):
    x = x_ref[...]
    nb = x.shape[0]
    pp = _dot(x, wpre_ref[...])
    pu = pp[:, O_PU:O_PU + BW]
    xbc_raw = pp[:, O_XBC:O_XBC + CONV_DIM]
    q = pp[:, O_Q:O_Q + BW]
    k = pp[:, O_K:O_K + BW] * (HD ** -0.5)
    v = pp[:, O_V:O_V + BW]
    su = pp[:, O_SU:O_SU + BW]

    def ext(j):
        return pool_ref[j]

    w2 = pu + ext(14)
    w4 = w2 + ext(13) + ext(12)
    w8 = w4
    for j in range(11, 7, -1):
        w8 = w8 + ext(j)
    w16 = w8
    for j in range(7, -1, -1):
        w16 = w16 + ext(j)
    cnt = jnp.minimum(PAST_LEN + 1, _pool_window_row()).astype(F32)
    dpool = _pool_select(w2, w4, w8, w16) / cnt - pu
    y_pool = _dot(dpool, wpool_ref[...]) * vec_ref[V_MISC:V_MISC + 1, 0:256]
    for j in range(POOL_BUF - 1):
        pool_o[j] = pool_ref[j + 1]
    pool_o[POOL_BUF - 1] = pu

    acc = vec_ref[V_CONVB:V_CONVB + 1, 0:CONV_DIM]
    for kk in range(CONV_K - 1):
        acc = acc + conv_ref[kk] * vec_ref[V_CONVW + kk:V_CONVW + kk + 1, 0:CONV_DIM]
    acc = acc + xbc_raw * vec_ref[V_CONVW + CONV_K - 1:V_CONVW + CONV_K, 0:CONV_DIM]
    for kk in range(CONV_K - 2):
        conv_o[kk] = conv_ref[kk + 1]
    conv_o[CONV_K - 2] = xbc_raw
    xbc = _silu(acc)
    xs = xbc[:, 0:BW]
    bc = xbc[:, BW:2 * BW]

    v_sm, sp, lsg, a_row = _small_block(_dot(x, wsm_ref[...]), vec_ref)
    sdec_blk = jnp.exp(sp * a_row)
    dt_cols = [sp[:, h:h + 1] for h in range(HEADS)]
    xdt = xs * _expand_heads(dt_cols)
    cb_cols = []
    for g in range(2):
        b_g = bc[:, g * SSD_STATE:(g + 1) * SSD_STATE]
        c_g = bc[:, 128 + g * SSD_STATE:128 + (g + 1) * SSD_STATE]
        cb_cols.append(jnp.sum(c_g * b_g, axis=1, keepdims=True))
    yssd_part = (_expand_heads([cb_cols[h // 2] for h in range(HEADS)]) * xdt
                 + vec_ref[V_MISC:V_MISC + 1, 256:512] * xs)
    sdec_exp = _expand_heads([sdec_blk[:, h:h + 1] for h in range(HEADS)])

    n0 = n_ref[...].T
    m0 = jnp.concatenate([m_ref[...], jnp.zeros((SUBLANES - HEADS, nb), F32)], axis=0).T
    s_cols, wi_cols, dw_cols, dmax_cols, m_cols = [], [], [], [], []
    for h in range(HEADS):
        sl = slice(h * HD, (h + 1) * HD)
        ig = v_sm[:, 4 + h:5 + h]
        lf = lsg[:, 8 + h:9 + h]
        inter = lf + m0[:, h:h + 1]
        m_t = jnp.maximum(inter, (lf - lf) + ig)
        dw = jnp.exp((lf - lf) + ig - m_t)
        wi = jnp.exp(inter - m_t)
        s = jnp.sum(q[:, sl] * k[:, sl], axis=1, keepdims=True) * dw
        den = s + wi * jnp.sum(q[:, sl] * n0[:, sl], axis=1, keepdims=True)
        s_cols.append(s)
        wi_cols.append(wi)
        dw_cols.append(dw)
        dmax_cols.append(jnp.maximum(jnp.abs(den), jnp.exp(-m_t)))
        m_cols.append(m_t)
    wi_exp = _expand_heads(wi_cols)
    wk = _expand_heads(dw_cols) * k
    n_o[...] = (wi_exp * n0 + wk).T
    lane = lax.broadcasted_iota(jnp.int32, (nb, LANES), 1)
    m_blk = jnp.zeros((nb, LANES), F32)
    wi_blk = jnp.zeros((nb, LANES), F32)
    for h in range(HEADS):
        m_blk = jnp.where(lane == h, jnp.broadcast_to(m_cols[h], (nb, LANES)), m_blk)
        wi_blk = jnp.where(lane == h, jnp.broadcast_to(wi_cols[h], (nb, LANES)), wi_blk)
    m_o[...] = m_blk.T[0:HEADS, :]

    bu = _dot(su, bbd_ref[0:BW, :])
    lbr = lamp_ref[R_LAM:R_LAM + 1, :]
    lbi = lamp_ref[R_LAM + 1:R_LAM + 2, :]
    h0r = s5re_ref[...].T
    h0i = s5im_ref[...].T
    h_re = bu[:, 0:S5_W] + (lbr * h0r - lbi * h0i)
    h_im = bu[:, S5_W:2 * S5_W] + (lbr * h0i + lbi * h0r)
    s5re_o[...] = h_re.T
    s5im_o[...] = h_im.T
    y_s5 = _dot(jnp.concatenate([h_re, h_im], axis=1), cbd_ref[...]) + vec_ref[V_MISC:V_MISC + 1, 512:768] * su

    rows_o[...] = jnp.concatenate([bc, xdt, q, wk, v, sdec_blk, wi_blk], axis=1).T
    mid_o[...] = jnp.concatenate([y_pool, yssd_part, sdec_exp, _expand_heads(s_cols) * v, wi_exp,
                                  _expand_heads(dmax_cols), y_s5], axis=1)


def _sample_pre(x, layer, p, pool2, conv2, n2, m2, s5re2, s5im2):
    nb = x.shape[0]
    full = lambda a: pl.BlockSpec(a.shape, lambda *_: (0,) * a.ndim)
    of_layer = lambda a: pl.BlockSpec((None,) + a.shape[1:], lambda *_: (layer,) + (0,) * (a.ndim - 1))
    out_shape = (
        jax.ShapeDtypeStruct((RW_W, nb), F32),
        jax.ShapeDtypeStruct((nb, MID_W), F32),
        jax.ShapeDtypeStruct(pool2.shape[1:], F32),
        jax.ShapeDtypeStruct(conv2.shape[1:], F32),
        jax.ShapeDtypeStruct((BW, nb), F32),
        jax.ShapeDtypeStruct((HEADS, nb), F32),
        jax.ShapeDtypeStruct((S5_W, nb), F32),
        jax.ShapeDtypeStruct((S5_W, nb), F32),
    )
    in_specs = [
        full(x),
        _w_spec(PRE_W, PRE_BLOCK, layer),
        _const_spec((D_MODEL, LANES), layer),
        _const_spec((16, D_MODEL), layer),
        _const_spec((BW, BW), layer),
        _const_spec((LAMP_ROWS, S5_W), layer),
        _const_spec((2 * BW, 2 * S5_W), layer),
        _const_spec((2 * S5_W, BW), layer),
        of_layer(pool2), of_layer(conv2), of_layer(n2), of_layer(m2), of_layer(s5re2), of_layer(s5im2),
    ]
    return pl.pallas_call(
        _sample_pre_kernel,
        grid=(1,),
        in_specs=in_specs,
        out_specs=tuple(pl.BlockSpec(s.shape, lambda *_, nd=len(s.shape): (0,) * nd) for s in out_shape),
        out_shape=out_shape,
        compiler_params=pltpu.CompilerParams(dimension_semantics=("arbitrary",),
                                             vmem_limit_bytes=VMEM_LIMIT),
        name="sample_pre",
    )(x, p["wall"], p["wsm"], p["vec"], p["wpool"], p["lamp"], p["bbd"], p["cbd"],
      pool2, conv2, n2, m2, s5re2, s5im2)


def _sample_state_kernel(first_layer, rows_ref, ssd_ref, mc_ref, *rest):
    h = pl.program_id(0)
    part = pl.program_id(1)
    if first_layer:
        yoff_ref, qc_ref, ssd_o, mc_o = rest
        ssd_o[1] = ssd_ref[1]
        mc_o[1] = mc_ref[1]
        ssd_in, mc_in, ssd_out, mc_out = ssd_ref.at[0], mc_ref.at[0], ssd_o.at[0], mc_o.at[0]
    else:
        _, _, yoff_ref, qc_ref, ssd_out, mc_out = rest
        ssd_in, mc_in = ssd_ref, mc_ref
    g64 = pl.multiple_of((h // 2) * SSD_STATE, SSD_STATE)
    h64 = pl.multiple_of(h * HD, HD)
    b_t = rows_ref[pl.ds(RW_BC + g64, SSD_STATE), :]
    c_t = rows_ref[pl.ds(RW_BC + 128 + g64, SSD_STATE), :]
    v_t = rows_ref[pl.ds(RW_V + h64, HD), :]
    sdec = rows_ref[pl.ds(RW_SDEC + h, 1), :]
    mdec = rows_ref[pl.ds(RW_MDEC + h, 1), :]
    base = h64 + part * STATE_ROWS

    def body(i, acc):
        hs = ssd_in[i]
        yoff_ref[pl.ds(i, 1), :] = jnp.sum(hs * c_t, axis=0, keepdims=True)
        ssd_out[i] = sdec * hs + rows_ref[pl.ds(RW_XDT + base + i, 1), :] * b_t
        cs = mc_in[i]
        mc_out[i] = mdec * cs + rows_ref[pl.ds(RW_WK + base + i, 1), :] * v_t
        return acc + rows_ref[pl.ds(RW_Q + base + i, 1), :] * cs

    acc = lax.fori_loop(0, STATE_ROWS, body, jnp.zeros((HD, rows_ref.shape[1]), F32), unroll=4)

    @pl.when(part == 0)
    def _():
        qc_ref[...] = acc

    @pl.when(part != 0)
    def _():
        qc_ref[...] = qc_ref[...] + acc


STATE_ROWS = 32


def _sample_state(rows_t, layer, ssd_all, mc_all, prev=None):
    nb = rows_t.shape[1]
    depth = ssd_all.shape[0]
    parts = HD // STATE_ROWS
    tile = (STATE_ROWS, HD, nb)
    rows_spec = pl.BlockSpec(rows_t.shape, lambda h, s: (0, 0))
    yoff_spec = pl.BlockSpec((STATE_ROWS, nb), lambda h, s: (h * parts + s, 0))
    qc_spec = pl.BlockSpec((HD, nb), lambda h, s: (h, 0))
    out_shape = (jax.ShapeDtypeStruct((BW, nb), F32),
                 jax.ShapeDtypeStruct((BW, nb), F32),
                 jax.ShapeDtypeStruct(ssd_all.shape, F32),
                 jax.ShapeDtypeStruct(mc_all.shape, F32))
    params = pltpu.CompilerParams(dimension_semantics=("arbitrary", "arbitrary"),
                                  vmem_limit_bytes=VMEM_LIMIT)
    if prev is None:
        assert layer == 0 and depth == 2
        both = pl.BlockSpec((depth, None) + tile, lambda h, s: (0, h, s, 0, 0))
        return pl.pallas_call(
            functools.partial(_sample_state_kernel, True),
            grid=(HEADS, parts),
            in_specs=[rows_spec, both, both],
            out_specs=(yoff_spec, qc_spec, both, both),
            out_shape=out_shape,
            compiler_params=params,
            name="sample_state_first",
        )(rows_t, ssd_all, mc_all)
    one = pl.BlockSpec((None, None) + tile, lambda h, s: (layer, h, s, 0, 0))
    anywhere = pl.BlockSpec(memory_space=pl.ANY)
    return pl.pallas_call(
        functools.partial(_sample_state_kernel, False),
        grid=(HEADS, parts),
        in_specs=[rows_spec, one, one, anywhere, anywhere],
        out_specs=(yoff_spec, qc_spec, one, one),
        out_shape=out_shape,
        input_output_aliases={3: 2, 4: 3},
        compiler_params=params,
        name="sample_state_next",
    )(rows_t, ssd_all, mc_all, prev[0], prev[1])


def _sample_post_kernel(x_ref, wpost_ref, vec_ref, mid_ref, yoff_ref, qc_ref, wglu_ref, wbr_ref, wout_ref, y_ref):
    x = x_ref[...]
    xb = x.astype(BF16)

    def post(lo, width):
        return jnp.dot(xb, wpost_ref[:, lo:lo + width], preferred_element_type=F32)

    def mid(i):
        return mid_ref[:, i * BW:(i + 1) * BW]

    y_pool, yssd_part, sdec_exp, sv, wi_exp, dmax_exp, y_s5 = [mid(i) for i in range(7)]
    y_ssd = yssd_part + yoff_ref[...].T * sdec_exp
    hh = (sv + wi_exp * qc_ref[...].T) / dmax_exp
    y_m = _sigmoid(post(O_OG, BW)) * hh
    y_s5 = _s5_glu(y_s5, wglu_ref)
    z_acts = [_silu_of_half(post(O_Z + b * BW, BW)) for b in range(4)]
    gates = [jnp.tanh(post(O_GATE + b * D_MODEL, D_MODEL)) for b in range(4)]
    y_ref[...] = _merge_tail(x, z_acts, [y_pool, y_ssd, y_m, y_s5], gates, wbr_ref, wout_ref, vec_ref)


def _sample_post(x, layer, p, mid, yoff_t, qc_t):
    nb = x.shape[0]
    full = lambda a: pl.BlockSpec(a.shape, lambda *_: (0,) * a.ndim)
    return pl.pallas_call(
        _sample_post_kernel,
        grid=(1,),
        in_specs=[full(x), _w_spec(POST_W, 0, layer), _const_spec((16, D_MODEL), layer),
                  full(mid), full(yoff_t), full(qc_t), _const_spec((BW, 2 * BW), layer),
                  _const_spec((4, BW, D_MODEL), layer), _const_spec((D_MODEL, D_MODEL), layer)],
        out_specs=pl.BlockSpec((nb, D_MODEL), lambda *_: (0, 0)),
        out_shape=jax.ShapeDtypeStruct((nb, D_MODEL), F32),
        compiler_params=pltpu.CompilerParams(dimension_semantics=("arbitrary",),
                                             vmem_limit_bytes=VMEM_LIMIT),
        name="sample_post",
    )(x, p["wall"], p["vec"], mid, yoff_t, qc_t, p["wglu"], p["wbr"], p["wout"])


def _w_permute_kernel(src_ref, halved_ref, w_hbm, out_ref, buf, sem):
    i = pl.program_id(0)
    depth = out_ref.shape[0]

    def copies(step, slot):
        start = src_ref[step]
        return [pltpu.make_async_copy(w_hbm.at[pl.ds(start, WCOLS), l, :], buf.at[slot, l], sem.at[slot, l])
                for l in range(depth)]

    @pl.when(i == 0)
    def _():
        for cp in copies(0, 0):
            cp.start()

    @pl.when(i + 1 < pl.num_programs(0))
    def _():
        for cp in copies(i + 1, (i + 1) % 2):
            cp.start()

    for cp in copies(i, i % 2):
        cp.wait()
    scale = jnp.where(halved_ref[i] == 1, 0.5, 1.0).astype(F32)
    for l in range(depth):
        out_ref[l] = (buf[i % 2, l].T * scale).astype(BF16)


def _w_permute(w_in):
    depth, rows, _ = w_in.shape
    w_t = jnp.transpose(w_in, (2, 0, 1))
    nblk = len(W_SRC)
    return pl.pallas_call(
        _w_permute_kernel,
        grid_spec=pltpu.PrefetchScalarGridSpec(
            num_scalar_prefetch=2, grid=(nblk,),
            in_specs=[pl.BlockSpec(memory_space=pl.ANY)],
            out_specs=pl.BlockSpec((depth, rows, WCOLS), lambda i, src, halved: (0, 0, i)),
            scratch_shapes=[pltpu.VMEM((2, depth, WCOLS, rows), F32), pltpu.SemaphoreType.DMA((2, depth))]),
        out_shape=jax.ShapeDtypeStruct((depth, rows, nblk * WCOLS), BF16),
        compiler_params=pltpu.CompilerParams(dimension_semantics=("arbitrary",)),
        name="w_permute",
    )(jnp.asarray(W_SRC, jnp.int32), jnp.asarray(W_HALVED, jnp.int32), w_t)


def _prepare_params(w_in, w_pool, pool_scale, conv_w, conv_b, dt_bias, a_log, d_ssd, ig_bias, fg_bias,
                    lam_re, lam_im, b_re, b_im, c_re, c_im, log_dt, d_s5, w_glu, w_br, w_out, ln_g, ln_b):
    depth = w_in.shape[0]
    wall = _w_permute(w_in)
    small_cols = [w_in[:, :, lo:lo + n] for lo, n in SMALL_SRC]
    wsm = jnp.concatenate(small_cols + [jnp.zeros((depth, D_MODEL, LANES - 12), w_in.dtype)], axis=2).astype(BF16)

    zeros = lambda n: jnp.zeros((depth, n), F32)
    misc = jnp.concatenate([pool_scale, jnp.repeat(d_ssd, HD, axis=1), d_s5,
                            dt_bias, ig_bias, fg_bias, zeros(LANES - 12),
                            a_log, zeros(LANES - HEADS)], axis=1)
    pad512 = lambda a: jnp.concatenate([a, jnp.zeros(a.shape[:-1] + (D_MODEL - CONV_DIM,), F32)], axis=-1)
    vec = jnp.concatenate([ln_g[:, None], ln_b[:, None], misc[:, None], pad512(conv_b)[:, None],
                           pad512(conv_w), jnp.zeros((depth, 16 - 4 - CONV_K, D_MODEL), F32)], axis=1)

    eye4 = jnp.eye(len(POOL_WINDOWS), dtype=F32)
    wpool = jnp.einsum("lgce,gh->lgche", w_pool, eye4).reshape(depth, BW, BW).astype(BF16)

    lam2 = jnp.stack([lam_re.reshape(depth, S5_W), lam_im.reshape(depth, S5_W)], axis=1)
    ldt = jnp.repeat(log_dt, S5_STATE, axis=1)[:, None]
    to_rows = lambda b: jnp.transpose(b, (0, 3, 1, 2)).reshape(depth, S5_CH, S5_W)
    bcat = jnp.concatenate([to_rows(b_re), to_rows(b_im)], axis=1)

    eye_g = jnp.eye(S5_GROUPS, dtype=F32)
    cb = lambda cm: jnp.einsum("lgcp,gh->lgphc", cm, eye_g).reshape(depth, S5_W, BW)
    cbd = jnp.concatenate([cb(c_re), -cb(c_im)], axis=1).astype(BF16)

    lamp, bbd = [], []
    for l in range(depth):
        lp, bb = _s5_prep(lam2[l], ldt[l], bcat[l])
        lamp.append(lp)
        bbd.append(bb)
    return dict(wall=wall, wsm=wsm, vec=vec, wpool=wpool, lamp=jnp.stack(lamp), bbd=jnp.stack(bbd),
                cbd=cbd, wglu=w_glu.astype(BF16), wbr=w_br.astype(BF16), wout=(0.5 * w_out).astype(BF16))


def kernel(x_prompt, x_sample, state_pool, state_ssd_conv, state_ssd, state_mlstm_C, state_mlstm_n, state_mlstm_m, state_s5_re, state_s5_im, w_in, w_pool, pool_scale, conv_w, conv_b, dt_bias, a_log, d_ssd, ig_bias, fg_bias, lam_re, lam_im, b_re, b_im, c_re, c_im, log_dt, d_s5, w_glu, w_br, w_out, ln_g, ln_b):
    p = _prepare_params(w_in, w_pool, pool_scale, conv_w, conv_b, dt_bias, a_log, d_ssd, ig_bias, fg_bias,
                        lam_re, lam_im, b_re, b_im, c_re, c_im, log_dt, d_s5, w_glu, w_br, w_out, ln_g, ln_b)
    depth = w_in.shape[0]
    bsz = x_prompt.shape[0]
    nb = x_sample.shape[0]
    yp = x_prompt
    ys = x_sample.reshape(nb, D_MODEL)
    outs_p = [[] for _ in range(8)]
    outs_s = [[] for _ in range(8)]
    mats_s = None
    pool_t = jnp.transpose(state_pool, (0, 2, 1, 3))
    conv_t = jnp.transpose(state_ssd_conv, (0, 2, 1, 3))
    ssd_t = jnp.transpose(state_ssd, (0, 2, 3, 4, 1))
    mc_t = jnp.transpose(state_mlstm_C, (0, 2, 3, 4, 1))
    n_t = jnp.transpose(state_mlstm_n, (0, 2, 3, 1)).reshape(depth, BW, nb)
    m_t = jnp.transpose(state_mlstm_m, (0, 2, 1))
    re_t = jnp.transpose(state_s5_re, (0, 2, 3, 1)).reshape(depth, S5_W, nb)
    im_t = jnp.transpose(state_s5_im, (0, 2, 3, 1)).reshape(depth, S5_W, nb)
    for l in range(depth):
        yp, pool_p, conv_p, ssd_p, mc_p, mn_p, mm_p, re_p, im_p = _prompt_layer(yp, l, p)
        new_p = (pool_p, conv_p, ssd_p, mc_p, mn_p.reshape(bsz, HEADS, HD), mm_p[:, 0, :HEADS],
                 re_p.reshape(bsz, S5_GROUPS, S5_STATE), im_p.reshape(bsz, S5_GROUPS, S5_STATE))

        rows_t, mid, pool_s, conv_s, n_s, m_s, re_s, im_s = _sample_pre(
            ys, l, p, pool_t, conv_t, n_t, m_t, re_t, im_t)
        yoff_t, qc_t, ssd_s, mc_s = _sample_state(rows_t, l, ssd_t, mc_t, mats_s)
        mats_s = (ssd_s, mc_s)
        ys = _sample_post(ys, l, p, mid, yoff_t, qc_t)
        new_s = (pool_s, conv_s, None, None, n_s, m_s, re_s, im_s)
        for i in range(8):
            outs_p[i].append(new_p[i])
            outs_s[i].append(new_s[i])
    sp = [jnp.stack(o, axis=0) for o in outs_p]
    st = [mats_s[i - 2] if i in (2, 3) else jnp.stack(outs_s[i], axis=0) for i in range(8)]
    ss = [jnp.transpose(st[0], (0, 2, 1, 3)), jnp.transpose(st[1], (0, 2, 1, 3)),
          jnp.transpose(st[2], (0, 4, 1, 2, 3)), jnp.transpose(st[3], (0, 4, 1, 2, 3)),
          jnp.transpose(st[4].reshape(depth, HEADS, HD, nb), (0, 3, 1, 2)),
          jnp.transpose(st[5], (0, 2, 1)),
          jnp.transpose(st[6].reshape(depth, S5_GROUPS, S5_STATE, nb), (0, 3, 1, 2)),
          jnp.transpose(st[7].reshape(depth, S5_GROUPS, S5_STATE, nb), (0, 3, 1, 2))]
    out = [yp, ys.reshape(nb, 1, D_MODEL)]
    for i in range(8):
        out.append(sp[i])
        out.append(ss[i])
    return tuple(out)
```

```python
import functools

import jax
import jax.numpy as jnp
from jax import lax
from jax.experimental import pallas as pl
from jax.experimental.pallas import tpu as pltpu

F32 = jnp.float32
BF16 = jnp.bfloat16

D_MODEL = 1024
DEPTH = 2
PAST_LEN = 16384
BW = 256
POOL_WINDOWS = (2, 4, 8, 16)
POOL_BUF = 15
POOL_PAD = 32
HEADS = 4
HD = 64
SSD_STATE = 64
CONV_K = 4
CONV_DIM = 512
S5_GROUPS = 16
S5_CH = 16
S5_STATE = 64
S5_W = S5_GROUPS * S5_STATE
CHUNK = 128
TILE = 512
ALPHA = (2.0 * DEPTH) ** 0.25
LN_EPS = 1e-5

SUBLANES = 8
LANES = 128
VMEM_LIMIT = 62 * 1024 * 1024

PRE_W = 1792
POST_W = 5376
O_PU, O_XBC, O_Q, O_K, O_V, O_SU = 0, 256, 768, 1024, 1280, 1536
O_Z, O_OG, O_GATE = 0, 1024, 1280
PRE_BLOCK = POST_W // PRE_W
assert POST_W == PRE_BLOCK * PRE_W
WCOLS = 256
_SRC_POST = [256, 1028, 2316, 2828, 2060] + [3084 + WCOLS * i for i in range(16)]
_SRC_PRE = [0, 512, 768, 1284, 1540, 1796, 2572]
W_SRC = tuple(_SRC_POST + _SRC_PRE)
W_HALVED = tuple([1, 1, 1, 1, 0] + [1] * 16 + [0] * 7)
SMALL_SRC = ((1024, 4), (2052, 4), (2056, 4))

R_LAM = 0
R_P = 8
R_M = 24
LAMP_ROWS = R_M + 3 * 16

V_LNG, V_LNB, V_MISC, V_CONVB, V_CONVW = 0, 1, 2, 3, 4


def _dot(a, b):
    return jnp.dot(a.astype(BF16), b.astype(BF16), preferred_element_type=F32)


def _dot_nt(a, b):
    return lax.dot_general(a.astype(BF16), b.astype(BF16), (((1,), (1,)), ((), ())),
                           preferred_element_type=F32)


def _softplus(x):
    return jnp.logaddexp(x, 0.0)


def _sigmoid(x):
    return 0.5 * jnp.tanh(0.5 * x) + 0.5


def _silu(x):
    return x * _sigmoid(x)


def _silu_of_half(xh):
    return xh * (jnp.tanh(xh) + 1.0)


def _expand_heads(cols, width=HD):
    rows = cols[0].shape[0]
    lane = lax.broadcasted_iota(jnp.int32, (rows, HEADS * width), 1)
    out = jnp.broadcast_to(cols[HEADS - 1], (rows, HEADS * width))
    for h in range(HEADS - 2, -1, -1):
        out = jnp.where(lane < (h + 1) * width, jnp.broadcast_to(cols[h], (rows, HEADS * width)), out)
    return out


def _pool_select(w2, w4, w8, w16):
    lane = lax.broadcasted_iota(jnp.int32, w2.shape, 1)
    return jnp.where(lane < 64, w2, jnp.where(lane < 128, w4, jnp.where(lane < 192, w8, w16)))


def _pool_window_row():
    lane = lax.broadcasted_iota(jnp.int32, (1, BW), 1)
    return jnp.where(lane < 64, 2, jnp.where(lane < 128, 4, jnp.where(lane < 192, 8, 16)))


def _small_block(sm, vec_ref):
    bias = vec_ref[V_MISC:V_MISC + 1, 768:896]
    alog = vec_ref[V_MISC:V_MISC + 1, 896:1024]
    v = sm + bias
    sp = _softplus(v)
    lsg = -_softplus(-v)
    a_row = -jnp.exp(alog)
    return v, sp, lsg, a_row


def _merge_tail(x, z_acts, ys, gate_tanh, wbr_ref, wout_ref, vec_ref):
    merged = None
    for b in range(4):
        br = ys[b] * z_acts[b]
        pb = _dot(br, wbr_ref[b])
        gb = gate_tanh[b][...] + 1.0
        merged = gb * pb if merged is None else merged + gb * pb
    out = _dot(merged, wout_ref[...])
    r = ALPHA * x + out
    mu = jnp.mean(r, axis=-1, keepdims=True)
    var = jnp.mean(jnp.square(r - mu), axis=-1, keepdims=True)
    g = vec_ref[V_LNG:V_LNG + 1, :]
    b = vec_ref[V_LNB:V_LNB + 1, :]
    return (r - mu) * lax.rsqrt(var + LN_EPS) * g + b


def _s5_glu(y_s5, wglu_ref):
    glu = _dot(jax.nn.gelu(y_s5), wglu_ref[...])
    return glu[:, :BW] * _sigmoid(glu[:, BW:])


def _s5_prep_kernel(lam_ref, ldt_ref, b_ref, lamp_ref, bbd_ref):
    lr = lam_ref[0:1, :]
    li = lam_ref[1:2, :]
    step = jnp.exp(ldt_ref[0:1, :])
    e = jnp.exp(lr * step)
    lbr = e * jnp.cos(li * step)
    lbi = e * jnp.sin(li * step)
    den = lr * lr + li * li
    nr = lbr - 1.0
    qr = (nr * lr + lbi * li) / den
    qi = (lbi * lr - nr * li) / den
    bre = b_ref[0:S5_CH, :]
    bim = b_ref[S5_CH:2 * S5_CH, :]
    bbr = qr * bre - qi * bim
    bbi = qr * bim + qi * bre
    for part, (vr, vi) in enumerate(((bbr, bbi), (lbr * bbr - lbi * bbi, lbr * bbi + lbi * bbr))):
        bb = jnp.concatenate([vr, vi], axis=1)
        tiled = jnp.concatenate([bb] * S5_GROUPS, axis=0)
        rowg = lax.broadcasted_iota(jnp.int32, tiled.shape, 0) // S5_CH
        colg = (lax.broadcasted_iota(jnp.int32, tiled.shape, 1) % S5_W) // S5_STATE
        bbd_ref[part * BW:(part + 1) * BW, :] = jnp.where(rowg == colg, tiled, 0.0).astype(BF16)

    pows = [(lbr, lbi)]
    for _ in range(7):
        pr, pi = pows[-1]
        pows.append((pr * lbr - pi * lbi, pr * lbi + pi * lbr))
    row = lax.broadcasted_iota(jnp.int32, (SUBLANES, S5_W), 0)

    def bcast(v):
        return jnp.broadcast_to(v, (SUBLANES, S5_W))

    p_re = bcast(pows[7][0])
    p_im = bcast(pows[7][1])
    for j in range(6, -1, -1):
        p_re = jnp.where(row == j, bcast(pows[j][0]), p_re)
        p_im = jnp.where(row == j, bcast(pows[j][1]), p_im)
    lamp_ref[R_LAM:R_LAM + SUBLANES, :] = jnp.where(row == 0, bcast(lbr), jnp.where(row == 1, bcast(lbi), 0.0))
    lamp_ref[R_P:R_P + SUBLANES, :] = p_re
    lamp_ref[R_P + SUBLANES:R_P + 2 * SUBLANES, :] = p_im
    for i, d in enumerate((1, 2, 4)):
        base = R_M + 16 * i
        lamp_ref[base:base + SUBLANES, :] = jnp.where(row >= d, bcast(pows[d - 1][0]), 0.0)
        lamp_ref[base + SUBLANES:base + 2 * SUBLANES, :] = jnp.where(row >= d, bcast(pows[d - 1][1]), 0.0)


def _s5_prep(lam2, ldt, bcat):
    return pl.pallas_call(
        _s5_prep_kernel,
        out_shape=(jax.ShapeDtypeStruct((LAMP_ROWS, S5_W), F32),
                   jax.ShapeDtypeStruct((2 * BW, 2 * S5_W), BF16)),
        name="s5_prep",
    )(lam2, ldt, bcat)


def _prompt_kernel(x_ref, wpre_ref, wpost_ref, wsm_ref, vec_ref, wpool_ref, lamp_ref, bbd_ref, cbd_ref,
                   wglu_ref, wbr_ref, wout_ref,
                   y_ref, pool_o, conv_o, ssd_o, mc_o, mn_o, mm_o, s5re_o, s5im_o,
                   poolh, poolw, convh, ssd_st, mc_st, nrow, mrow, s5cr, s5ci, gsc):
    T = x_ref.shape[1]
    Q = CHUNK
    c = pl.program_id(1)
    last = pl.num_programs(1) - 1

    @pl.when(c == 0)
    def _():
        poolh[0:POOL_PAD, :] = jnp.zeros((POOL_PAD, BW), F32)
        convh[0:SUBLANES, :] = jnp.zeros((SUBLANES, CONV_DIM), F32)
        ssd_st[...] = jnp.zeros(ssd_st.shape, F32)
        mc_st[...] = jnp.zeros(mc_st.shape, F32)
        nrow[...] = jnp.zeros(nrow.shape, F32)
        mrow[...] = jnp.zeros(mrow.shape, F32)
        s5cr[...] = jnp.zeros(s5cr.shape, F32)
        s5ci[...] = jnp.zeros(s5ci.shape, F32)

    x = x_ref[0]
    xb = x.astype(BF16)

    def pre(lo, width):
        return jnp.dot(xb, wpre_ref[:, lo:lo + width], preferred_element_type=F32)

    def post(lo, width):
        return jnp.dot(xb, wpost_ref[:, lo:lo + width], preferred_element_type=F32)

    rows_i = lax.broadcasted_iota(jnp.int32, (Q, Q), 0)
    cols_i = lax.broadcasted_iota(jnp.int32, (Q, Q), 1)
    causal = rows_i >= cols_i

    n_rows = POOL_PAD + T

    def pool_mix(pu):
        poolh[POOL_PAD:n_rows, :] = pu
        poolw[0, 8:n_rows, :] = poolh[8:n_rows, :] + poolh[7:n_rows - 1, :]
        poolw[1, 16:n_rows, :] = poolw[0, 16:n_rows, :] + poolw[0, 14:n_rows - 2, :]
        poolw[2, 24:n_rows, :] = poolw[1, 24:n_rows, :] + poolw[1, 20:n_rows - 4, :]
        w2 = poolw[0, POOL_PAD:n_rows, :]
        w4 = poolw[1, POOL_PAD:n_rows, :]
        w8 = poolw[2, POOL_PAD:n_rows, :]
        w16 = w8 + poolw[2, POOL_PAD - 8:n_rows - 8, :]
        pos = c * T + lax.broadcasted_iota(jnp.int32, (T, BW), 0)
        cnt = jnp.minimum(pos + 1, _pool_window_row()).astype(F32)
        dpool = _pool_select(w2, w4, w8, w16) / cnt - pu
        poolh[0:POOL_PAD, :] = poolh[T:n_rows, :]
        return _dot(dpool, wpool_ref[...]) * vec_ref[V_MISC:V_MISC + 1, 0:256]

    def conv_mix(xbc_raw):
        convh[SUBLANES:SUBLANES + T, :] = xbc_raw
        acc = vec_ref[V_CONVB:V_CONVB + 1, 0:CONV_DIM]
        for kk in range(CONV_K):
            acc = acc + convh[5 + kk:5 + kk + T, :] * vec_ref[V_CONVW + kk:V_CONVW + kk + 1, 0:CONV_DIM]
        convh[0:SUBLANES, :] = convh[T:T + SUBLANES, :]
        return _silu(acc)

    v_sm, sp_all, lsg, a_row = _small_block(jnp.dot(xb, wsm_ref[...], preferred_element_type=F32), vec_ref)
    lane_s = lax.broadcasted_iota(jnp.int32, (T, LANES), 1)
    g_all = jnp.where(lane_s < 4, sp_all * a_row, jnp.where(lane_s < 8, v_sm, jnp.where(lane_s < 12, lsg, 0.0)))
    tril = causal.astype(BF16)
    g_hi = g_all.astype(BF16)
    g_r1 = g_all - g_hi.astype(F32)
    g_mid = g_r1.astype(BF16)
    g_lo = (g_r1 - g_mid.astype(F32)).astype(BF16)
    cums = []
    for s in range(T // Q):
        rs = slice(s * Q, (s + 1) * Q)
        cum = (jnp.dot(tril, g_hi[rs], preferred_element_type=F32)
               + jnp.dot(tril, g_mid[rs], preferred_element_type=F32)
               + jnp.dot(tril, g_lo[rs], preferred_element_type=F32))
        cums.append((cum, cum.T, g_all[rs].T))

    proj = {}
    dyn0 = jnp.minimum(c, 0)

    half = D_MODEL // 2

    def gate(b, part):
        lo = part * half
        gsc[dyn0 + b, :, lo:lo + half] = jnp.tanh(post(O_GATE + b * D_MODEL + lo, half))

    fillers = [
        lambda: proj.update(pu=pre(O_PU, BW)),
        lambda: proj.update(xbc=pre(O_XBC, CONV_DIM)),
        lambda: proj.update(q=pre(O_Q, BW)),
        lambda: proj.update(k=pre(O_K, BW) * (HD ** -0.5)),
        lambda: proj.update(v=pre(O_V, BW)),
        lambda: proj.update(zs=[post(O_Z + b * BW, BW) for b in range(4)]),
        lambda: proj.update(og=post(O_OG, BW)),
    ] + [functools.partial(gate, b, part) for b in range(4) for part in range(2)]

    def su_piece(piece):
        return jnp.dot(xb[piece * Q:(piece + 1) * Q], wpre_ref[:, O_SU:O_SU + BW], preferred_element_type=F32)

    y_s5 = _s5_chunk_scan(su_piece, T, fillers, lamp_ref, bbd_ref, cbd_ref, vec_ref, s5cr, s5ci)
    zs = proj["zs"]
    y_pool = pool_mix(proj["pu"])
    xbc = conv_mix(proj["xbc"])
    gates = [gsc.at[dyn0 + b] for b in range(4)]

    y_ssd_parts, h_m_parts = [], []
    for s in range(T // Q):
        y_c, h_c = _ssd_mlstm_chunk(
            xbc[s * Q:(s + 1) * Q], proj["q"][s * Q:(s + 1) * Q], proj["k"][s * Q:(s + 1) * Q],
            proj["v"][s * Q:(s + 1) * Q], sp_all[s * Q:(s + 1) * Q], g_all[s * Q:(s + 1) * Q],
            cums[s], causal, vec_ref, ssd_st, mc_st, nrow, mrow)
        y_ssd_parts.append(y_c)
        h_m_parts.append(h_c)
    y_ssd = jnp.concatenate(y_ssd_parts, axis=0)
    y_m = _sigmoid(proj["og"]) * jnp.concatenate(h_m_parts, axis=0)
    y_s5 = _s5_glu(y_s5, wglu_ref)

    z_acts = [_silu_of_half(z) for z in zs]
    y_ref[0] = _merge_tail(x, z_acts, [y_pool, y_ssd, y_m, y_s5], gates, wbr_ref, wout_ref, vec_ref)

    @pl.when(c == last)
    def _():
        pool_o[0] = poolh[POOL_PAD - POOL_BUF:POOL_PAD, :]
        conv_o[0] = convh[5:8, :]
        ssd_o[0] = ssd_st[...]
        mc_o[0] = mc_st[...]
        mn_o[0] = nrow[0:1, :]
        mm_o[0] = mrow[0:1, :]
        s5re_o[0] = s5cr[0:1, :]
        s5im_o[0] = s5ci[0:1, :]


def _ssd_mlstm_chunk(xbc, q, k, v, sp, g_blk, cums, causal, vec_ref, ssd_st, mc_st, nrow, mrow):
    T = xbc.shape[0]
    cum, cum_t, g_t = cums
    xs = xbc[:, 0:BW]
    dt_cols = [sp[:, h:h + 1] for h in range(HEADS)]
    acs_cols = [cum[:, h:h + 1] for h in range(HEADS)]
    acs_last = [cum[T - 1:T, h:h + 1] for h in range(HEADS)]
    xdt = xs * _expand_heads(dt_cols)
    xw_t = (xdt * _expand_heads([jnp.exp(acs_last[h] - acs_cols[h]) for h in range(HEADS)])).T
    d_ssd = vec_ref[V_MISC:V_MISC + 1, 256:512]
    m_row = mrow[0:1, :]
    n_row = nrow[0:1, :]
    hsl = [slice(h * HD, (h + 1) * HD) for h in range(HEADS)]
    b_gs = [xbc[:, BW + g * SSD_STATE:BW + (g + 1) * SSD_STATE] for g in range(2)]
    c_gs = [xbc[:, BW + 128 + g * SSD_STATE:BW + 128 + (g + 1) * SSD_STATE] for g in range(2)]

    cbs = [_dot_nt(c_gs[g], b_gs[g]) for g in range(2)]
    off_all = jnp.concatenate(
        [_dot_nt(c_gs[g], ssd_st[2 * g:2 * g + 2].reshape(2 * HD, SSD_STATE)) for g in range(2)], axis=1)
    qks = [_dot_nt(q[:, hsl[h]], k[:, hsl[h]]) for h in range(HEADS)]
    qcs = [_dot(q[:, hsl[h]], mc_st[h]) for h in range(HEADS)]
    gms = [cbs[h // 2] * jnp.exp(jnp.where(causal, acs_cols[h] - cum_t[h:h + 1, :], -jnp.inf))
           for h in range(HEADS)]
    b_cols = [cum[:, 8 + h:9 + h] for h in range(HEADS)]
    m_prevs = [m_row[:, h:h + 1] for h in range(HEADS)]
    dlogs = [jnp.where(causal, b_cols[h] - cum_t[8 + h:9 + h, :] + g_t[4 + h:5 + h, :], -jnp.inf)
             for h in range(HEADS)]
    inters = [b_cols[h] + m_prevs[h] for h in range(HEADS)]
    m_ts = [jnp.maximum(inters[h], jnp.max(dlogs[h], axis=1, keepdims=True)) for h in range(HEADS)]
    ss = [qks[h] * jnp.exp(dlogs[h] - m_ts[h]) for h in range(HEADS)]
    wis = [jnp.exp(inters[h] - m_ts[h]) for h in range(HEADS)]
    y_diag_all = jnp.concatenate([_dot(gms[h], xdt[:, hsl[h]]) for h in range(HEADS)], axis=1)
    svs = [_dot(ss[h], v[:, hsl[h]]) for h in range(HEADS)]
    y_off = off_all * _expand_heads([jnp.exp(acs_cols[h]) for h in range(HEADS)])
    y_ssd = y_diag_all + y_off + d_ssd * xs
    h_heads, wk_parts, decays, m_news = [], [], [], []
    for h in range(HEADS):
        num = svs[h] + wis[h] * qcs[h]
        den = (jnp.sum(ss[h], axis=1, keepdims=True)
               + wis[h] * jnp.sum(q[:, hsl[h]] * n_row[:, hsl[h]], axis=1, keepdims=True))
        h_heads.append(num / jnp.maximum(jnp.abs(den), jnp.exp(-m_ts[h])))
        m_new = m_ts[h][T - 1:T, :]
        b_last = b_cols[h][T - 1:T, :]
        wk_parts.append(jnp.exp(b_last - b_cols[h] + g_blk[:, 4 + h:5 + h] - m_new))
        decays.append(jnp.exp(b_last + m_prevs[h] - m_new))
        m_news.append(m_new)
    for h in range(HEADS):
        ssd_st[h] = jnp.exp(acs_last[h]) * ssd_st[h] + _dot(xw_t[hsl[h], :], b_gs[h // 2])
    lane_m = lax.broadcasted_iota(jnp.int32, (1, LANES), 1)
    m_row_new = jnp.zeros((1, LANES), F32)
    for h in range(HEADS):
        m_row_new = jnp.where(lane_m == h, jnp.broadcast_to(m_news[h], (1, LANES)), m_row_new)
    mrow[0:1, :] = m_row_new
    wk = k * _expand_heads(wk_parts)
    wk_t = wk.T
    for h in range(HEADS):
        mc_st[h] = decays[h] * mc_st[h] + _dot(wk_t[h * HD:(h + 1) * HD, :], v[:, h * HD:(h + 1) * HD])
    nrow[0:1, :] = _expand_heads(decays) * n_row + jnp.sum(wk, axis=0, keepdims=True)
    return y_ssd, jnp.concatenate(h_heads, axis=1)


def _s5_chunk_scan(su_piece, T, fillers, lamp_ref, bbd_ref, cbd_ref, vec_ref, s5cr, s5ci):
    p_re = lamp_ref[R_P:R_P + SUBLANES, :]
    p_im = lamp_ref[R_P + SUBLANES:R_P + 2 * SUBLANES, :]
    cr = s5cr[0:1, :]
    ci = s5ci[0:1, :]
    blk_re = []
    blk_im = []
    pieces = T // CHUNK
    nblk = CHUNK // SUBLANES
    slots = 3 * pieces
    filler_at = {}
    for i, f in enumerate(fillers):
        filler_at.setdefault((i * slots) // len(fillers), []).append(f)
    in_block = lax.broadcasted_iota(jnp.int32, (SUBLANES, BW), 0) >= 1
    su_parts = []
    for piece in range(pieces):
        su_p = su_piece(piece)
        su_parts.append(su_p)
        su_prev = jnp.where(in_block, pltpu.roll(su_p.reshape(nblk, SUBLANES, BW), 1, 1), 0.0).reshape(CHUNK, BW)
        bu = _dot(jnp.concatenate([su_p, su_prev], axis=1), bbd_ref[...])
        h_re = bu[:, 0:S5_W].reshape(nblk, SUBLANES, S5_W)
        h_im = bu[:, S5_W:2 * S5_W].reshape(nblk, SUBLANES, S5_W)
        for i, d in ((0, 2), (1, 4)):
            base = R_M + 16 * (i + 1)
            m_re = lamp_ref[base:base + SUBLANES, :]
            m_im = lamp_ref[base + SUBLANES:base + 2 * SUBLANES, :]
            r_re = pltpu.roll(h_re, d, 1)
            r_im = pltpu.roll(h_im, d, 1)
            h_re, h_im = h_re + (m_re * r_re - m_im * r_im), h_im + (m_re * r_im + m_im * r_re)
            for f in filler_at.get(3 * piece + i, ()):
                f()
        for j in range(nblk):
            br = h_re[j] + (p_re * cr - p_im * ci)
            bi = h_im[j] + (p_re * ci + p_im * cr)
            blk_re.append(br)
            blk_im.append(bi)
            cr = br[SUBLANES - 1:SUBLANES, :]
            ci = bi[SUBLANES - 1:SUBLANES, :]
        for f in filler_at.get(3 * piece + 2, ()):
            f()
    s5cr[0:1, :] = cr
    s5ci[0:1, :] = ci
    h_all = jnp.concatenate([jnp.concatenate(blk_re, axis=0), jnp.concatenate(blk_im, axis=0)], axis=1)
    return _dot(h_all, cbd_ref[...]) + vec_ref[V_MISC:V_MISC + 1, 512:768] * jnp.concatenate(su_parts, axis=0)


def _const_spec(shape, layer):
    nd = len(shape)
    return pl.BlockSpec((None,) + tuple(shape), lambda *_: (layer,) + (0,) * nd,
                        pipeline_mode=pl.Buffered(1))


def _w_spec(width, block, layer):
    return pl.BlockSpec((None, D_MODEL, width), lambda *_: (layer, 0, block), pipeline_mode=pl.Buffered(1))


def _prompt_layer(x, layer, p):
    bsz, seq, _ = x.shape
    T = TILE if seq % TILE == 0 else CHUNK
    nc = seq // T

    def bmap(nd):
        return lambda b, c: (b,) + (0,) * nd

    out_shape = (
        jax.ShapeDtypeStruct((bsz, seq, D_MODEL), F32),
        jax.ShapeDtypeStruct((bsz, POOL_BUF, BW), F32),
        jax.ShapeDtypeStruct((bsz, CONV_K - 1, CONV_DIM), F32),
        jax.ShapeDtypeStruct((bsz, HEADS, HD, SSD_STATE), F32),
        jax.ShapeDtypeStruct((bsz, HEADS, HD, HD), F32),
        jax.ShapeDtypeStruct((bsz, 1, BW), F32),
        jax.ShapeDtypeStruct((bsz, 1, LANES), F32),
        jax.ShapeDtypeStruct((bsz, 1, S5_W), F32),
        jax.ShapeDtypeStruct((bsz, 1, S5_W), F32),
    )
    out_specs = (
        pl.BlockSpec((1, T, D_MODEL), lambda b, c: (b, c, 0)),
        pl.BlockSpec((1, POOL_BUF, BW), bmap(2)),
        pl.BlockSpec((1, CONV_K - 1, CONV_DIM), bmap(2)),
        pl.BlockSpec((1, HEADS, HD, SSD_STATE), bmap(3)),
        pl.BlockSpec((1, HEADS, HD, HD), bmap(3)),
        pl.BlockSpec((1, 1, BW), bmap(2)),
        pl.BlockSpec((1, 1, LANES), bmap(2)),
        pl.BlockSpec((1, 1, S5_W), bmap(2)),
        pl.BlockSpec((1, 1, S5_W), bmap(2)),
    )
    in_specs = [
        pl.BlockSpec((1, T, D_MODEL), lambda b, c: (b, c, 0)),
        _w_spec(PRE_W, PRE_BLOCK, layer),
        _w_spec(POST_W, 0, layer),
        _const_spec((D_MODEL, LANES), layer),
        _const_spec((16, D_MODEL), layer),
        _const_spec((BW, BW), layer),
        _const_spec((LAMP_ROWS, S5_W), layer),
        _const_spec((2 * BW, 2 * S5_W), layer),
        _const_spec((2 * S5_W, BW), layer),
        _const_spec((BW, 2 * BW), layer),
        _const_spec((4, BW, D_MODEL), layer),
        _const_spec((D_MODEL, D_MODEL), layer),
    ]
    scratch = [
        pltpu.VMEM((T + POOL_PAD, BW), F32),
        pltpu.VMEM((3, T + POOL_PAD, BW), F32),
        pltpu.VMEM((T + SUBLANES, CONV_DIM), F32),
        pltpu.VMEM((HEADS, HD, SSD_STATE), F32),
        pltpu.VMEM((HEADS, HD, HD), F32),
        pltpu.VMEM((SUBLANES, BW), F32),
        pltpu.VMEM((SUBLANES, LANES), F32),
        pltpu.VMEM((SUBLANES, S5_W), F32),
        pltpu.VMEM((SUBLANES, S5_W), F32),
        pltpu.VMEM((4, T, D_MODEL), F32),
    ]
    return pl.pallas_call(
        _prompt_kernel,
        grid=(bsz, nc),
        in_specs=in_specs,
        out_specs=out_specs,
        out_shape=out_shape,
        scratch_shapes=scratch,
        compiler_params=pltpu.CompilerParams(dimension_semantics=("arbitrary", "arbitrary"),
                                             vmem_limit_bytes=VMEM_LIMIT),
        name="prompt_layer",
    )(x, p["wall"], p["wall"], p["wsm"], p["vec"], p["wpool"], p["lamp"], p["bbd"], p["cbd"],
      p["wglu"], p["wbr"], p["wout"])


RW_BC, RW_XDT, RW_Q, RW_WK, RW_V, RW_SDEC, RW_MDEC, RW_W = 0, 256, 512, 768, 1024, 1280, 1408, 1536
MID_W = 7 * BW


def _sample_pre_kernel(x_ref, wpre_ref, wsm_ref, vec_ref, wpool_ref, lamp_ref, bbd_ref, cbd_ref,
                       pool_ref, conv_ref, n_ref, m_ref, s5re_ref, s5im_ref,
                       rows_o, mid_o, pool_o, conv_o, n_o, m_o, s5re_o, s5im_o):
    x = x_ref[...]
    nb = x.shape[0]
    pp = _dot(x, wpre_ref[...])
    pu = pp[:, O_PU:O_PU + BW]
    xbc_raw = pp[:, O_XBC:O_XBC + CONV_DIM]
    q = pp[:, O_Q:O_Q + BW]
    k = pp[:, O_K:O_K + BW] * (HD ** -0.5)
    v = pp[:, O_V:O_V + BW]
    su = pp[:, O_SU:O_SU + BW]

    def ext(j):
        return pool_ref[j]

    w2 = pu + ext(14)
    w4 = w2 + ext(13) + ext(12)
    w8 = w4
    for j in range(11, 7, -1):
        w8 = w8 + ext(j)
    w16 = w8
    for j in range(7, -1, -1):
        w16 = w16 + ext(j)
    cnt = jnp.minimum(PAST_LEN + 1, _pool_window_row()).astype(F32)
    dpool = _pool_select(w2, w4, w8, w16) / cnt - pu
    y_pool = _dot(dpool, wpool_ref[...]) * vec_ref[V_MISC:V_MISC + 1, 0:256]
    for j in range(POOL_BUF - 1):
        pool_o[j] = pool_ref[j + 1]
    pool_o[POOL_BUF - 1] = pu

    acc = vec_ref[V_CONVB:V_CONVB + 1, 0:CONV_DIM]
    for kk in range(CONV_K - 1):
        acc = acc + conv_ref[kk] * vec_ref[V_CONVW + kk:V_CONVW + kk + 1, 0:CONV_DIM]
    acc = acc + xbc_raw * vec_ref[V_CONVW + CONV_K - 1:V_CONVW + CONV_K, 0:CONV_DIM]
    for kk in range(CONV_K - 2):
        conv_o[kk] = conv_ref[kk + 1]
    conv_o[CONV_K - 2] = xbc_raw
    xbc = _silu(acc)
    xs = xbc[:, 0:BW]
    bc = xbc[:, BW:2 * BW]

    v_sm, sp, lsg, a_row = _small_block(_dot(x, wsm_ref[...]), vec_ref)
    sdec_blk = jnp.exp(sp * a_row)
    dt_cols = [sp[:, h:h + 1] for h in range(HEADS)]
    xdt = xs * _expand_heads(dt_cols)
    cb_cols = []
    for g in range(2):
        b_g = bc[:, g * SSD_STATE:(g + 1) * SSD_STATE]
        c_g = bc[:, 128 + g * SSD_STATE:128 + (g + 1) * SSD_STATE]
        cb_cols.append(jnp.sum(c_g * b_g, axis=1, keepdims=True))
    yssd_part = (_expand_heads([cb_cols[h // 2] for h in range(HEADS)]) * xdt
                 + vec_ref[V_MISC:V_MISC + 1, 256:512] * xs)
    sdec_exp = _expand_heads([sdec_blk[:, h:h + 1] for h in range(HEADS)])

    n0 = n_ref[...].T
    m0 = jnp.concatenate([m_ref[...], jnp.zeros((SUBLANES - HEADS, nb), F32)], axis=0).T
    s_cols, wi_cols, dw_cols, dmax_cols, m_cols = [], [], [], [], []
    for h in range(HEADS):
        sl = slice(h * HD, (h + 1) * HD)
        ig = v_sm[:, 4 + h:5 + h]
        lf = lsg[:, 8 + h:9 + h]
        inter = lf + m0[:, h:h + 1]
        m_t = jnp.maximum(inter, (lf - lf) + ig)
        dw = jnp.exp((lf - lf) + ig - m_t)
        wi = jnp.exp(inter - m_t)
        s = jnp.sum(q[:, sl] * k[:, sl], axis=1, keepdims=True) * dw
        den = s + wi * jnp.sum(q[:, sl] * n0[:, sl], axis=1, keepdims=True)
        s_cols.append(s)
        wi_cols.append(wi)
        dw_cols.append(dw)
        dmax_cols.append(jnp.maximum(jnp.abs(den), jnp.exp(-m_t)))
        m_cols.append(m_t)
    wi_exp = _expand_heads(wi_cols)
    wk = _expand_heads(dw_cols) * k
    n_o[...] = (wi_exp * n0 + wk).T
    lane = lax.broadcasted_iota(jnp.int32, (nb, LANES), 1)
    m_blk = jnp.zeros((nb, LANES), F32)
    wi_blk = jnp.zeros((nb, LANES), F32)
    for h in range(HEADS):
        m_blk = jnp.where(lane == h, jnp.broadcast_to(m_cols[h], (nb, LANES)), m_blk)
        wi_blk = jnp.where(lane == h, jnp.broadcast_to(wi_cols[h], (nb, LANES)), wi_blk)
    m_o[...] = m_blk.T[0:HEADS, :]

    bu = _dot(su, bbd_ref[0:BW, :])
    lbr = lamp_ref[R_LAM:R_LAM + 1, :]
    lbi = lamp_ref[R_LAM + 1:R_LAM + 2, :]
    h0r = s5re_ref[...].T
    h0i = s5im_ref[...].T
    h_re = bu[:, 0:S5_W] + (lbr * h0r - lbi * h0i)
    h_im = bu[:, S5_W:2 * S5_W] + (lbr * h0i + lbi * h0r)
    s5re_o[...] = h_re.T
    s5im_o[...] = h_im.T
    y_s5 = _dot(jnp.concatenate([h_re, h_im], axis=1), cbd_ref[...]) + vec_ref[V_MISC:V_MISC + 1, 512:768] * su

    rows_o[...] = jnp.concatenate([bc, xdt, q, wk, v, sdec_blk, wi_blk], axis=1).T
    mid_o[...] = jnp.concatenate([y_pool, yssd_part, sdec_exp, _expand_heads(s_cols) * v, wi_exp,
                                  _expand_heads(dmax_cols), y_s5], axis=1)


def _sample_pre(x, layer, p, pool2, conv2, n2, m2, s5re2, s5im2):
    nb = x.shape[0]
    full = lambda a: pl.BlockSpec(a.shape, lambda *_: (0,) * a.ndim)
    of_layer = lambda a: pl.BlockSpec((None,) + a.shape[1:], lambda *_: (layer,) + (0,) * (a.ndim - 1))
    out_shape = (
        jax.ShapeDtypeStruct((RW_W, nb), F32),
        jax.ShapeDtypeStruct((nb, MID_W), F32),
        jax.ShapeDtypeStruct(pool2.shape[1:], F32),
        jax.ShapeDtypeStruct(conv2.shape[1:], F32),
        jax.ShapeDtypeStruct((BW, nb), F32),
        jax.ShapeDtypeStruct((HEADS, nb), F32),
        jax.ShapeDtypeStruct((S5_W, nb), F32),
        jax.ShapeDtypeStruct((S5_W, nb), F32),
    )
    in_specs = [
        full(x),
        _w_spec(PRE_W, PRE_BLOCK, layer),
        _const_spec((D_MODEL, LANES), layer),
        _const_spec((16, D_MODEL), layer),
        _const_spec((BW, BW), layer),
        _const_spec((LAMP_ROWS, S5_W), layer),
        _const_spec((2 * BW, 2 * S5_W), layer),
        _const_spec((2 * S5_W, BW), layer),
        of_layer(pool2), of_layer(conv2), of_layer(n2), of_layer(m2), of_layer(s5re2), of_layer(s5im2),
    ]
    return pl.pallas_call(
        _sample_pre_kernel,
        grid=(1,),
        in_specs=in_specs,
        out_specs=tuple(pl.BlockSpec(s.shape, lambda *_, nd=len(s.shape): (0,) * nd) for s in out_shape),
        out_shape=out_shape,
        compiler_params=pltpu.CompilerParams(dimension_semantics=("arbitrary",),
                                             vmem_limit_bytes=VMEM_LIMIT),
        name="sample_pre",
    )(x, p["wall"], p["wsm"], p["vec"], p["wpool"], p["lamp"], p["bbd"], p["cbd"],
      pool2, conv2, n2, m2, s5re2, s5im2)


def _sample_state_kernel(first_layer, rows_ref, ssd_ref, mc_ref, *rest):
    h = pl.program_id(0)
    part = pl.program_id(1)
    if first_layer:
        yoff_ref, qc_ref, ssd_o, mc_o = rest
        ssd_o[1] = ssd_ref[1]
        mc_o[1] = mc_ref[1]
        ssd_in, mc_in, ssd_out, mc_out = ssd_ref.at[0], mc_ref.at[0], ssd_o.at[0], mc_o.at[0]
    else:
        _, _, yoff_ref, qc_ref, ssd_out, mc_out = rest
        ssd_in, mc_in = ssd_ref, mc_ref
    g64 = pl.multiple_of((h // 2) * SSD_STATE, SSD_STATE)
    h64 = pl.multiple_of(h * HD, HD)
    b_t = rows_ref[pl.ds(RW_BC + g64, SSD_STATE), :]
    c_t = rows_ref[pl.ds(RW_BC + 128 + g64, SSD_STATE), :]
    v_t = rows_ref[pl.ds(RW_V + h64, HD), :]
    sdec = rows_ref[pl.ds(RW_SDEC + h, 1), :]
    mdec = rows_ref[pl.ds(RW_MDEC + h, 1), :]
    base = h64 + part * STATE_ROWS

    def body(i, acc):
        hs = ssd_in[i]
        yoff_ref[pl.ds(i, 1), :] = jnp.sum(hs * c_t, axis=0, keepdims=True)
        ssd_out[i] = sdec * hs + rows_ref[pl.ds(RW_XDT + base + i, 1), :] * b_t
        cs = mc_in[i]
        mc_out[i] = mdec * cs + rows_ref[pl.ds(RW_WK + base + i, 1), :] * v_t
        return acc + rows_ref[pl.ds(RW_Q + base + i, 1), :] * cs

    acc = lax.fori_loop(0, STATE_ROWS, body, jnp.zeros((HD, rows_ref.shape[1]), F32), unroll=4)

    @pl.when(part == 0)
    def _():
        qc_ref[...] = acc

    @pl.when(part != 0)
    def _():
        qc_ref[...] = qc_ref[...] + acc


STATE_ROWS = 32


def _sample_state(rows_t, layer, ssd_all, mc_all, prev=None):
    nb = rows_t.shape[1]
    depth = ssd_all.shape[0]
    parts = HD // STATE_ROWS
    tile = (STATE_ROWS, HD, nb)
    rows_spec = pl.BlockSpec(rows_t.shape, lambda h, s: (0, 0))
    yoff_spec = pl.BlockSpec((STATE_ROWS, nb), lambda h, s: (h * parts + s, 0))
    qc_spec = pl.BlockSpec((HD, nb), lambda h, s: (h, 0))
    out_shape = (jax.ShapeDtypeStruct((BW, nb), F32),
                 jax.ShapeDtypeStruct((BW, nb), F32),
                 jax.ShapeDtypeStruct(ssd_all.shape, F32),
                 jax.ShapeDtypeStruct(mc_all.shape, F32))
    params = pltpu.CompilerParams(dimension_semantics=("arbitrary", "arbitrary"),
                                  vmem_limit_bytes=VMEM_LIMIT)
    if prev is None:
        assert layer == 0 and depth == 2
        both = pl.BlockSpec((depth, None) + tile, lambda h, s: (0, h, s, 0, 0))
        return pl.pallas_call(
            functools.partial(_sample_state_kernel, True),
            grid=(HEADS, parts),
            in_specs=[rows_spec, both, both],
            out_specs=(yoff_spec, qc_spec, both, both),
            out_shape=out_shape,
            compiler_params=params,
            name="sample_state_first",
        )(rows_t, ssd_all, mc_all)
    one = pl.BlockSpec((None, None) + tile, lambda h, s: (layer, h, s, 0, 0))
    anywhere = pl.BlockSpec(memory_space=pl.ANY)
    return pl.pallas_call(
        functools.partial(_sample_state_kernel, False),
        grid=(HEADS, parts),
        in_specs=[rows_spec, one, one, anywhere, anywhere],
        out_specs=(yoff_spec, qc_spec, one, one),
        out_shape=out_shape,
        input_output_aliases={3: 2, 4: 3},
        compiler_params=params,
        name="sample_state_next",
    )(rows_t, ssd_all, mc_all, prev[0], prev[1])


def _sample_post_kernel(x_ref, wpost_ref, vec_ref, mid_ref, yoff_ref, qc_ref, wglu_ref, wbr_ref, wout_ref, y_ref):
    x = x_ref[...]
    xb = x.astype(BF16)

    def post(lo, width):
        return jnp.dot(xb, wpost_ref[:, lo:lo + width], preferred_element_type=F32)

    def mid(i):
        return mid_ref[:, i * BW:(i + 1) * BW]

    y_pool, yssd_part, sdec_exp, sv, wi_exp, dmax_exp, y_s5 = [mid(i) for i in range(7)]
    y_ssd = yssd_part + yoff_ref[...].T * sdec_exp
    hh = (sv + wi_exp * qc_ref[...].T) / dmax_exp
    y_m = _sigmoid(post(O_OG, BW)) * hh
    y_s5 = _s5_glu(y_s5, wglu_ref)
    z_acts = [_silu_of_half(post(O_Z + b * BW, BW)) for b in range(4)]
    gates = [jnp.tanh(post(O_GATE + b * D_MODEL, D_MODEL)) for b in range(4)]
    y_ref[...] = _merge_tail(x, z_acts, [y_pool, y_ssd, y_m, y_s5], gates, wbr_ref, wout_ref, vec_ref)


def _sample_post(x, layer, p, mid, yoff_t, qc_t):
    nb = x.shape[0]
    full = lambda a: pl.BlockSpec(a.shape, lambda *_: (0,) * a.ndim)
    return pl.pallas_call(
        _sample_post_kernel,
        grid=(1,),
        in_specs=[full(x), _w_spec(POST_W, 0, layer), _const_spec((16, D_MODEL), layer),
                  full(mid), full(yoff_t), full(qc_t), _const_spec((BW, 2 * BW), layer),
                  _const_spec((4, BW, D_MODEL), layer), _const_spec((D_MODEL, D_MODEL), layer)],
        out_specs=pl.BlockSpec((nb, D_MODEL), lambda *_: (0, 0)),
        out_shape=jax.ShapeDtypeStruct((nb, D_MODEL), F32),
        compiler_params=pltpu.CompilerParams(dimension_semantics=("arbitrary",),
                                             vmem_limit_bytes=VMEM_LIMIT),
        name="sample_post",
    )(x, p["wall"], p["vec"], mid, yoff_t, qc_t, p["wglu"], p["wbr"], p["wout"])


def _w_permute_kernel(src_ref, halved_ref, w_hbm, out_ref, buf, sem):
    i = pl.program_id(0)
    depth = out_ref.shape[0]

    def copies(step, slot):
        start = src_ref[step]
        return [pltpu.make_async_copy(w_hbm.at[pl.ds(start, WCOLS), l, :], buf.at[slot, l], sem.at[slot, l])
                for l in range(depth)]

    @pl.when(i == 0)
    def _():
        for cp in copies(0, 0):
            cp.start()

    @pl.when(i + 1 < pl.num_programs(0))
    def _():
        for cp in copies(i + 1, (i + 1) % 2):
            cp.start()

    for cp in copies(i, i % 2):
        cp.wait()
    scale = jnp.where(halved_ref[i] == 1, 0.5, 1.0).astype(F32)
    for l in range(depth):
        out_ref[l] = (buf[i % 2, l].T * scale).astype(BF16)


def _w_permute(w_in):
    depth, rows, _ = w_in.shape
    w_t = jnp.transpose(w_in, (2, 0, 1))
    nblk = len(W_SRC)
    return pl.pallas_call(
        _w_permute_kernel,
        grid_spec=pltpu.PrefetchScalarGridSpec(
            num_scalar_prefetch=2, grid=(nblk,),
            in_specs=[pl.BlockSpec(memory_space=pl.ANY)],
            out_specs=pl.BlockSpec((depth, rows, WCOLS), lambda i, src, halved: (0, 0, i)),
            scratch_shapes=[pltpu.VMEM((2, depth, WCOLS, rows), F32), pltpu.SemaphoreType.DMA((2, depth))]),
        out_shape=jax.ShapeDtypeStruct((depth, rows, nblk * WCOLS), BF16),
        compiler_params=pltpu.CompilerParams(dimension_semantics=("arbitrary",)),
        name="w_permute",
    )(jnp.asarray(W_SRC, jnp.int32), jnp.asarray(W_HALVED, jnp.int32), w_t)


def _prepare_params(w_in, w_pool, pool_scale, conv_w, conv_b, dt_bias, a_log, d_ssd, ig_bias, fg_bias,
                    lam_re, lam_im, b_re, b_im, c_re, c_im, log_dt, d_s5, w_glu, w_br, w_out, ln_g, ln_b):
    depth = w_in.shape[0]
    wall = _w_permute(w_in)
    small_cols = [w_in[:, :, lo:lo + n] for lo, n in SMALL_SRC]
    wsm = jnp.concatenate(small_cols + [jnp.zeros((depth, D_MODEL, LANES - 12), w_in.dtype)], axis=2).astype(BF16)

    zeros = lambda n: jnp.zeros((depth, n), F32)
    misc = jnp.concatenate([pool_scale, jnp.repeat(d_ssd, HD, axis=1), d_s5,
                            dt_bias, ig_bias, fg_bias, zeros(LANES - 12),
                            a_log, zeros(LANES - HEADS)], axis=1)
    pad512 = lambda a: jnp.concatenate([a, jnp.zeros(a.shape[:-1] + (D_MODEL - CONV_DIM,), F32)], axis=-1)
    vec = jnp.concatenate([ln_g[:, None], ln_b[:, None], misc[:, None], pad512(conv_b)[:, None],
                           pad512(conv_w), jnp.zeros((depth, 16 - 4 - CONV_K, D_MODEL), F32)], axis=1)

    eye4 = jnp.eye(len(POOL_WINDOWS), dtype=F32)
    wpool = jnp.einsum("lgce,gh->lgche", w_pool, eye4).reshape(depth, BW, BW).astype(BF16)

    lam2 = jnp.stack([lam_re.reshape(depth, S5_W), lam_im.reshape(depth, S5_W)], axis=1)
    ldt = jnp.repeat(log_dt, S5_STATE, axis=1)[:, None]
    to_rows = lambda b: jnp.transpose(b, (0, 3, 1, 2)).reshape(depth, S5_CH, S5_W)
    bcat = jnp.concatenate([to_rows(b_re), to_rows(b_im)], axis=1)

    eye_g = jnp.eye(S5_GROUPS, dtype=F32)
    cb = lambda cm: jnp.einsum("lgcp,gh->lgphc", cm, eye_g).reshape(depth, S5_W, BW)
    cbd = jnp.concatenate([cb(c_re), -cb(c_im)], axis=1).astype(BF16)

    lamp, bbd = [], []
    for l in range(depth):
        lp, bb = _s5_prep(lam2[l], ldt[l], bcat[l])
        lamp.append(lp)
        bbd.append(bb)
    return dict(wall=wall, wsm=wsm, vec=vec, wpool=wpool, lamp=jnp.stack(lamp), bbd=jnp.stack(bbd),
                cbd=cbd, wglu=w_glu.astype(BF16), wbr=w_br.astype(BF16), wout=(0.5 * w_out).astype(BF16))


def kernel(x_prompt, x_sample, state_pool, state_ssd_conv, state_ssd, state_mlstm_C, state_mlstm_n, state_mlstm_m, state_s5_re, state_s5_im, w_in, w_pool, pool_scale, conv_w, conv_b, dt_bias, a_log, d_ssd, ig_bias, fg_bias, lam_re, lam_im, b_re, b_im, c_re, c_im, log_dt, d_s5, w_glu, w_br, w_out, ln_g, ln_b):
    p = _prepare_params(w_in, w_pool, pool_scale, conv_w, conv_b, dt_bias, a_log, d_ssd, ig_bias, fg_bias,
                        lam_re, lam_im, b_re, b_im, c_re, c_im, log_dt, d_s5, w_glu, w_br, w_out, ln_g, ln_b)
    depth = w_in.shape[0]
    bsz = x_prompt.shape[0]
    nb = x_sample.shape[0]
    yp = x_prompt
    ys = x_sample.reshape(nb, D_MODEL)
    outs_p = [[] for _ in range(8)]
    outs_s = [[] for _ in range(8)]
    mats_s = None
    pool_t = jnp.transpose(state_pool, (0, 2, 1, 3))
    conv_t = jnp.transpose(state_ssd_conv, (0, 2, 1, 3))
    ssd_t = jnp.transpose(state_ssd, (0, 2, 3, 4, 1))
    mc_t = jnp.transpose(state_mlstm_C, (0, 2, 3, 4, 1))
    n_t = jnp.transpose(state_mlstm_n, (0, 2, 3, 1)).reshape(depth, BW, nb)
    m_t = jnp.transpose(state_mlstm_m, (0, 2, 1))
    re_t = jnp.transpose(state_s5_re, (0, 2, 3, 1)).reshape(depth, S5_W, nb)
    im_t = jnp.transpose(state_s5_im, (0, 2, 3, 1)).reshape(depth, S5_W, nb)
    for l in range(depth):
        yp, pool_p, conv_p, ssd_p, mc_p, mn_p, mm_p, re_p, im_p = _prompt_layer(yp, l, p)
        new_p = (pool_p, conv_p, ssd_p, mc_p, mn_p.reshape(bsz, HEADS, HD), mm_p[:, 0, :HEADS],
                 re_p.reshape(bsz, S5_GROUPS, S5_STATE), im_p.reshape(bsz, S5_GROUPS, S5_STATE))

        rows_t, mid, pool_s, conv_s, n_s, m_s, re_s, im_s = _sample_pre(
            ys, l, p, pool_t, conv_t, n_t, m_t, re_t, im_t)
        yoff_t, qc_t, ssd_s, mc_s = _sample_state(rows_t, l, ssd_t, mc_t, mats_s)
        mats_s = (ssd_s, mc_s)
        ys = _sample_post(ys, l, p, mid, yoff_t, qc_t)
        new_s = (pool_s, conv_s, None, None, n_s, m_s, re_s, im_s)
        for i in range(8):
            outs_p[i].append(new_p[i])
            outs_s[i].append(new_s[i])
    sp = [jnp.stack(o, axis=0) for o in outs_p]
    st = [mats_s[i - 2] if i in (2, 3) else jnp.stack(outs_s[i], axis=0) for i in range(8)]
    ss = [jnp.transpose(st[0], (0, 2, 1, 3)), jnp.transpose(st[1], (0, 2, 1, 3)),
          jnp.transpose(st[2], (0, 4, 1, 2, 3)), jnp.transpose(st[3], (0, 4, 1, 2, 3)),
          jnp.transpose(st[4].reshape(depth, HEADS, HD, nb), (0, 3, 1, 2)),
          jnp.transpose(st[5], (0, 2, 1)),
          jnp.transpose(st[6].reshape(depth, S5_GROUPS, S5_STATE, nb), (0, 3, 1, 2)),
          jnp.transpose(st[7].reshape(depth, S5_GROUPS, S5_STATE, nb), (0, 3, 1, 2))]
    out = [yp, ys.reshape(nb, 1, D_MODEL)]
    for i in range(8):
        out.append(sp[i])
        out.append(ss[i])
    return tuple(out)
```

```python
import functools

import jax
import jax.numpy as jnp
from jax import lax
from jax.experimental import pallas as pl
from jax.experimental.pallas import tpu as pltpu

F32 = jnp.float32
BF16 = jnp.bfloat16

D_MODEL = 1024
DEPTH = 2
PAST_LEN = 16384
BW = 256
POOL_WINDOWS = (2, 4, 8, 16)
POOL_BUF = 15
POOL_PAD = 32
HEADS = 4
HD = 64
SSD_STATE = 64
CONV_K = 4
CONV_DIM = 512
S5_GROUPS = 16
S5_CH = 16
S5_STATE = 64
S5_W = S5_GROUPS * S5_STATE
CHUNK = 128
TILE = 512
ALPHA = (2.0 * DEPTH) ** 0.25
LN_EPS = 1e-5

SUBLANES = 8
LANES = 128
VMEM_LIMIT = 62 * 1024 * 1024

PRE_W = 1792
POST_W = 5376
O_PU, O_XBC, O_Q, O_K, O_V, O_SU = 0, 256, 768, 1024, 1280, 1536
O_Z, O_OG, O_GATE = 0, 1024, 1280
PRE_BLOCK = POST_W // PRE_W
assert POST_W == PRE_BLOCK * PRE_W
WCOLS = 256
_SRC_POST = [256, 1028, 2316, 2828, 2060] + [3084 + WCOLS * i for i in range(16)]
_SRC_PRE = [0, 512, 768, 1284, 1540, 1796, 2572]
W_SRC = tuple(_SRC_POST + _SRC_PRE)
W_HALVED = tuple([1, 1, 1, 1, 0] + [1] * 16 + [0] * 7)
SMALL_SRC = ((1024, 4), (2052, 4), (2056, 4))

R_LAM = 0
R_P = 8
R_M = 24
LAMP_ROWS = R_M + 3 * 16

V_LNG, V_LNB, V_MISC, V_CONVB, V_CONVW = 0, 1, 2, 3, 4


def _dot(a, b):
    return jnp.dot(a.astype(BF16), b.astype(BF16), preferred_element_type=F32)


def _dot_nt(a, b):
    return lax.dot_general(a.astype(BF16), b.astype(BF16), (((1,), (1,)), ((), ())),
                           preferred_element_type=F32)


def _softplus(x):
    return jnp.logaddexp(x, 0.0)


def _sigmoid(x):
    return 0.5 * jnp.tanh(0.5 * x) + 0.5


def _silu(x):
    return x * _sigmoid(x)


def _silu_of_half(xh):
    return xh * (jnp.tanh(xh) + 1.0)


def _expand_heads(cols, width=HD):
    rows = cols[0].shape[0]
    lane = lax.broadcasted_iota(jnp.int32, (rows, HEADS * width), 1)
    out = jnp.broadcast_to(cols[HEADS - 1], (rows, HEADS * width))
    for h in range(HEADS - 2, -1, -1):
        out = jnp.where(lane < (h + 1) * width, jnp.broadcast_to(cols[h], (rows, HEADS * width)), out)
    return out


def _pool_select(w2, w4, w8, w16):
    lane = lax.broadcasted_iota(jnp.int32, w2.shape, 1)
    return jnp.where(lane < 64, w2, jnp.where(lane < 128, w4, jnp.where(lane < 192, w8, w16)))


def _pool_window_row():
    lane = lax.broadcasted_iota(jnp.int32, (1, BW), 1)
    return jnp.where(lane < 64, 2, jnp.where(lane < 128, 4, jnp.where(lane < 192, 8, 16)))


def _small_block(sm, vec_ref):
    bias = vec_ref[V_MISC:V_MISC + 1, 768:896]
    alog = vec_ref[V_MISC:V_MISC + 1, 896:1024]
    v = sm + bias
    sp = _softplus(v)
    lsg = -_softplus(-v)
    a_row = -jnp.exp(alog)
    return v, sp, lsg, a_row


TAIL_ROWS = 256


def _merge_tail(x, z_acts, ys, gate_tanh, wbr_ref, wout_ref, vec_ref):
    g = vec_ref[V_LNG:V_LNG + 1, :]
    beta = vec_ref[V_LNB:V_LNB + 1, :]
    n_rows = x.shape[0]
    halves = 2 if n_rows >= 2 * TAIL_ROWS else 1
    outs = []
    for i in range(halves):
        sl = slice(i * n_rows // halves, (i + 1) * n_rows // halves)
        merged = None
        for b in range(4):
            br = ys[b][sl] * z_acts[b][sl]
            pb = _dot(br, wbr_ref[b])
            gb = gate_tanh[b][sl, :] + 1.0
            merged = gb * pb if merged is None else merged + gb * pb
        out = _dot(merged, wout_ref[...])
        r = ALPHA * x[sl] + out
        mu = jnp.mean(r, axis=-1, keepdims=True)
        var = jnp.mean(jnp.square(r - mu), axis=-1, keepdims=True)
        outs.append((r - mu) * lax.rsqrt(var + LN_EPS) * g + beta)
    return outs[0] if halves == 1 else jnp.concatenate(outs, axis=0)


def _s5_glu(y_s5, wglu_ref):
    glu = _dot(jax.nn.gelu(y_s5), wglu_ref[...])
    return glu[:, :BW] * _sigmoid(glu[:, BW:])


def _s5_prep_kernel(lam_ref, ldt_ref, b_ref, lamp_ref, bbd_ref):
    lr = lam_ref[0:1, :]
    li = lam_ref[1:2, :]
    step = jnp.exp(ldt_ref[0:1, :])
    e = jnp.exp(lr * step)
    lbr = e * jnp.cos(li * step)
    lbi = e * jnp.sin(li * step)
    den = lr * lr + li * li
    nr = lbr - 1.0
    qr = (nr * lr + lbi * li) / den
    qi = (lbi * lr - nr * li) / den
    bre = b_ref[0:S5_CH, :]
    bim = b_ref[S5_CH:2 * S5_CH, :]
    bbr = qr * bre - qi * bim
    bbi = qr * bim + qi * bre
    for part, (vr, vi) in enumerate(((bbr, bbi), (lbr * bbr - lbi * bbi, lbr * bbi + lbi * bbr))):
        bb = jnp.concatenate([vr, vi], axis=1)
        tiled = jnp.concatenate([bb] * S5_GROUPS, axis=0)
        rowg = lax.broadcasted_iota(jnp.int32, tiled.shape, 0) // S5_CH
        colg = (lax.broadcasted_iota(jnp.int32, tiled.shape, 1) % S5_W) // S5_STATE
        bbd_ref[part * BW:(part + 1) * BW, :] = jnp.where(rowg == colg, tiled, 0.0).astype(BF16)

    pows = [(lbr, lbi)]
    for _ in range(7):
        pr, pi = pows[-1]
        pows.append((pr * lbr - pi * lbi, pr * lbi + pi * lbr))
    row = lax.broadcasted_iota(jnp.int32, (SUBLANES, S5_W), 0)

    def bcast(v):
        return jnp.broadcast_to(v, (SUBLANES, S5_W))

    p_re = bcast(pows[7][0])
    p_im = bcast(pows[7][1])
    for j in range(6, -1, -1):
        p_re = jnp.where(row == j, bcast(pows[j][0]), p_re)
        p_im = jnp.where(row == j, bcast(pows[j][1]), p_im)
    lamp_ref[R_LAM:R_LAM + SUBLANES, :] = jnp.where(row == 0, bcast(lbr), jnp.where(row == 1, bcast(lbi), 0.0))
    lamp_ref[R_P:R_P + SUBLANES, :] = p_re
    lamp_ref[R_P + SUBLANES:R_P + 2 * SUBLANES, :] = p_im
    for i, d in enumerate((1, 2, 4)):
        base = R_M + 16 * i
        lamp_ref[base:base + SUBLANES, :] = jnp.where(row >= d, bcast(pows[d - 1][0]), 0.0)
        lamp_ref[base + SUBLANES:base + 2 * SUBLANES, :] = jnp.where(row >= d, bcast(pows[d - 1][1]), 0.0)


def _s5_prep(lam2, ldt, bcat):
    return pl.pallas_call(
        _s5_prep_kernel,
        out_shape=(jax.ShapeDtypeStruct((LAMP_ROWS, S5_W), F32),
                   jax.ShapeDtypeStruct((2 * BW, 2 * S5_W), BF16)),
        name="s5_prep",
    )(lam2, ldt, bcat)


def _prompt_kernel(x_ref, wpre_ref, wpost_ref, wsm_ref, vec_ref, wpool_ref, lamp_ref, bbd_ref, cbd_ref,
                   wglu_ref, wbr_ref, wout_ref,
                   y_ref, pool_o, conv_o, ssd_o, mc_o, mn_o, mm_o, s5re_o, s5im_o,
                   poolh, poolw, convh, ssd_st, mc_st, nrow, mrow, s5cr, s5ci, gsc):
    T = x_ref.shape[1]
    Q = CHUNK
    c = pl.program_id(1)
    last = pl.num_programs(1) - 1

    @pl.when(c == 0)
    def _():
        poolh[0:POOL_PAD, :] = jnp.zeros((POOL_PAD, BW), F32)
        convh[0:SUBLANES, :] = jnp.zeros((SUBLANES, CONV_DIM), F32)
        ssd_st[...] = jnp.zeros(ssd_st.shape, F32)
        mc_st[...] = jnp.zeros(mc_st.shape, F32)
        nrow[...] = jnp.zeros(nrow.shape, F32)
        mrow[...] = jnp.zeros(mrow.shape, F32)
        s5cr[...] = jnp.zeros(s5cr.shape, F32)
        s5ci[...] = jnp.zeros(s5ci.shape, F32)

    x = x_ref[0]
    xb = x.astype(BF16)

    def pre(lo, width):
        return jnp.dot(xb, wpre_ref[:, lo:lo + width], preferred_element_type=F32)

    def post(lo, width):
        return jnp.dot(xb, wpost_ref[:, lo:lo + width], preferred_element_type=F32)

    rows_i = lax.broadcasted_iota(jnp.int32, (Q, Q), 0)
    cols_i = lax.broadcasted_iota(jnp.int32, (Q, Q), 1)
    causal = rows_i >= cols_i

    n_rows = POOL_PAD + T

    def pool_mix(pu):
        poolh[POOL_PAD:n_rows, :] = pu
        poolw[0, 8:n_rows, :] = poolh[8:n_rows, :] + poolh[7:n_rows - 1, :]
        poolw[1, 16:n_rows, :] = poolw[0, 16:n_rows, :] + poolw[0, 14:n_rows - 2, :]
        poolw[2, 24:n_rows, :] = poolw[1, 24:n_rows, :] + poolw[1, 20:n_rows - 4, :]
        w2 = poolw[0, POOL_PAD:n_rows, :]
        w4 = poolw[1, POOL_PAD:n_rows, :]
        w8 = poolw[2, POOL_PAD:n_rows, :]
        w16 = w8 + poolw[2, POOL_PAD - 8:n_rows - 8, :]
        pos = c * T + lax.broadcasted_iota(jnp.int32, (T, BW), 0)
        cnt = jnp.minimum(pos + 1, _pool_window_row()).astype(F32)
        dpool = _pool_select(w2, w4, w8, w16) / cnt - pu
        poolh[0:POOL_PAD, :] = poolh[T:n_rows, :]
        return _dot(dpool, wpool_ref[...]) * vec_ref[V_MISC:V_MISC + 1, 0:256]

    def conv_mix(xbc_raw):
        convh[SUBLANES:SUBLANES + T, :] = xbc_raw
        acc = vec_ref[V_CONVB:V_CONVB + 1, 0:CONV_DIM]
        for kk in range(CONV_K):
            acc = acc + convh[5 + kk:5 + kk + T, :] * vec_ref[V_CONVW + kk:V_CONVW + kk + 1, 0:CONV_DIM]
        convh[0:SUBLANES, :] = convh[T:T + SUBLANES, :]
        return _silu(acc)

    v_sm, sp_all, lsg, a_row = _small_block(jnp.dot(xb, wsm_ref[...], preferred_element_type=F32), vec_ref)
    lane_s = lax.broadcasted_iota(jnp.int32, (T, LANES), 1)
    g_all = jnp.where(lane_s < 4, sp_all * a_row, jnp.where(lane_s < 8, v_sm, jnp.where(lane_s < 12, lsg, 0.0)))
    tril = causal.astype(BF16)
    g_hi = g_all.astype(BF16)
    g_r1 = g_all - g_hi.astype(F32)
    g_mid = g_r1.astype(BF16)
    g_lo = (g_r1 - g_mid.astype(F32)).astype(BF16)
    cums = []
    for s in range(T // Q):
        rs = slice(s * Q, (s + 1) * Q)
        cum = (jnp.dot(tril, g_hi[rs], preferred_element_type=F32)
               + jnp.dot(tril, g_mid[rs], preferred_element_type=F32)
               + jnp.dot(tril, g_lo[rs], preferred_element_type=F32))
        cums.append((cum, cum.T, g_all[rs].T))

    proj = {}
    dyn0 = jnp.minimum(c, 0)

    half = D_MODEL // 2

    def gate(b, part):
        lo = part * half
        gsc[dyn0 + b, :, lo:lo + half] = jnp.tanh(post(O_GATE + b * D_MODEL + lo, half))

    fillers = [
        lambda: proj.update(pu=pre(O_PU, BW)),
        lambda: proj.update(xbc=pre(O_XBC, CONV_DIM)),
        lambda: proj.update(q=pre(O_Q, BW)),
        lambda: proj.update(k=pre(O_K, BW) * (HD ** -0.5)),
        lambda: proj.update(v=pre(O_V, BW)),
        lambda: proj.update(zs=[post(O_Z + b * BW, BW) for b in range(4)]),
        lambda: proj.update(og=post(O_OG, BW)),
    ] + [functools.partial(gate, b, part) for b in range(4) for part in range(2)]

    def su_piece(piece):
        return jnp.dot(xb[piece * Q:(piece + 1) * Q], wpre_ref[:, O_SU:O_SU + BW], preferred_element_type=F32)

    y_s5 = _s5_chunk_scan(su_piece, T, fillers, lamp_ref, bbd_ref, cbd_ref, vec_ref, s5cr, s5ci)
    zs = proj["zs"]
    y_pool = pool_mix(proj["pu"])
    xbc = conv_mix(proj["xbc"])
    gates = [gsc.at[dyn0 + b] for b in range(4)]

    y_ssd_parts, h_m_parts = [], []
    for s in range(T // Q):
        y_c, h_c = _ssd_mlstm_chunk(
            xbc[s * Q:(s + 1) * Q], proj["q"][s * Q:(s + 1) * Q], proj["k"][s * Q:(s + 1) * Q],
            proj["v"][s * Q:(s + 1) * Q], sp_all[s * Q:(s + 1) * Q], g_all[s * Q:(s + 1) * Q],
            cums[s], causal, vec_ref, ssd_st, mc_st, nrow, mrow)
        y_ssd_parts.append(y_c)
        h_m_parts.append(h_c)
    y_ssd = jnp.concatenate(y_ssd_parts, axis=0)
    y_m = _sigmoid(proj["og"]) * jnp.concatenate(h_m_parts, axis=0)
    y_s5 = _s5_glu(y_s5, wglu_ref)

    z_acts = [_silu_of_half(z) for z in zs]
    y_ref[0] = _merge_tail(x, z_acts, [y_pool, y_ssd, y_m, y_s5], gates, wbr_ref, wout_ref, vec_ref)

    @pl.when(c == last)
    def _():
        pool_o[0] = poolh[POOL_PAD - POOL_BUF:POOL_PAD, :]
        conv_o[0] = convh[5:8, :]
        ssd_o[0] = ssd_st[...]
        mc_o[0] = mc_st[...]
        mn_o[0] = nrow[0:1, :]
        mm_o[0] = mrow[0:1, :]
        s5re_o[0] = s5cr[0:1, :]
        s5im_o[0] = s5ci[0:1, :]


def _ssd_mlstm_chunk(xbc, q, k, v, sp, g_blk, cums, causal, vec_ref, ssd_st, mc_st, nrow, mrow):
    T = xbc.shape[0]
    cum, cum_t, g_t = cums
    xs = xbc[:, 0:BW]
    dt_cols = [sp[:, h:h + 1] for h in range(HEADS)]
    acs_cols = [cum[:, h:h + 1] for h in range(HEADS)]
    acs_last = [cum[T - 1:T, h:h + 1] for h in range(HEADS)]
    xdt = xs * _expand_heads(dt_cols)
    xw_t = (xdt * _expand_heads([jnp.exp(acs_last[h] - acs_cols[h]) for h in range(HEADS)])).T
    d_ssd = vec_ref[V_MISC:V_MISC + 1, 256:512]
    m_row = mrow[0:1, :]
    n_row = nrow[0:1, :]
    hsl = [slice(h * HD, (h + 1) * HD) for h in range(HEADS)]
    b_gs = [xbc[:, BW + g * SSD_STATE:BW + (g + 1) * SSD_STATE] for g in range(2)]
    c_gs = [xbc[:, BW + 128 + g * SSD_STATE:BW + 128 + (g + 1) * SSD_STATE] for g in range(2)]

    cbs = [_dot_nt(c_gs[g], b_gs[g]) for g in range(2)]
    off_all = jnp.concatenate(
        [_dot_nt(c_gs[g], ssd_st[2 * g:2 * g + 2].reshape(2 * HD, SSD_STATE)) for g in range(2)], axis=1)
    qks = [_dot_nt(q[:, hsl[h]], k[:, hsl[h]]) for h in range(HEADS)]
    qcs = [_dot(q[:, hsl[h]], mc_st[h]) for h in range(HEADS)]
    gms = [cbs[h // 2] * jnp.exp(jnp.where(causal, acs_cols[h] - cum_t[h:h + 1, :], -jnp.inf))
           for h in range(HEADS)]
    b_cols = [cum[:, 8 + h:9 + h] for h in range(HEADS)]
    m_prevs = [m_row[:, h:h + 1] for h in range(HEADS)]
    dlogs = [jnp.where(causal, b_cols[h] - cum_t[8 + h:9 + h, :] + g_t[4 + h:5 + h, :], -jnp.inf)
             for h in range(HEADS)]
    inters = [b_cols[h] + m_prevs[h] for h in range(HEADS)]
    m_ts = [jnp.maximum(inters[h], jnp.max(dlogs[h], axis=1, keepdims=True)) for h in range(HEADS)]
    ss = [qks[h] * jnp.exp(dlogs[h] - m_ts[h]) for h in range(HEADS)]
    wis = [jnp.exp(inters[h] - m_ts[h]) for h in range(HEADS)]
    y_diag_all = jnp.concatenate([_dot(gms[h], xdt[:, hsl[h]]) for h in range(HEADS)], axis=1)
    svs = [_dot(ss[h], v[:, hsl[h]]) for h in range(HEADS)]
    y_off = off_all * _expand_heads([jnp.exp(acs_cols[h]) for h in range(HEADS)])
    y_ssd = y_diag_all + y_off + d_ssd * xs
    h_heads, wk_parts, decays, m_news = [], [], [], []
    for h in range(HEADS):
        num = svs[h] + wis[h] * qcs[h]
        den = (jnp.sum(ss[h], axis=1, keepdims=True)
               + wis[h] * jnp.sum(q[:, hsl[h]] * n_row[:, hsl[h]], axis=1, keepdims=True))
        h_heads.append(num / jnp.maximum(jnp.abs(den), jnp.exp(-m_ts[h])))
        m_new = m_ts[h][T - 1:T, :]
        b_last = b_cols[h][T - 1:T, :]
        wk_parts.append(jnp.exp(b_last - b_cols[h] + g_blk[:, 4 + h:5 + h] - m_new))
        decays.append(jnp.exp(b_last + m_prevs[h] - m_new))
        m_news.append(m_new)
    for h in range(HEADS):
        ssd_st[h] = jnp.exp(acs_last[h]) * ssd_st[h] + _dot(xw_t[hsl[h], :], b_gs[h // 2])
    lane_m = lax.broadcasted_iota(jnp.int32, (1, LANES), 1)
    m_row_new = jnp.zeros((1, LANES), F32)
    for h in range(HEADS):
        m_row_new = jnp.where(lane_m == h, jnp.broadcast_to(m_news[h], (1, LANES)), m_row_new)
    mrow[0:1, :] = m_row_new
    wk = k * _expand_heads(wk_parts)
    wk_t = wk.T
    for h in range(HEADS):
        mc_st[h] = decays[h] * mc_st[h] + _dot(wk_t[h * HD:(h + 1) * HD, :], v[:, h * HD:(h + 1) * HD])
    nrow[0:1, :] = _expand_heads(decays) * n_row + jnp.sum(wk, axis=0, keepdims=True)
    return y_ssd, jnp.concatenate(h_heads, axis=1)


def _s5_chunk_scan(su_piece, T, fillers, lamp_ref, bbd_ref, cbd_ref, vec_ref, s5cr, s5ci):
    p_re = lamp_ref[R_P:R_P + SUBLANES, :]
    p_im = lamp_ref[R_P + SUBLANES:R_P + 2 * SUBLANES, :]
    cr = s5cr[0:1, :]
    ci = s5ci[0:1, :]
    blk_re = []
    blk_im = []
    pieces = T // CHUNK
    nblk = CHUNK // SUBLANES
    slots = 3 * pieces
    filler_at = {}
    for i, f in enumerate(fillers):
        filler_at.setdefault((i * slots) // len(fillers), []).append(f)
    in_block = lax.broadcasted_iota(jnp.int32, (SUBLANES, BW), 0) >= 1
    su_parts = []
    for piece in range(pieces):
        su_p = su_piece(piece)
        su_parts.append(su_p)
        su_prev = jnp.where(in_block, pltpu.roll(su_p.reshape(nblk, SUBLANES, BW), 1, 1), 0.0).reshape(CHUNK, BW)
        bu = _dot(jnp.concatenate([su_p, su_prev], axis=1), bbd_ref[...])
        h_re = bu[:, 0:S5_W].reshape(nblk, SUBLANES, S5_W)
        h_im = bu[:, S5_W:2 * S5_W].reshape(nblk, SUBLANES, S5_W)
        for i, d in ((0, 2), (1, 4)):
            base = R_M + 16 * (i + 1)
            m_re = lamp_ref[base:base + SUBLANES, :]
            m_im = lamp_ref[base + SUBLANES:base + 2 * SUBLANES, :]
            r_re = pltpu.roll(h_re, d, 1)
            r_im = pltpu.roll(h_im, d, 1)
            h_re, h_im = h_re + (m_re * r_re - m_im * r_im), h_im + (m_re * r_im + m_im * r_re)
            for f in filler_at.get(3 * piece + i, ()):
                f()
        for j in range(nblk):
            br = h_re[j] + (p_re * cr - p_im * ci)
            bi = h_im[j] + (p_re * ci + p_im * cr)
            blk_re.append(br)
            blk_im.append(bi)
            cr = br[SUBLANES - 1:SUBLANES, :]
            ci = bi[SUBLANES - 1:SUBLANES, :]
        for f in filler_at.get(3 * piece + 2, ()):
            f()
    s5cr[0:1, :] = cr
    s5ci[0:1, :] = ci
    h_all = jnp.concatenate([jnp.concatenate(blk_re, axis=0), jnp.concatenate(blk_im, axis=0)], axis=1)
    return _dot(h_all, cbd_ref[...]) + vec_ref[V_MISC:V_MISC + 1, 512:768] * jnp.concatenate(su_parts, axis=0)


def _const_spec(shape, layer):
    nd = len(shape)
    return pl.BlockSpec((None,) + tuple(shape), lambda *_: (layer,) + (0,) * nd,
                        pipeline_mode=pl.Buffered(1))


def _w_spec(width, block, layer):
    return pl.BlockSpec((None, D_MODEL, width), lambda *_: (layer, 0, block), pipeline_mode=pl.Buffered(1))


def _prompt_layer(x, layer, p):
    bsz, seq, _ = x.shape
    T = TILE if seq % TILE == 0 else CHUNK
    nc = seq // T

    def bmap(nd):
        return lambda b, c: (b,) + (0,) * nd

    out_shape = (
        jax.ShapeDtypeStruct((bsz, seq, D_MODEL), F32),
        jax.ShapeDtypeStruct((bsz, POOL_BUF, BW), F32),
        jax.ShapeDtypeStruct((bsz, CONV_K - 1, CONV_DIM), F32),
        jax.ShapeDtypeStruct((bsz, HEADS, HD, SSD_STATE), F32),
        jax.ShapeDtypeStruct((bsz, HEADS, HD, HD), F32),
        jax.ShapeDtypeStruct((bsz, 1, BW), F32),
        jax.ShapeDtypeStruct((bsz, 1, LANES), F32),
        jax.ShapeDtypeStruct((bsz, 1, S5_W), F32),
        jax.ShapeDtypeStruct((bsz, 1, S5_W), F32),
    )
    out_specs = (
        pl.BlockSpec((1, T, D_MODEL), lambda b, c: (b, c, 0)),
        pl.BlockSpec((1, POOL_BUF, BW), bmap(2)),
        pl.BlockSpec((1, CONV_K - 1, CONV_DIM), bmap(2)),
        pl.BlockSpec((1, HEADS, HD, SSD_STATE), bmap(3)),
        pl.BlockSpec((1, HEADS, HD, HD), bmap(3)),
        pl.BlockSpec((1, 1, BW), bmap(2)),
        pl.BlockSpec((1, 1, LANES), bmap(2)),
        pl.BlockSpec((1, 1, S5_W), bmap(2)),
        pl.BlockSpec((1, 1, S5_W), bmap(2)),
    )
    in_specs = [
        pl.BlockSpec((1, T, D_MODEL), lambda b, c: (b, c, 0)),
        _w_spec(PRE_W, PRE_BLOCK, layer),
        _w_spec(POST_W, 0, layer),
        _const_spec((D_MODEL, LANES), layer),
        _const_spec((16, D_MODEL), layer),
        _const_spec((BW, BW), layer),
        _const_spec((LAMP_ROWS, S5_W), layer),
        _const_spec((2 * BW, 2 * S5_W), layer),
        _const_spec((2 * S5_W, BW), layer),
        _const_spec((BW, 2 * BW), layer),
        _const_spec((4, BW, D_MODEL), layer),
        _const_spec((D_MODEL, D_MODEL), layer),
    ]
    scratch = [
        pltpu.VMEM((T + POOL_PAD, BW), F32),
        pltpu.VMEM((3, T + POOL_PAD, BW), F32),
        pltpu.VMEM((T + SUBLANES, CONV_DIM), F32),
        pltpu.VMEM((HEADS, HD, SSD_STATE), F32),
        pltpu.VMEM((HEADS, HD, HD), F32),
        pltpu.VMEM((SUBLANES, BW), F32),
        pltpu.VMEM((SUBLANES, LANES), F32),
        pltpu.VMEM((SUBLANES, S5_W), F32),
        pltpu.VMEM((SUBLANES, S5_W), F32),
        pltpu.VMEM((4, T, D_MODEL), F32),
    ]
    return pl.pallas_call(
        _prompt_kernel,
        grid=(bsz, nc),
        in_specs=in_specs,
        out_specs=out_specs,
        out_shape=out_shape,
        scratch_shapes=scratch,
        compiler_params=pltpu.CompilerParams(dimension_semantics=("arbitrary", "arbitrary"),
                                             vmem_limit_bytes=VMEM_LIMIT),
        name="prompt_layer",
    )(x, p["wall"], p["wall"], p["wsm"], p["vec"], p["wpool"], p["lamp"], p["bbd"], p["cbd"],
      p["wglu"], p["wbr"], p["wout"])


RW_BC, RW_XDT, RW_Q, RW_WK, RW_V, RW_SDEC, RW_MDEC, RW_W = 0, 256, 512, 768, 1024, 1280, 1408, 1536
MID_W = 7 * BW


def _sample_pre_kernel(x_ref, wpre_ref, wsm_ref, vec_ref, wpool_ref, lamp_ref, bbd_ref, cbd_ref,
                       pool_ref, conv_ref, n_ref, m_ref, s5re_ref, s5im_ref,
                       rows_o, mid_o, pool_o, conv_o, n_o, m_o, s5re_o, s5im_o):
    x = x_ref[...]
    nb = x.shape[0]
    pp = _dot(x, wpre_ref[...])
    pu = pp[:, O_PU:O_PU + BW]
    xbc_raw = pp[:, O_XBC:O_XBC + CONV_DIM]
    q = pp[:, O_Q:O_Q + BW]
    k = pp[:, O_K:O_K + BW] * (HD ** -0.5)
    v = pp[:, O_V:O_V + BW]
    su = pp[:, O_SU:O_SU + BW]

    def ext(j):
        return pool_ref[j]

    w2 = pu + ext(14)
    w4 = w2 + ext(13) + ext(12)
    w8 = w4
    for j in range(11, 7, -1):
        w8 = w8 + ext(j)
    w16 = w8
    for j in range(7, -1, -1):
        w16 = w16 + ext(j)
    cnt = jnp.minimum(PAST_LEN + 1, _pool_window_row()).astype(F32)
    dpool = _pool_select(w2, w4, w8, w16) / cnt - pu
    y_pool = _dot(dpool, wpool_ref[...]) * vec_ref[V_MISC:V_MISC + 1, 0:256]
    for j in range(POOL_BUF - 1):
        pool_o[j] = pool_ref[j + 1]
    pool_o[POOL_BUF - 1] = pu

    acc = vec_ref[V_CONVB:V_CONVB + 1, 0:CONV_DIM]
    for kk in range(CONV_K - 1):
        acc = acc + conv_ref[kk] * vec_ref[V_CONVW + kk:V_CONVW + kk + 1, 0:CONV_DIM]
    acc = acc + xbc_raw * vec_ref[V_CONVW + CONV_K - 1:V_CONVW + CONV_K, 0:CONV_DIM]
    for kk in range(CONV_K - 2):
        conv_o[kk] = conv_ref[kk + 1]
    conv_o[CONV_K - 2] = xbc_raw
    xbc = _silu(acc)
    xs = xbc[:, 0:BW]
    bc = xbc[:, BW:2 * BW]

    v_sm, sp, lsg, a_row = _small_block(_dot(x, wsm_ref[...]), vec_ref)
    sdec_blk = jnp.exp(sp * a_row)
    dt_cols = [sp[:, h:h + 1] for h in range(HEADS)]
    xdt = xs * _expand_heads(dt_cols)
    cb_cols = []
    for g in range(2):
        b_g = bc[:, g * SSD_STATE:(g + 1) * SSD_STATE]
        c_g = bc[:, 128 + g * SSD_STATE:128 + (g + 1) * SSD_STATE]
        cb_cols.append(jnp.sum(c_g * b_g, axis=1, keepdims=True))
    yssd_part = (_expand_heads([cb_cols[h // 2] for h in range(HEADS)]) * xdt
                 + vec_ref[V_MISC:V_MISC + 1, 256:512] * xs)
    sdec_exp = _expand_heads([sdec_blk[:, h:h + 1] for h in range(HEADS)])

    n0 = n_ref[...].T
    m0 = jnp.concatenate([m_ref[...], jnp.zeros((SUBLANES - HEADS, nb), F32)], axis=0).T
    s_cols, wi_cols, dw_cols, dmax_cols, m_cols = [], [], [], [], []
    for h in range(HEADS):
        sl = slice(h * HD, (h + 1) * HD)
        ig = v_sm[:, 4 + h:5 + h]
        lf = lsg[:, 8 + h:9 + h]
        inter = lf + m0[:, h:h + 1]
        m_t = jnp.maximum(inter, (lf - lf) + ig)
        dw = jnp.exp((lf - lf) + ig - m_t)
        wi = jnp.exp(inter - m_t)
        s = jnp.sum(q[:, sl] * k[:, sl], axis=1, keepdims=True) * dw
        den = s + wi * jnp.sum(q[:, sl] * n0[:, sl], axis=1, keepdims=True)
        s_cols.append(s)
        wi_cols.append(wi)
        dw_cols.append(dw)
        dmax_cols.append(jnp.maximum(jnp.abs(den), jnp.exp(-m_t)))
        m_cols.append(m_t)
    wi_exp = _expand_heads(wi_cols)
    wk = _expand_heads(dw_cols) * k
    n_o[...] = (wi_exp * n0 + wk).T
    lane = lax.broadcasted_iota(jnp.int32, (nb, LANES), 1)
    m_blk = jnp.zeros((nb, LANES), F32)
    wi_blk = jnp.zeros((nb, LANES), F32)
    for h in range(HEADS):
        m_blk = jnp.where(lane == h, jnp.broadcast_to(m_cols[h], (nb, LANES)), m_blk)
        wi_blk = jnp.where(lane == h, jnp.broadcast_to(wi_cols[h], (nb, LANES)), wi_blk)
    m_o[...] = m_blk.T[0:HEADS, :]

    bu = _dot(su, bbd_ref[0:BW, :])
    lbr = lamp_ref[R_LAM:R_LAM + 1, :]
    lbi = lamp_ref[R_LAM + 1:R_LAM + 2, :]
    h0r = s5re_ref[...].T
    h0i = s5im_ref[...].T
    h_re = bu[:, 0:S5_W] + (lbr * h0r - lbi * h0i)
    h_im = bu[:, S5_W:2 * S5_W] + (lbr * h0i + lbi * h0r)
    s5re_o[...] = h_re.T
    s5im_o[...] = h_im.T
    y_s5 = _dot(jnp.concatenate([h_re, h_im], axis=1), cbd_ref[...]) + vec_ref[V_MISC:V_MISC + 1, 512:768] * su

    rows_o[...] = jnp.concatenate([bc, xdt, q, wk, v, sdec_blk, wi_blk], axis=1).T
    mid_o[...] = jnp.concatenate([y_pool, yssd_part, sdec_exp, _expand_heads(s_cols) * v, wi_exp,
                                  _expand_heads(dmax_cols), y_s5], axis=1)


def _sample_pre(x, layer, p, pool2, conv2, n2, m2, s5re2, s5im2):
    nb = x.shape[0]
    full = lambda a: pl.BlockSpec(a.shape, lambda *_: (0,) * a.ndim)
    of_layer = lambda a: pl.BlockSpec((None,) + a.shape[1:], lambda *_: (layer,) + (0,) * (a.ndim - 1))
    out_shape = (
        jax.ShapeDtypeStruct((RW_W, nb), F32),
        jax.ShapeDtypeStruct((nb, MID_W), F32),
        jax.ShapeDtypeStruct(pool2.shape[1:], F32),
        jax.ShapeDtypeStruct(conv2.shape[1:], F32),
        jax.ShapeDtypeStruct((BW, nb), F32),
        jax.ShapeDtypeStruct((HEADS, nb), F32),
        jax.ShapeDtypeStruct((S5_W, nb), F32),
        jax.ShapeDtypeStruct((S5_W, nb), F32),
    )
    in_specs = [
        full(x),
        _w_spec(PRE_W, PRE_BLOCK, layer),
        _const_spec((D_MODEL, LANES), layer),
        _const_spec((16, D_MODEL), layer),
        _const_spec((BW, BW), layer),
        _const_spec((LAMP_ROWS, S5_W), layer),
        _const_spec((2 * BW, 2 * S5_W), layer),
        _const_spec((2 * S5_W, BW), layer),
        of_layer(pool2), of_layer(conv2), of_layer(n2), of_layer(m2), of_layer(s5re2), of_layer(s5im2),
    ]
    return pl.pallas_call(
        _sample_pre_kernel,
        grid=(1,),
        in_specs=in_specs,
        out_specs=tuple(pl.BlockSpec(s.shape, lambda *_, nd=len(s.shape): (0,) * nd) for s in out_shape),
        out_shape=out_shape,
        compiler_params=pltpu.CompilerParams(dimension_semantics=("arbitrary",),
                                             vmem_limit_bytes=VMEM_LIMIT),
        name="sample_pre",
    )(x, p["wall"], p["wsm"], p["vec"], p["wpool"], p["lamp"], p["bbd"], p["cbd"],
      pool2, conv2, n2, m2, s5re2, s5im2)


def _sample_state_kernel(first_layer, rows_ref, ssd_ref, mc_ref, *rest):
    h = pl.program_id(0)
    part = pl.program_id(1)
    if first_layer:
        yoff_ref, qc_ref, ssd_o, mc_o = rest
        ssd_o[1] = ssd_ref[1]
        mc_o[1] = mc_ref[1]
        ssd_in, mc_in, ssd_out, mc_out = ssd_ref.at[0], mc_ref.at[0], ssd_o.at[0], mc_o.at[0]
    else:
        _, _, yoff_ref, qc_ref, ssd_out, mc_out = rest
        ssd_in, mc_in = ssd_ref, mc_ref
    g64 = pl.multiple_of((h // 2) * SSD_STATE, SSD_STATE)
    h64 = pl.multiple_of(h * HD, HD)
    b_t = rows_ref[pl.ds(RW_BC + g64, SSD_STATE), :]
    c_t = rows_ref[pl.ds(RW_BC + 128 + g64, SSD_STATE), :]
    v_t = rows_ref[pl.ds(RW_V + h64, HD), :]
    sdec = rows_ref[pl.ds(RW_SDEC + h, 1), :]
    mdec = rows_ref[pl.ds(RW_MDEC + h, 1), :]
    base = h64 + part * STATE_ROWS

    def body(i, acc):
        hs = ssd_in[i]
        yoff_ref[pl.ds(i, 1), :] = jnp.sum(hs * c_t, axis=0, keepdims=True)
        ssd_out[i] = sdec * hs + rows_ref[pl.ds(RW_XDT + base + i, 1), :] * b_t
        cs = mc_in[i]
        mc_out[i] = mdec * cs + rows_ref[pl.ds(RW_WK + base + i, 1), :] * v_t
        return acc + rows_ref[pl.ds(RW_Q + base + i, 1), :] * cs

    acc = lax.fori_loop(0, STATE_ROWS, body, jnp.zeros((HD, rows_ref.shape[1]), F32), unroll=4)

    @pl.when(part == 0)
    def _():
        qc_ref[...] = acc

    @pl.when(part != 0)
    def _():
        qc_ref[...] = qc_ref[...] + acc


STATE_ROWS = 32


def _sample_state(rows_t, layer, ssd_all, mc_all, prev=None):
    nb = rows_t.shape[1]
    depth = ssd_all.shape[0]
    parts = HD // STATE_ROWS
    tile = (STATE_ROWS, HD, nb)
    rows_spec = pl.BlockSpec(rows_t.shape, lambda h, s: (0, 0))
    yoff_spec = pl.BlockSpec((STATE_ROWS, nb), lambda h, s: (h * parts + s, 0))
    qc_spec = pl.BlockSpec((HD, nb), lambda h, s: (h, 0))
    out_shape = (jax.ShapeDtypeStruct((BW, nb), F32),
                 jax.ShapeDtypeStruct((BW, nb), F32),
                 jax.ShapeDtypeStruct(ssd_all.shape, F32),
                 jax.ShapeDtypeStruct(mc_all.shape, F32))
    params = pltpu.CompilerParams(dimension_semantics=("arbitrary", "arbitrary"),
                                  vmem_limit_bytes=VMEM_LIMIT)
    if prev is None:
        assert layer == 0 and depth == 2
        both = pl.BlockSpec((depth, None) + tile, lambda h, s: (0, h, s, 0, 0))
        return pl.pallas_call(
            functools.partial(_sample_state_kernel, True),
            grid=(HEADS, parts),
            in_specs=[rows_spec, both, both],
            out_specs=(yoff_spec, qc_spec, both, both),
            out_shape=out_shape,
            compiler_params=params,
            name="sample_state_first",
        )(rows_t, ssd_all, mc_all)
    one = pl.BlockSpec((None, None) + tile, lambda h, s: (layer, h, s, 0, 0))
    anywhere = pl.BlockSpec(memory_space=pl.ANY)
    return pl.pallas_call(
        functools.partial(_sample_state_kernel, False),
        grid=(HEADS, parts),
        in_specs=[rows_spec, one, one, anywhere, anywhere],
        out_specs=(yoff_spec, qc_spec, one, one),
        out_shape=out_shape,
        input_output_aliases={3: 2, 4: 3},
        compiler_params=params,
        name="sample_state_next",
    )(rows_t, ssd_all, mc_all, prev[0], prev[1])


def _sample_post_kernel(x_ref, wpost_ref, vec_ref, mid_ref, yoff_ref, qc_ref, wglu_ref, wbr_ref, wout_ref, y_ref):
    x = x_ref[...]
    xb = x.astype(BF16)

    def post(lo, width):
        return jnp.dot(xb, wpost_ref[:, lo:lo + width], preferred_element_type=F32)

    def mid(i):
        return mid_ref[:, i * BW:(i + 1) * BW]

    y_pool, yssd_part, sdec_exp, sv, wi_exp, dmax_exp, y_s5 = [mid(i) for i in range(7)]
    y_ssd = yssd_part + yoff_ref[...].T * sdec_exp
    hh = (sv + wi_exp * qc_ref[...].T) / dmax_exp
    y_m = _sigmoid(post(O_OG, BW)) * hh
    y_s5 = _s5_glu(y_s5, wglu_ref)
    z_acts = [_silu_of_half(post(O_Z + b * BW, BW)) for b in range(4)]
    gates = [jnp.tanh(post(O_GATE + b * D_MODEL, D_MODEL)) for b in range(4)]
    y_ref[...] = _merge_tail(x, z_acts, [y_pool, y_ssd, y_m, y_s5], gates, wbr_ref, wout_ref, vec_ref)


def _sample_post(x, layer, p, mid, yoff_t, qc_t):
    nb = x.shape[0]
    full = lambda a: pl.BlockSpec(a.shape, lambda *_: (0,) * a.ndim)
    return pl.pallas_call(
        _sample_post_kernel,
        grid=(1,),
        in_specs=[full(x), _w_spec(POST_W, 0, layer), _const_spec((16, D_MODEL), layer),
                  full(mid), full(yoff_t), full(qc_t), _const_spec((BW, 2 * BW), layer),
                  _const_spec((4, BW, D_MODEL), layer), _const_spec((D_MODEL, D_MODEL), layer)],
        out_specs=pl.BlockSpec((nb, D_MODEL), lambda *_: (0, 0)),
        out_shape=jax.ShapeDtypeStruct((nb, D_MODEL), F32),
        compiler_params=pltpu.CompilerParams(dimension_semantics=("arbitrary",),
                                             vmem_limit_bytes=VMEM_LIMIT),
        name="sample_post",
    )(x, p["wall"], p["vec"], mid, yoff_t, qc_t, p["wglu"], p["wbr"], p["wout"])


def _w_permute_kernel(src_ref, halved_ref, w_hbm, out_ref, buf, sem):
    i = pl.program_id(0)
    depth = out_ref.shape[0]

    def copies(step, slot):
        start = src_ref[step]
        return [pltpu.make_async_copy(w_hbm.at[pl.ds(start, WCOLS), l, :], buf.at[slot, l], sem.at[slot, l])
                for l in range(depth)]

    @pl.when(i == 0)
    def _():
        for cp in copies(0, 0):
            cp.start()

    @pl.when(i + 1 < pl.num_programs(0))
    def _():
        for cp in copies(i + 1, (i + 1) % 2):
            cp.start()

    for cp in copies(i, i % 2):
        cp.wait()
    scale = jnp.where(halved_ref[i] == 1, 0.5, 1.0).astype(F32)
    for l in range(depth):
        out_ref[l] = (buf[i % 2, l].T * scale).astype(BF16)


def _w_permute(w_in):
    depth, rows, _ = w_in.shape
    w_t = jnp.transpose(w_in, (2, 0, 1))
    nblk = len(W_SRC)
    return pl.pallas_call(
        _w_permute_kernel,
        grid_spec=pltpu.PrefetchScalarGridSpec(
            num_scalar_prefetch=2, grid=(nblk,),
            in_specs=[pl.BlockSpec(memory_space=pl.ANY)],
            out_specs=pl.BlockSpec((depth, rows, WCOLS), lambda i, src, halved: (0, 0, i)),
            scratch_shapes=[pltpu.VMEM((2, depth, WCOLS, rows), F32), pltpu.SemaphoreType.DMA((2, depth))]),
        out_shape=jax.ShapeDtypeStruct((depth, rows, nblk * WCOLS), BF16),
        compiler_params=pltpu.CompilerParams(dimension_semantics=("arbitrary",)),
        name="w_permute",
    )(jnp.asarray(W_SRC, jnp.int32), jnp.asarray(W_HALVED, jnp.int32), w_t)


def _prepare_params(w_in, w_pool, pool_scale, conv_w, conv_b, dt_bias, a_log, d_ssd, ig_bias, fg_bias,
                    lam_re, lam_im, b_re, b_im, c_re, c_im, log_dt, d_s5, w_glu, w_br, w_out, ln_g, ln_b):
    depth = w_in.shape[0]
    wall = _w_permute(w_in)
    small_cols = [w_in[:, :, lo:lo + n] for lo, n in SMALL_SRC]
    wsm = jnp.concatenate(small_cols + [jnp.zeros((depth, D_MODEL, LANES - 12), w_in.dtype)], axis=2).astype(BF16)

    zeros = lambda n: jnp.zeros((depth, n), F32)
    misc = jnp.concatenate([pool_scale, jnp.repeat(d_ssd, HD, axis=1), d_s5,
                            dt_bias, ig_bias, fg_bias, zeros(LANES - 12),
                            a_log, zeros(LANES - HEADS)], axis=1)
    pad512 = lambda a: jnp.concatenate([a, jnp.zeros(a.shape[:-1] + (D_MODEL - CONV_DIM,), F32)], axis=-1)
    vec = jnp.concatenate([ln_g[:, None], ln_b[:, None], misc[:, None], pad512(conv_b)[:, None],
                           pad512(conv_w), jnp.zeros((depth, 16 - 4 - CONV_K, D_MODEL), F32)], axis=1)

    eye4 = jnp.eye(len(POOL_WINDOWS), dtype=F32)
    wpool = jnp.einsum("lgce,gh->lgche", w_pool, eye4).reshape(depth, BW, BW).astype(BF16)

    lam2 = jnp.stack([lam_re.reshape(depth, S5_W), lam_im.reshape(depth, S5_W)], axis=1)
    ldt = jnp.repeat(log_dt, S5_STATE, axis=1)[:, None]
    to_rows = lambda b: jnp.transpose(b, (0, 3, 1, 2)).reshape(depth, S5_CH, S5_W)
    bcat = jnp.concatenate([to_rows(b_re), to_rows(b_im)], axis=1)

    eye_g = jnp.eye(S5_GROUPS, dtype=F32)
    cb = lambda cm: jnp.einsum("lgcp,gh->lgphc", cm, eye_g).reshape(depth, S5_W, BW)
    cbd = jnp.concatenate([cb(c_re), -cb(c_im)], axis=1).astype(BF16)

    lamp, bbd = [], []
    for l in range(depth):
        lp, bb = _s5_prep(lam2[l], ldt[l], bcat[l])
        lamp.append(lp)
        bbd.append(bb)
    return dict(wall=wall, wsm=wsm, vec=vec, wpool=wpool, lamp=jnp.stack(lamp), bbd=jnp.stack(bbd),
                cbd=cbd, wglu=w_glu.astype(BF16), wbr=w_br.astype(BF16), wout=(0.5 * w_out).astype(BF16))


def kernel(x_prompt, x_sample, state_pool, state_ssd_conv, state_ssd, state_mlstm_C, state_mlstm_n, state_mlstm_m, state_s5_re, state_s5_im, w_in, w_pool, pool_scale, conv_w, conv_b, dt_bias, a_log, d_ssd, ig_bias, fg_bias, lam_re, lam_im, b_re, b_im, c_re, c_im, log_dt, d_s5, w_glu, w_br, w_out, ln_g, ln_b):
    p = _prepare_params(w_in, w_pool, pool_scale, conv_w, conv_b, dt_bias, a_log, d_ssd, ig_bias, fg_bias,
                        lam_re, lam_im, b_re, b_im, c_re, c_im, log_dt, d_s5, w_glu, w_br, w_out, ln_g, ln_b)
    depth = w_in.shape[0]
    bsz = x_prompt.shape[0]
    nb = x_sample.shape[0]
    yp = x_prompt
    ys = x_sample.reshape(nb, D_MODEL)
    outs_p = [[] for _ in range(8)]
    outs_s = [[] for _ in range(8)]
    mats_s = None
    pool_t = jnp.transpose(state_pool, (0, 2, 1, 3))
    conv_t = jnp.transpose(state_ssd_conv, (0, 2, 1, 3))
    ssd_t = jnp.transpose(state_ssd, (0, 2, 3, 4, 1))
    mc_t = jnp.transpose(state_mlstm_C, (0, 2, 3, 4, 1))
    n_t = jnp.transpose(state_mlstm_n, (0, 2, 3, 1)).reshape(depth, BW, nb)
    m_t = jnp.transpose(state_mlstm_m, (0, 2, 1))
    re_t = jnp.transpose(state_s5_re, (0, 2, 3, 1)).reshape(depth, S5_W, nb)
    im_t = jnp.transpose(state_s5_im, (0, 2, 3, 1)).reshape(depth, S5_W, nb)
    for l in range(depth):
        yp, pool_p, conv_p, ssd_p, mc_p, mn_p, mm_p, re_p, im_p = _prompt_layer(yp, l, p)
        new_p = (pool_p, conv_p, ssd_p, mc_p, mn_p.reshape(bsz, HEADS, HD), mm_p[:, 0, :HEADS],
                 re_p.reshape(bsz, S5_GROUPS, S5_STATE), im_p.reshape(bsz, S5_GROUPS, S5_STATE))

        rows_t, mid, pool_s, conv_s, n_s, m_s, re_s, im_s = _sample_pre(
            ys, l, p, pool_t, conv_t, n_t, m_t, re_t, im_t)
        yoff_t, qc_t, ssd_s, mc_s = _sample_state(rows_t, l, ssd_t, mc_t, mats_s)
        mats_s = (ssd_s, mc_s)
        ys = _sample_post(ys, l, p, mid, yoff_t, qc_t)
        new_s = (pool_s, conv_s, None, None, n_s, m_s, re_s, im_s)
        for i in range(8):
            outs_p[i].append(new_p[i])
            outs_s[i].append(new_s[i])
    sp = [jnp.stack(o, axis=0) for o in outs_p]
    st = [mats_s[i - 2] if i in (2, 3) else jnp.stack(outs_s[i], axis=0) for i in range(8)]
    ss = [jnp.transpose(st[0], (0, 2, 1, 3)), jnp.transpose(st[1], (0, 2, 1, 3)),
          jnp.transpose(st[2], (0, 4, 1, 2, 3)), jnp.transpose(st[3], (0, 4, 1, 2, 3)),
          jnp.transpose(st[4].reshape(depth, HEADS, HD, nb), (0, 3, 1, 2)),
          jnp.transpose(st[5], (0, 2, 1)),
          jnp.transpose(st[6].reshape(depth, S5_GROUPS, S5_STATE, nb), (0, 3, 1, 2)),
          jnp.transpose(st[7].reshape(depth, S5_GROUPS, S5_STATE, nb), (0, 3, 1, 2))]
    out = [yp, ys.reshape(nb, 1, D_MODEL)]
    for i in range(8):
        out.append(sp[i])
        out.append(ss[i])
    return tuple(out)
```
